```python
import jax, jax.numpy as jnp
from jax import lax
import numpy as np

D_MODEL = 1024
BATCH = 8
SEQ = 4096
DEPTH = 1

MEM_LEN = 256
GRID_W = 64
ROPE_THETA = 10000.0
Q_BLOCK = 128
RMS_EPS = 1e-6
LN_EPS = 1e-5

GQA_HEADS = 8
GQA_KV_HEADS = 2
GQA_HEAD_DIM = D_MODEL // 16
MLA_HEADS = 8
MLA_NOPE_DIM = D_MODEL // 16
MLA_ROPE_DIM = D_MODEL // 32
MLA_V_DIM = D_MODEL // 16
MLA_Q_LORA = 3 * D_MODEL // 8
MLA_KV_LORA = D_MODEL // 4
MEM_HEADS = 4
MEM_HEAD_DIM = D_MODEL // 8
N_BRANCH = 3

GQA_Q_W = GQA_HEADS * GQA_HEAD_DIM
GQA_KV_W = GQA_KV_HEADS * GQA_HEAD_DIM
MLA_OUT_W = MLA_HEADS * MLA_V_DIM
MEM_Q_W = MEM_HEADS * MEM_HEAD_DIM
GATE_W = N_BRANCH * D_MODEL
SPLIT_SIZES = (GQA_Q_W, GQA_KV_W, GQA_KV_W, MLA_Q_LORA, MLA_KV_LORA, MLA_ROPE_DIM, MEM_Q_W, GATE_W)
SPLIT_OFFSETS = tuple(int(v) for v in np.cumsum(SPLIT_SIZES)[:-1])
IN_PROJ_W = sum(SPLIT_SIZES)

N_EXPERTS = 32
TOP_K = 4
D_EXPERT = D_MODEL
SWIGLU_LIMIT = 7.0
SWIGLU_ALPHA = 1.702
EXPERT_BLOCK = 128

DEEPNORM_ALPHA = (2 * DEPTH) ** 0.25
DEEPNORM_BETA = (8 * DEPTH) ** -0.25

kernel_name = "hybrid_gqa_mla_mem_moe_deepnorm_encoder"


def layer_norm(x, g, b):
    xf = x.astype(jnp.float32)
    mu = jnp.mean(xf, axis=-1, keepdims=True)
    xc = xf - mu
    var = jnp.mean(xc * xc, axis=-1, keepdims=True)
    return (xc * lax.rsqrt(var + LN_EPS) * g + b).astype(x.dtype)


def rms_norm(x, g):
    xf = x.astype(jnp.float32)
    return (xf * lax.rsqrt(jnp.mean(xf * xf, axis=-1, keepdims=True) + RMS_EPS) * g).astype(x.dtype)


def rope_1d(x, pos):
    half = x.shape[-1] // 2
    inv = ROPE_THETA ** (-jnp.arange(half, dtype=jnp.float32) / half)
    ang = pos.astype(jnp.float32)[:, None] * inv[None, :]
    c = jnp.cos(ang)[None, :, None, :]
    s = jnp.sin(ang)[None, :, None, :]
    x1 = x[..., :half].astype(jnp.float32)
    x2 = x[..., half:].astype(jnp.float32)
    return jnp.concatenate([x1 * c - x2 * s, x2 * c + x1 * s], axis=-1).astype(x.dtype)


def axial_rope(x, row, col):
    d = x.shape[-1] // 2
    return jnp.concatenate([rope_1d(x[..., :d], row), rope_1d(x[..., d:], col)], axis=-1)


def blocked_attention(q, k, v):
    B, S, Hk, G, Dq = q.shape
    nqb = S // Q_BLOCK
    qb = (q * (Dq ** -0.5)).reshape(B, nqb, Q_BLOCK, Hk, G, Dq).transpose(1, 0, 2, 3, 4, 5)

    def attend(qblk):
        s = jnp.einsum('bqhgd,bthd->bhgqt', qblk, k, preferred_element_type=jnp.float32)
        p = jax.nn.softmax(s, axis=-1).astype(v.dtype)
        return jnp.einsum('bhgqt,bthe->bqhge', p, v)

    o = lax.map(attend, qb)
    return o.transpose(1, 0, 2, 3, 4, 5).reshape(B, S, Hk, G, v.shape[-1])


def hybrid_mixer(h, mem, row, col, w_in_proj, b_gate, gqa_q_norm, gqa_k_norm, mla_q_norm,
                 mla_kv_norm, w_mla_qb, w_mla_kvb, w_mem_kv, w_br_gqa, w_br_mla, w_br_mem, w_out):
    B, S, D = h.shape
    proj = h @ w_in_proj
    q_a, k_a, v_a, q_lat, kv_lat, k_rope, q_m, gate_pre = jnp.split(proj, SPLIT_OFFSETS, axis=-1)

    qa = axial_rope(rms_norm(q_a.reshape(B, S, GQA_HEADS, GQA_HEAD_DIM), gqa_q_norm), row, col)
    ka = axial_rope(rms_norm(k_a.reshape(B, S, GQA_KV_HEADS, GQA_HEAD_DIM), gqa_k_norm), row, col)
    va = v_a.reshape(B, S, GQA_KV_HEADS, GQA_HEAD_DIM)
    group = GQA_HEADS // GQA_KV_HEADS
    oa = blocked_attention(qa.reshape(B, S, GQA_KV_HEADS, group, GQA_HEAD_DIM), ka, va).reshape(B, S, GQA_Q_W)

    qm = (rms_norm(q_lat, mla_q_norm) @ w_mla_qb).reshape(B, S, MLA_HEADS, MLA_NOPE_DIM + MLA_ROPE_DIM)
    q_nope, q_pe = qm[..., :MLA_NOPE_DIM], axial_rope(qm[..., MLA_NOPE_DIM:], row, col)
    kv = (rms_norm(kv_lat, mla_kv_norm) @ w_mla_kvb).reshape(B, S, MLA_HEADS, MLA_NOPE_DIM + MLA_V_DIM)
    k_nope, v_b = kv[..., :MLA_NOPE_DIM], kv[..., MLA_NOPE_DIM:]
    k_pe = axial_rope(k_rope[:, :, None, :], row, col)
    k_b = jnp.concatenate([k_nope, jnp.broadcast_to(k_pe, (B, S, MLA_HEADS, MLA_ROPE_DIM))], axis=-1)
    q_b = jnp.concatenate([q_nope, q_pe], axis=-1)[:, :, :, None, :]
    ob = blocked_attention(q_b, k_b, v_b).reshape(B, S, MLA_OUT_W)

    mkv = mem @ w_mem_kv
    M = mem.shape[1]
    mk = mkv[..., :MEM_Q_W].reshape(B, M, MEM_HEADS, MEM_HEAD_DIM)
    mv = mkv[..., MEM_Q_W:].reshape(B, M, MEM_HEADS, MEM_HEAD_DIM)
    oc = blocked_attention(q_m.reshape(B, S, MEM_HEADS, 1, MEM_HEAD_DIM), mk, mv).reshape(B, S, MEM_Q_W)

    gates = jax.nn.sigmoid((gate_pre + b_gate).astype(jnp.float32)).astype(h.dtype).reshape(B, S, N_BRANCH, D)
    merged = (gates[:, :, 0] * (oa @ w_br_gqa)
              + gates[:, :, 1] * (ob @ w_br_mla)
              + gates[:, :, 2] * (oc @ w_br_mem))
    return merged @ w_out


def moe_ffn(h, w_router, b_router, w_exp_in, b_exp_in, w_exp_out, b_exp_out):
    B, S, D = h.shape
    N = B * S
    A = N * TOP_K
    xt = h.reshape(N, D)
    logits = jnp.dot(xt, w_router, preferred_element_type=jnp.float32) + b_router.astype(jnp.float32)
    top_val, top_idx = lax.top_k(logits, TOP_K)
    gate = jax.nn.softmax(top_val, axis=-1)

    flat_e = top_idx.reshape(A).astype(jnp.int32)
    flat_tok = jnp.repeat(jnp.arange(N, dtype=jnp.int32), TOP_K)
    flat_w = gate.reshape(A)
    order = jnp.argsort(flat_e)
    e_s, tok_s, w_s = flat_e[order], flat_tok[order], flat_w[order]
    counts = jnp.bincount(flat_e, length=N_EXPERTS).astype(jnp.int32)
    padded = (counts + EXPERT_BLOCK - 1) // EXPERT_BLOCK * EXPERT_BLOCK
    pad_end = jnp.cumsum(padded)
    pad_start = pad_end - padded
    grp_start = jnp.cumsum(counts) - counts
    dest = pad_start[e_s] + jnp.arange(A, dtype=jnp.int32) - grp_start[e_s]
    n_blocks = -(-A // EXPERT_BLOCK) + N_EXPERTS
    n_slots = n_blocks * EXPERT_BLOCK
    slot_tok = jnp.full((n_slots,), N, jnp.int32).at[dest].set(tok_s)
    slot_w = jnp.zeros((n_slots,), jnp.float32).at[dest].set(w_s)
    blk_start = jnp.arange(n_blocks, dtype=jnp.int32) * EXPERT_BLOCK
    blk_e = jnp.minimum(jnp.searchsorted(pad_end, blk_start, side='right'), N_EXPERTS - 1).astype(jnp.int32)
    x_pad = jnp.concatenate([xt, jnp.zeros((1, D), xt.dtype)], axis=0)
    xs = x_pad[slot_tok].reshape(n_blocks, EXPERT_BLOCK, D)

    def expert_block(args):
        xb, e = args
        hb = xb @ w_exp_in[e] + b_exp_in[e]
        x_glu = jnp.minimum(hb[:, :D_EXPERT], SWIGLU_LIMIT)
        x_lin = jnp.clip(hb[:, D_EXPERT:], -SWIGLU_LIMIT, SWIGLU_LIMIT)
        act = x_glu * jax.nn.sigmoid(SWIGLU_ALPHA * x_glu) * (x_lin + 1.0)
        return act @ w_exp_out[e] + b_exp_out[e]

    ys = lax.map(expert_block, (xs, blk_e)).reshape(n_slots, D)
    ys = ys * slot_w[:, None].astype(ys.dtype)
    out = jnp.zeros((N + 1, D), ys.dtype).at[slot_tok].add(ys)[:N]
    return out.reshape(B, S, D)


def setup_inputs(seed: int = 0) -> dict:
    key = jax.random.key(seed)
    ks = iter(jax.random.split(key, 32))

    def w(shape, fan_in, scale=1.0):
        return jax.random.normal(next(ks), shape, jnp.float32) * (fan_in ** -0.5) * scale

    def gain(shape):
        return 1.0 + 0.05 * jax.random.normal(next(ks), shape, jnp.float32)

    def bias(shape, scale=0.02):
        return scale * jax.random.normal(next(ks), shape, jnp.float32)

    L, D = DEPTH, D_MODEL
    return {
        "x": jax.random.normal(next(ks), (BATCH, SEQ, D), jnp.float32),
        "mem": jax.random.normal(next(ks), (BATCH, MEM_LEN, D), jnp.float32),
        "ln_in_g": gain((D,)),
        "ln_in_b": bias((D,)),
        "w_in_proj": w((L, D, IN_PROJ_W), D),
        "b_gate": bias((L, GATE_W)),
        "gqa_q_norm": gain((L, GQA_HEAD_DIM)),
        "gqa_k_norm": gain((L, GQA_HEAD_DIM)),
        "mla_q_norm": gain((L, MLA_Q_LORA)),
        "mla_kv_norm": gain((L, MLA_KV_LORA)),
        "w_mla_qb": w((L, MLA_Q_LORA, MLA_HEADS * (MLA_NOPE_DIM + MLA_ROPE_DIM)), MLA_Q_LORA),
        "w_mla_kvb": w((L, MLA_KV_LORA, MLA_HEADS * (MLA_NOPE_DIM + MLA_V_DIM)), MLA_KV_LORA),
        "w_mem_kv": w((L, D, 2 * MEM_Q_W), D),
        "w_br_gqa": w((L, GQA_Q_W, D), GQA_Q_W),
        "w_br_mla": w((L, MLA_OUT_W, D), MLA_OUT_W),
        "w_br_mem": w((L, MEM_Q_W, D), MEM_Q_W),
        "w_out": w((L, D, D), D, DEEPNORM_BETA),
        "ln1_g": gain((L, D)),
        "ln1_b": bias((L, D)),
        "w_router": w((L, D, N_EXPERTS), D),
        "b_router": bias((L, N_EXPERTS), 0.01),
        "w_exp_in": w((L, N_EXPERTS, D, 2 * D_EXPERT), D),
        "b_exp_in": bias((L, N_EXPERTS, 2 * D_EXPERT)),
        "w_exp_out": w((L, N_EXPERTS, D_EXPERT, D), D_EXPERT, DEEPNORM_BETA),
        "b_exp_out": bias((L, N_EXPERTS, D)),
        "ln2_g": gain((L, D)),
        "ln2_b": bias((L, D)),
    }


def reference(x, mem, ln_in_g, ln_in_b, w_in_proj, b_gate, gqa_q_norm, gqa_k_norm, mla_q_norm,
              mla_kv_norm, w_mla_qb, w_mla_kvb, w_mem_kv, w_br_gqa, w_br_mla, w_br_mem, w_out,
              ln1_g, ln1_b, w_router, b_router, w_exp_in, b_exp_in, w_exp_out, b_exp_out,
              ln2_g, ln2_b):
    S = x.shape[1]
    rows = S // GRID_W
    row = jnp.repeat(jnp.arange(rows, dtype=jnp.int32), GRID_W)
    col = jnp.tile(jnp.arange(GRID_W, dtype=jnp.int32), rows)
    h = layer_norm(x, ln_in_g, ln_in_b)
    for l in range(DEPTH):
        mix = hybrid_mixer(h, mem, row, col, w_in_proj[l], b_gate[l], gqa_q_norm[l], gqa_k_norm[l],
                           mla_q_norm[l], mla_kv_norm[l], w_mla_qb[l], w_mla_kvb[l], w_mem_kv[l],
                           w_br_gqa[l], w_br_mla[l], w_br_mem[l], w_out[l])
        h = layer_norm(DEEPNORM_ALPHA * h + mix, ln1_g[l], ln1_b[l])
        ffn = moe_ffn(h, w_router[l], b_router[l], w_exp_in[l], b_exp_in[l], w_exp_out[l], b_exp_out[l])
        h = layer_norm(DEEPNORM_ALPHA * h + ffn, ln2_g[l], ln2_b[l])
    return h
```

```python
import functools

import jax
import jax.numpy as jnp
import numpy as np
from jax import lax
from jax.experimental import pallas as pl
from jax.experimental.pallas import tpu as pltpu

D_MODEL = 1024
MEM_LEN = 256
GRID_W = 64
ROPE_THETA = 10000.0
RMS_EPS = 1e-6
LN_EPS = 1e-5

GQA_HEADS = 8
GQA_KV_HEADS = 2
GQA_HEAD_DIM = 64
MLA_HEADS = 8
MLA_NOPE_DIM = 64
MLA_ROPE_DIM = 32
MLA_V_DIM = 64
MLA_Q_LORA = 384
MLA_KV_LORA = 256
MEM_HEADS = 4
MEM_HEAD_DIM = 128

N_EXPERTS = 32
TOP_K = 4
SWIGLU_LIMIT = 7.0
SWIGLU_ALPHA = 1.702
DEEPNORM_ALPHA = 2.0 ** 0.25

LANES = 128
HALF = LANES // 2
NEG_INF = float("-inf")

OFF_QA, OFF_KA, OFF_VA, OFF_QLAT, OFF_KVLAT, OFF_KROPE, OFF_QM, OFF_GATE = (
    0, 512, 640, 768, 1152, 1408, 1440, 1952)
IN_PROJ_W = 5024
A_QA, A_KA, A_VA, A_QLAT, A_KVLAT, A_KROPE, A_QM, A_END = 0, 512, 640, 768, 1152, 1408, 1536, 2048

ROW_TILE = 256
ATTN_Q_TILE = 256
ROUTE_TILE = 512
EXPERT_BLOCK = 256
VMEM_LIMIT = 48 * 1024 * 1024


def _cparams(*sem):
    return pltpu.CompilerParams(dimension_semantics=sem, vmem_limit_bytes=VMEM_LIMIT)


def _lane_iota(shape):
    return lax.broadcasted_iota(jnp.int32, shape, len(shape) - 1)


def _layer_norm(x, g, b):
    mu = jnp.mean(x, axis=-1, keepdims=True)
    xc = x - mu
    var = jnp.mean(xc * xc, axis=-1, keepdims=True)
    return xc * lax.rsqrt(var + LN_EPS) * g + b


def _bdot(a, b):
    return jnp.dot(a.astype(jnp.bfloat16), b.astype(jnp.bfloat16), preferred_element_type=jnp.float32)


def _rope(x, cos, sin_signed, pair):
    lane = _lane_iota(x.shape)
    fwd = pltpu.roll(x, LANES - pair, 1)
    bwd = pltpu.roll(x, pair, 1)
    partner = jnp.where((lane % (2 * pair)) < pair, fwd, bwd)
    return x * cos + partner * sin_signed


def _half_rms_scale(x):
    lane = _lane_iota(x.shape)
    sq = x * x
    lo = jnp.sum(jnp.where(lane < HALF, sq, 0.0), axis=-1, keepdims=True)
    hi = jnp.sum(jnp.where(lane >= HALF, sq, 0.0), axis=-1, keepdims=True)
    inv = 1.0 / GQA_HEAD_DIM
    return jnp.where(lane < HALF, lax.rsqrt(lo * inv + RMS_EPS), lax.rsqrt(hi * inv + RMS_EPS))


def _softmax_rows(s):
    m = jnp.max(s, axis=-1, keepdims=True)
    p = jnp.exp(s - m)
    return p, jnp.sum(p, axis=-1, keepdims=True)


def _mem_kv_kernel(mem_ref, w_ref, kt_ref, v_ref):
    kv = _bdot(mem_ref[0], w_ref[...])
    width = MEM_HEADS * MEM_HEAD_DIM
    kt_ref[0] = kv[:, :width].T.astype(jnp.bfloat16)
    v_ref[0] = kv[:, width:].astype(jnp.bfloat16)


def _mem_kv(mem, w_mem_kv):
    B, M, D = mem.shape
    width = MEM_HEADS * MEM_HEAD_DIM
    return pl.pallas_call(
        _mem_kv_kernel,
        grid=(B,),
        in_specs=[pl.BlockSpec((1, M, D), lambda b: (b, 0, 0)),
                  pl.BlockSpec((D, 2 * width), lambda b: (0, 0))],
        out_specs=[pl.BlockSpec((1, width, M), lambda b: (b, 0, 0)),
                   pl.BlockSpec((1, M, width), lambda b: (b, 0, 0))],
        out_shape=[jax.ShapeDtypeStruct((B, width, M), jnp.bfloat16),
                   jax.ShapeDtypeStruct((B, M, width), jnp.bfloat16)],
        compiler_params=_cparams("arbitrary"),
        name="mem_kv",
    )(mem, w_mem_kv)


def _in_proj_kernel(x_ref, lng_ref, lnb_ref, wa_ref, gq_ref, gk_ref, gql_ref, gkvl_ref,
                    wqb_ref, wkvk_ref, wkvv_ref, ca_ref, sa_ref, cb_ref, sb_ref, mkt_ref, mv_ref,
                    h_ref, qa_ref, kta_ref, va_ref, qb_ref, ktb_ref, vb_ref, oc_ref):
    h = _layer_norm(x_ref[...], lng_ref[...], lnb_ref[...])
    h_ref[...] = h
    proj = _bdot(h, wa_ref[...])
    lane = _lane_iota((h.shape[0], LANES))
    lo_mask = lane < HALF
    ca, sa, cb, sb = ca_ref[...], sa_ref[...], cb_ref[...], sb_ref[...]

    q_scale = GQA_HEAD_DIM ** -0.5
    per_group = GQA_HEADS // GQA_KV_HEADS
    for c in range(GQA_HEADS // 2):
        slab = proj[:, A_QA + c * LANES:A_QA + (c + 1) * LANES]
        slab = _rope(slab * _half_rms_scale(slab) * gq_ref[...], ca, sa, 16) * q_scale
        swapped = pltpu.roll(slab, HALF, 1)
        for half in range(2):
            head = 2 * c + half
            group_lo = (head // per_group) == 0
            src = slab if (half == 0) == group_lo else swapped
            keep = lo_mask if group_lo else jnp.logical_not(lo_mask)
            qa_ref[:, head * LANES:(head + 1) * LANES] = jnp.where(keep, src, 0.0).astype(jnp.bfloat16)

    ka = proj[:, A_KA:A_KA + LANES]
    ka = _rope(ka * _half_rms_scale(ka) * gk_ref[...], ca, sa, 16)
    kta_ref[0] = ka.T.astype(jnp.bfloat16)
    va = proj[:, A_VA:A_VA + LANES]
    va_sw = pltpu.roll(va, HALF, 1)
    va_ref[:, :LANES] = jnp.where(lo_mask, va, va_sw).astype(jnp.bfloat16)
    va_ref[:, LANES:] = jnp.where(lo_mask, va_sw, va).astype(jnp.bfloat16)

    ql = proj[:, A_QLAT:A_KVLAT]
    ql = ql * lax.rsqrt(jnp.mean(ql * ql, axis=-1, keepdims=True) + RMS_EPS) * gql_ref[...]
    qm = _bdot(ql, wqb_ref[...])
    kvl = proj[:, A_KVLAT:A_KROPE]
    kvl = kvl * lax.rsqrt(jnp.mean(kvl * kvl, axis=-1, keepdims=True) + RMS_EPS) * gkvl_ref[...]
    kn = _bdot(kvl, wkvk_ref[...])
    vb_ref[...] = _bdot(kvl, wkvv_ref[...]).astype(jnp.bfloat16)
    k_pe = _rope(proj[:, A_KROPE:A_QM], cb, sb, 8)
    qb_scale = (MLA_NOPE_DIM + MLA_ROPE_DIM) ** -0.5
    for hd in range(MLA_HEADS):
        sl = slice(hd * LANES, (hd + 1) * LANES)
        qb_ref[:, sl] = (_rope(qm[:, sl], cb, sb, 8) * qb_scale).astype(jnp.bfloat16)
        ktb_ref[0, sl, :] = (kn[:, sl] + k_pe).T.astype(jnp.bfloat16)

    qc = proj[:, A_QM:A_END] * (MEM_HEAD_DIM ** -0.5)
    for hd in range(MEM_HEADS):
        sl = slice(hd * MEM_HEAD_DIM, (hd + 1) * MEM_HEAD_DIM)
        p, l = _softmax_rows(_bdot(qc[:, sl], mkt_ref[0, sl, :]))
        oc_ref[:, sl] = (_bdot(p, mv_ref[0, :, sl]) / l).astype(jnp.bfloat16)


def _in_proj(x2, ln_g, ln_b, wa, gq, gk, gql, gkvl, wqb, wkvk, wkvv, ca, sa, cb, sb, mkt, mv, B, S):
    N, D = x2.shape
    T = ROW_TILE
    tiles_per_seq = S // T
    row = lambda i: (i, 0)
    const = lambda i: (0, 0)
    pos = lambda i: (i % tiles_per_seq, 0)
    batch3 = lambda i: (i // tiles_per_seq, 0, 0)
    kt_map = lambda i: (i // tiles_per_seq, 0, i % tiles_per_seq)
    full = lambda a: pl.BlockSpec(a.shape, const)
    return pl.pallas_call(
        _in_proj_kernel,
        grid=(N // T,),
        in_specs=[pl.BlockSpec((T, D), row), full(ln_g), full(ln_b), full(wa), full(gq), full(gk),
                  full(gql), full(gkvl), full(wqb), full(wkvk), full(wkvv),
                  pl.BlockSpec((T, LANES), pos), pl.BlockSpec((T, LANES), pos),
                  pl.BlockSpec((T, LANES), pos), pl.BlockSpec((T, LANES), pos),
                  pl.BlockSpec((1,) + mkt.shape[1:], batch3), pl.BlockSpec((1,) + mv.shape[1:], batch3)],
        out_specs=[pl.BlockSpec((T, D), row),
                   pl.BlockSpec((T, GQA_HEADS * LANES), row),
                   pl.BlockSpec((1, LANES, T), kt_map),
                   pl.BlockSpec((T, 2 * LANES), row),
                   pl.BlockSpec((T, MLA_HEADS * LANES), row),
                   pl.BlockSpec((1, MLA_HEADS * LANES, T), kt_map),
                   pl.BlockSpec((T, MLA_HEADS * MLA_V_DIM), row),
                   pl.BlockSpec((T, MEM_HEADS * MEM_HEAD_DIM), row)],
        out_shape=[jax.ShapeDtypeStruct((N, D), jnp.float32),
                   jax.ShapeDtypeStruct((N, GQA_HEADS * LANES), jnp.bfloat16),
                   jax.ShapeDtypeStruct((B, LANES, S), jnp.bfloat16),
                   jax.ShapeDtypeStruct((N, 2 * LANES), jnp.bfloat16),
                   jax.ShapeDtypeStruct((N, MLA_HEADS * LANES), jnp.bfloat16),
                   jax.ShapeDtypeStruct((B, MLA_HEADS * LANES, S), jnp.bfloat16),
                   jax.ShapeDtypeStruct((N, MLA_HEADS * MLA_V_DIM), jnp.bfloat16),
                   jax.ShapeDtypeStruct((N, MEM_HEADS * MEM_HEAD_DIM), jnp.bfloat16)],
        compiler_params=_cparams("arbitrary"),
        name="in_proj",
    )(x2, ln_g, ln_b, wa, gq, gk, gql, gkvl, wqb, wkvk, wkvv, ca, sa, cb, sb, mkt, mv)


def _attention_kernel(q_ref, kt_ref, v_ref, o_ref, *, shared_k):
    v = v_ref[0]
    lane = _lane_iota(v.shape)
    zero = jnp.zeros_like(v)
    v_halves = (jnp.where(lane < HALF, v, zero), jnp.where(lane >= HALF, v, zero))
    out = None
    for hd in range(2):
        q = q_ref[0, :, hd * LANES:(hd + 1) * LANES]
        kt = kt_ref[0] if shared_k else kt_ref[0, hd * LANES:(hd + 1) * LANES, :]
        s = jnp.dot(q, kt, preferred_element_type=jnp.float32)
        p, l = _softmax_rows(s)
        o = jnp.dot(p.astype(jnp.bfloat16), v_halves[hd], preferred_element_type=jnp.float32) / l
        out = o if out is None else out + o
    o_ref[0] = out.astype(jnp.bfloat16)


def _attention(q, kt, v, *, shared_k, v_block_of_pair):
    B, S, qw = q.shape
    pairs = qw // (2 * LANES)
    TQ = ATTN_Q_TILE
    k_rows = LANES if shared_k else 2 * LANES
    kt_map = (lambda b, j, i: (b, 0, 0)) if shared_k else (lambda b, j, i: (b, j, 0))
    return pl.pallas_call(
        functools.partial(_attention_kernel, shared_k=shared_k),
        grid=(B, pairs, S // TQ),
        in_specs=[pl.BlockSpec((1, TQ, 2 * LANES), lambda b, j, i: (b, i, j)),
                  pl.BlockSpec((1, k_rows, S), kt_map),
                  pl.BlockSpec((1, S, LANES), lambda b, j, i: (b, 0, v_block_of_pair(j)))],
        out_specs=pl.BlockSpec((1, TQ, LANES), lambda b, j, i: (b, i, j)),
        out_shape=jax.ShapeDtypeStruct((B, S, pairs * LANES), jnp.bfloat16),
        compiler_params=_cparams("arbitrary", "arbitrary", "arbitrary"),
        name="attention_gqa" if shared_k else "attention_mla",
    )(q, kt, v)


def _split3(x):
    a = x.astype(jnp.bfloat16)
    r = x - a.astype(jnp.float32)
    b = r.astype(jnp.bfloat16)
    c = (r - b.astype(jnp.float32)).astype(jnp.bfloat16)
    return a, b, c


def _mix_out_kernel(h_ref, oa_ref, ob_ref, oc_ref, wg_ref, bg_ref, wa_ref, wb_ref, wc_ref, wo_ref,
                    g_ref, b_ref, wr_ref, br_ref, h1_ref, idx_ref, gate_ref):
    h = h_ref[...]
    gates = jax.nn.sigmoid(_bdot(h, wg_ref[...]) + bg_ref[...])
    D = D_MODEL
    merged = (gates[:, :D] * _bdot(oa_ref[...], wa_ref[...])
              + gates[:, D:2 * D] * _bdot(ob_ref[...], wb_ref[...])
              + gates[:, 2 * D:] * _bdot(oc_ref[...], wc_ref[...]))
    mix = _bdot(merged, wo_ref[...])
    h1 = _layer_norm(DEEPNORM_ALPHA * h + mix, g_ref[...], b_ref[...])
    h1_ref[...] = h1

    ha, hb, hc = _split3(h1)
    wa_, wb_, wc_ = wr_ref[0], wr_ref[1], wr_ref[2]
    dot = lambda a, b: jnp.dot(a, b, preferred_element_type=jnp.float32)
    logits = (dot(ha, wa_) + (dot(ha, wb_) + dot(hb, wa_))
              + (dot(ha, wc_) + dot(hb, wb_) + dot(hc, wa_))) + br_ref[...]
    lane = _lane_iota(logits.shape)
    logits = jnp.where(lane < N_EXPERTS, logits, NEG_INF)

    idx_out = jnp.zeros(logits.shape, jnp.int32)
    val_out = jnp.zeros(logits.shape, jnp.float32)
    top = None
    for k in range(TOP_K):
        m = jnp.max(logits, axis=-1, keepdims=True)
        idx = jnp.min(jnp.where(logits == m, lane, LANES), axis=-1, keepdims=True)
        logits = jnp.where(lane == idx, NEG_INF, logits)
        top = m if top is None else top
        idx_out = jnp.where(lane == k, idx, idx_out)
        val_out = jnp.where(lane == k, jnp.exp(m - top), val_out)
    idx_ref[...] = idx_out
    gate_ref[...] = val_out / jnp.sum(val_out, axis=-1, keepdims=True)


def _mix_out(h, oa, ob, oc, wg, bg, wa, wb, wc, wo, g, b, wr3, br):
    N, D = h.shape
    T = ROW_TILE
    row = lambda i: (i, 0)
    full = lambda a: pl.BlockSpec(a.shape, lambda i: (0,) * a.ndim)
    return pl.pallas_call(
        _mix_out_kernel,
        grid=(N // T,),
        in_specs=[pl.BlockSpec((T, D), row), pl.BlockSpec((T, oa.shape[1]), row),
                  pl.BlockSpec((T, ob.shape[1]), row), pl.BlockSpec((T, oc.shape[1]), row),
                  full(wg), full(bg), full(wa), full(wb), full(wc), full(wo), full(g), full(b),
                  full(wr3), full(br)],
        out_specs=[pl.BlockSpec((T, D), row), pl.BlockSpec((T, LANES), row), pl.BlockSpec((T, LANES), row)],
        out_shape=[jax.ShapeDtypeStruct((N, D), jnp.float32),
                   jax.ShapeDtypeStruct((N, LANES), jnp.int32),
                   jax.ShapeDtypeStruct((N, LANES), jnp.float32)],
        compiler_params=_cparams("arbitrary"),
        name="mix_out",
    )(h, oa, ob, oc, wg, bg, wa, wb, wc, wo, g, b, wr3, br)


def _route_kernel(idx_ref, rank_ref, count_ref, carry_ref):
    @pl.when(pl.program_id(0) == 0)
    def _():
        carry_ref[...] = jnp.zeros_like(carry_ref)

    idx = idx_ref[...]
    T = idx.shape[0]
    lane = _lane_iota(idx.shape)
    sel = [lane == idx[:, k:k + 1] for k in range(TOP_K)]
    onehot = sum(s.astype(jnp.float32) for s in sel)
    r = lax.broadcasted_iota(jnp.int32, (T, T), 0)
    c = lax.broadcasted_iota(jnp.int32, (T, T), 1)
    before = (c < r).astype(jnp.bfloat16)
    ranks = jnp.dot(before, onehot.astype(jnp.bfloat16), preferred_element_type=jnp.float32)
    ranks = ranks + carry_ref[0:1, :]
    out = jnp.zeros(idx.shape, jnp.int32)
    for k in range(TOP_K):
        rk = jnp.sum(jnp.where(sel[k], ranks, 0.0), axis=-1, keepdims=True)
        out = jnp.where(lane == k, rk.astype(jnp.int32), out)
    rank_ref[...] = out
    carry_ref[...] = carry_ref[...] + jnp.sum(onehot, axis=0, keepdims=True)
    count_ref[...] = carry_ref[...].astype(jnp.int32)


def _route(idx):
    N = idx.shape[0]
    T = ROUTE_TILE
    return pl.pallas_call(
        _route_kernel,
        grid=(N // T,),
        in_specs=[pl.BlockSpec((T, LANES), lambda i: (i, 0))],
        out_specs=[pl.BlockSpec((T, LANES), lambda i: (i, 0)), pl.BlockSpec((8, LANES), lambda i: (0, 0))],
        out_shape=[jax.ShapeDtypeStruct((N, LANES), jnp.int32), jax.ShapeDtypeStruct((8, LANES), jnp.int32)],
        scratch_shapes=[pltpu.VMEM((8, LANES), jnp.float32)],
        compiler_params=_cparams("arbitrary"),
        name="route",
    )(idx)


def _row_copy(src, src_row, dst, dst_row, sem):
    return pltpu.make_async_copy(src.at[pl.ds(src_row, 1), :], dst.at[pl.ds(dst_row, 1), :], sem)


def _dispatch_kernel(start_ref, idx_ref, rank_ref, h_ref, xs_in_ref, xs_ref, sem):
    del xs_in_ref
    n = h_ref.shape[0] * TOP_K

    def issue(a, carry):
        slot = start_ref[idx_ref[a]] + rank_ref[a]
        _row_copy(h_ref, a // TOP_K, xs_ref, slot, sem).start()
        return carry

    lax.fori_loop(0, n, issue, 0)

    def drain(a, carry):
        _row_copy(h_ref, 0, xs_ref, 0, sem).wait()
        return carry

    lax.fori_loop(0, n, drain, 0)


def _dispatch(pad_start, idx_flat, rank_flat, h1, n_slots):
    N, D = h1.shape
    T = ROW_TILE
    xs0 = jnp.zeros((n_slots, D), jnp.float32)
    smem = lambda: pl.BlockSpec((T * TOP_K,), lambda i, s: (i,), memory_space=pltpu.SMEM)
    return pl.pallas_call(
        _dispatch_kernel,
        grid_spec=pltpu.PrefetchScalarGridSpec(
            num_scalar_prefetch=1,
            grid=(N // T,),
            in_specs=[smem(), smem(), pl.BlockSpec((T, D), lambda i, s: (i, 0)),
                      pl.BlockSpec(memory_space=pl.ANY)],
            out_specs=pl.BlockSpec(memory_space=pl.ANY),
            scratch_shapes=[pltpu.SemaphoreType.DMA(())]),
        out_shape=jax.ShapeDtypeStruct((n_slots, D), jnp.float32),
        input_output_aliases={4: 0},
        compiler_params=_cparams("arbitrary"),
        name="dispatch",
    )(pad_start, idx_flat, rank_flat, h1, xs0)


def _experts_kernel(blk_e_ref, n_used_ref, x_ref, wi_ref, bi_ref, wo_ref, bo_ref, y_ref):
    del blk_e_ref
    i = pl.program_id(0)

    @pl.when(i < n_used_ref[0])
    def _():
        hb = _bdot(x_ref[...], wi_ref[0]) + bi_ref[0]
        De = D_MODEL
        x_glu = jnp.minimum(hb[:, :De], SWIGLU_LIMIT)
        x_lin = jnp.clip(hb[:, De:], -SWIGLU_LIMIT, SWIGLU_LIMIT)
        act = x_glu * jax.nn.sigmoid(SWIGLU_ALPHA * x_glu) * (x_lin + 1.0)
        y_ref[...] = _bdot(act, wo_ref[0]) + bo_ref[0]

    @pl.when(i >= n_used_ref[0])
    def _():
        y_ref[...] = jnp.zeros_like(y_ref)


def _experts(blk_e, n_used, xs, w_in, b_in, w_out, b_out):
    n_slots, D = xs.shape
    R = EXPERT_BLOCK
    E, _, F = w_in.shape
    return pl.pallas_call(
        _experts_kernel,
        grid_spec=pltpu.PrefetchScalarGridSpec(
            num_scalar_prefetch=2,
            grid=(n_slots // R,),
            in_specs=[pl.BlockSpec((R, D), lambda i, e, n: (i, 0)),
                      pl.BlockSpec((1, D, F), lambda i, e, n: (e[i], 0, 0)),
                      pl.BlockSpec((1, 1, F), lambda i, e, n: (e[i], 0, 0)),
                      pl.BlockSpec((1, F // 2, D), lambda i, e, n: (e[i], 0, 0)),
                      pl.BlockSpec((1, 1, D), lambda i, e, n: (e[i], 0, 0))],
            out_specs=pl.BlockSpec((R, D), lambda i, e, n: (i, 0))),
        out_shape=jax.ShapeDtypeStruct((n_slots, D), jnp.float32),
        compiler_params=_cparams("arbitrary"),
        name="experts",
    )(blk_e, n_used, xs, w_in, b_in, w_out, b_out)


def _combine_kernel(start_ref, idx_ref, rank_ref, gate_ref, h_ref, g_ref, b_ref, ys_ref, o_ref, buf, sem):
    T = h_ref.shape[0]
    n = T * TOP_K

    def issue(a, carry):
        slot = start_ref[idx_ref[a]] + rank_ref[a]
        t = a // TOP_K
        k = a - t * TOP_K
        pltpu.make_async_copy(ys_ref.at[pl.ds(slot, 1), :], buf.at[k, pl.ds(t, 1), :], sem).start()
        return carry

    lax.fori_loop(0, n, issue, 0)

    def drain(a, carry):
        pltpu.make_async_copy(ys_ref.at[pl.ds(0, 1), :], buf.at[0, pl.ds(0, 1), :], sem).wait()
        return carry

    lax.fori_loop(0, n, drain, 0)

    gate = gate_ref[...]
    ffn = gate[:, 0:1] * buf[0]
    for k in range(1, TOP_K):
        ffn = ffn + gate[:, k:k + 1] * buf[k]
    o_ref[...] = _layer_norm(DEEPNORM_ALPHA * h_ref[...] + ffn, g_ref[...], b_ref[...])


def _combine(pad_start, idx_flat, rank_flat, gate, h1, g, b, ys):
    N, D = h1.shape
    T = ROW_TILE
    smem = lambda: pl.BlockSpec((T * TOP_K,), lambda i, s: (i,), memory_space=pltpu.SMEM)
    row = lambda i, s: (i, 0)
    const = lambda i, s: (0, 0)
    return pl.pallas_call(
        _combine_kernel,
        grid_spec=pltpu.PrefetchScalarGridSpec(
            num_scalar_prefetch=1,
            grid=(N // T,),
            in_specs=[smem(), smem(), pl.BlockSpec((T, LANES), row), pl.BlockSpec((T, D), row),
                      pl.BlockSpec(g.shape, const), pl.BlockSpec(b.shape, const),
                      pl.BlockSpec(memory_space=pl.ANY)],
            out_specs=pl.BlockSpec((T, D), row),
            scratch_shapes=[pltpu.VMEM((TOP_K, T, D), jnp.float32), pltpu.SemaphoreType.DMA(())]),
        out_shape=jax.ShapeDtypeStruct((N, D), jnp.float32),
        compiler_params=_cparams("arbitrary"),
        name="combine",
    )(pad_start, idx_flat, rank_flat, gate, h1, g, b, ys)


def _rope_tables(S):
    t = np.arange(S)
    row, col = (t // GRID_W).astype(np.float64), (t % GRID_W).astype(np.float64)

    def block(half):
        inv = ROPE_THETA ** (-np.arange(half, dtype=np.float64) / half)
        ar, ac = row[:, None] * inv[None, :], col[:, None] * inv[None, :]
        cos = np.concatenate([np.cos(ar), np.cos(ar), np.cos(ac), np.cos(ac)], axis=1)
        sin = np.concatenate([-np.sin(ar), np.sin(ar), -np.sin(ac), np.sin(ac)], axis=1)
        return cos, sin

    ca64, sa64 = block(GQA_HEAD_DIM // 4)
    ca, sa = np.tile(ca64, (1, 2)), np.tile(sa64, (1, 2))
    cb32, sb32 = block(MLA_ROPE_DIM // 4)
    ones, zeros = np.ones((S, MLA_NOPE_DIM)), np.zeros((S, MLA_NOPE_DIM))
    cb = np.concatenate([ones, cb32, ones[:, :LANES - MLA_NOPE_DIM - MLA_ROPE_DIM]], axis=1)
    sb = np.concatenate([zeros, sb32, zeros[:, :LANES - MLA_NOPE_DIM - MLA_ROPE_DIM]], axis=1)
    return tuple(jnp.asarray(a, jnp.float32) for a in (ca, sa, cb, sb))


def _layer(h_in_is_x, x2, mem, B, S, ln_g, ln_b, w_in_proj, b_gate, gqa_q_norm, gqa_k_norm, mla_q_norm,
           mla_kv_norm, w_mla_qb, w_mla_kvb, w_mem_kv, w_br_gqa, w_br_mla, w_br_mem, w_out,
           ln1_g, ln1_b, w_router, b_router, w_exp_in, b_exp_in, w_exp_out, b_exp_out, ln2_g, ln2_b):
    del h_in_is_x
    bf = jnp.bfloat16
    N, D = x2.shape
    row2 = lambda a: a.reshape(1, -1)

    W = w_in_proj
    zc = lambda n: jnp.zeros((D, n), W.dtype)
    wa = jnp.concatenate([W[:, :OFF_KROPE], zc(HALF), W[:, OFF_KROPE:OFF_QM],
                          zc(LANES - HALF - MLA_ROPE_DIM), W[:, OFF_QM:OFF_GATE]], axis=1).astype(bf)
    wg = W[:, OFF_GATE:].astype(bf)
    qd = MLA_NOPE_DIM + MLA_ROPE_DIM
    wqb = jnp.pad(w_mla_qb.reshape(MLA_Q_LORA, MLA_HEADS, qd),
                  ((0, 0), (0, 0), (0, LANES - qd))).reshape(MLA_Q_LORA, MLA_HEADS * LANES).astype(bf)
    kvb = w_mla_kvb.reshape(MLA_KV_LORA, MLA_HEADS, MLA_NOPE_DIM + MLA_V_DIM)
    wkvk = jnp.pad(kvb[:, :, :MLA_NOPE_DIM], ((0, 0), (0, 0), (0, LANES - MLA_NOPE_DIM))
                   ).reshape(MLA_KV_LORA, MLA_HEADS * LANES).astype(bf)
    wkvv = kvb[:, :, MLA_NOPE_DIM:].reshape(MLA_KV_LORA, MLA_HEADS * MLA_V_DIM).astype(bf)
    gq = row2(jnp.tile(gqa_q_norm, 2))
    gk = row2(jnp.tile(gqa_k_norm, 2))
    ca, sa, cb, sb = _rope_tables(S)

    mkt, mv = _mem_kv(mem, w_mem_kv.astype(bf))
    h, qa, kta, va, qb, ktb, vb, oc = _in_proj(
        x2, row2(ln_g), row2(ln_b), wa, gq, gk, row2(mla_q_norm), row2(mla_kv_norm),
        wqb, wkvk, wkvv, ca, sa, cb, sb, mkt, mv, B, S)

    pairs_per_group = GQA_HEADS // GQA_KV_HEADS // 2
    oa = _attention(qa.reshape(B, S, -1), kta, va.reshape(B, S, -1), shared_k=True,
                    v_block_of_pair=lambda j: j // pairs_per_group)
    ob = _attention(qb.reshape(B, S, -1), ktb, vb.reshape(B, S, -1), shared_k=False,
                    v_block_of_pair=lambda j: j)

    wr = jnp.pad(w_router, ((0, 0), (0, LANES - N_EXPERTS)))
    wr3 = jnp.stack(_split3(wr))
    br = row2(jnp.pad(b_router, (0, LANES - N_EXPERTS)))
    h1, idx, gate = _mix_out(h, oa.reshape(N, -1), ob.reshape(N, -1), oc, wg, row2(b_gate),
                             w_br_gqa.astype(bf), w_br_mla.astype(bf), w_br_mem.astype(bf),
                             w_out.astype(bf), row2(ln1_g), row2(ln1_b), wr3, br)

    rank, counts = _route(idx)
    counts = counts[0, :N_EXPERTS]
    R = EXPERT_BLOCK
    padded = (counts + R - 1) // R * R
    pad_end = jnp.cumsum(padded)
    pad_start = (pad_end - padded).astype(jnp.int32)
    n_blocks = N * TOP_K // R + N_EXPERTS
    blk_start = jnp.arange(n_blocks, dtype=jnp.int32) * R
    blk_e = jnp.minimum(jnp.searchsorted(pad_end, blk_start, side='right'), N_EXPERTS - 1).astype(jnp.int32)
    n_used = (pad_end[-1:] // R).astype(jnp.int32)
    blk_e = jnp.where(jnp.arange(n_blocks) < n_used[0], blk_e, blk_e[jnp.maximum(n_used[0] - 1, 0)])
    idx_flat = idx[:, :TOP_K].reshape(-1)
    rank_flat = rank[:, :TOP_K].reshape(-1)

    xs = _dispatch(pad_start, idx_flat, rank_flat, h1, n_blocks * R)
    ys = _experts(blk_e, n_used, xs, w_exp_in.astype(bf), b_exp_in[:, None, :],
                  w_exp_out.astype(bf), b_exp_out[:, None, :])
    return _combine(pad_start, idx_flat, rank_flat, gate, h1, row2(ln2_g), row2(ln2_b), ys)


def kernel(x, mem, ln_in_g, ln_in_b, w_in_proj, b_gate, gqa_q_norm, gqa_k_norm, mla_q_norm, mla_kv_norm, w_mla_qb, w_mla_kvb, w_mem_kv, w_br_gqa, w_br_mla, w_br_mem, w_out, ln1_g, ln1_b, w_router, b_router, w_exp_in, b_exp_in, w_exp_out, b_exp_out, ln2_g, ln2_b):
    B, S, D = x.shape
    depth = w_in_proj.shape[0]
    assert depth == 1, "the input LayerNorm is fused into the first (only) layer's projection kernel"
    out = _layer(True, x.reshape(B * S, D), mem, B, S, ln_in_g, ln_in_b, w_in_proj[0], b_gate[0],
                 gqa_q_norm[0], gqa_k_norm[0], mla_q_norm[0], mla_kv_norm[0], w_mla_qb[0], w_mla_kvb[0],
                 w_mem_kv[0], w_br_gqa[0], w_br_mla[0], w_br_mem[0], w_out[0], ln1_g[0], ln1_b[0],
                 w_router[0], b_router[0], w_exp_in[0], b_exp_in[0], w_exp_out[0], b_exp_out[0],
                 ln2_g[0], ln2_b[0])
    return out.reshape(B, S, D)
```

```python
import functools

import jax
import jax.numpy as jnp
import numpy as np
from jax import lax
from jax.experimental import pallas as pl
from jax.experimental.pallas import tpu as pltpu

D_MODEL = 1024
MEM_LEN = 256
GRID_W = 64
ROPE_THETA = 10000.0
RMS_EPS = 1e-6
LN_EPS = 1e-5

GQA_HEADS = 8
GQA_KV_HEADS = 2
GQA_HEAD_DIM = 64
MLA_HEADS = 8
MLA_NOPE_DIM = 64
MLA_ROPE_DIM = 32
MLA_V_DIM = 64
MLA_Q_LORA = 384
MLA_KV_LORA = 256
MEM_HEADS = 4
MEM_HEAD_DIM = 128

N_EXPERTS = 32
TOP_K = 4
SWIGLU_LIMIT = 7.0
SWIGLU_ALPHA = 1.702
DEEPNORM_ALPHA = 2.0 ** 0.25

LANES = 128
SUBLANES = 8
HALF = LANES // 2
NEG_INF = float("-inf")

OFF_QA, OFF_KA, OFF_VA, OFF_QLAT, OFF_KVLAT, OFF_KROPE, OFF_QM, OFF_GATE = (
    0, 512, 640, 768, 1152, 1408, 1440, 1952)
IN_PROJ_W = 5024
A_QA, A_KA, A_VA, A_QLAT, A_KVLAT, A_KROPE, A_QM, A_END = 0, 512, 640, 768, 1152, 1408, 1536, 2048

ROW_TILE = 256
ATTN_Q_TILE = 256
EXPERT_BLOCK = 256
VMEM_LIMIT = 48 * 1024 * 1024


def _cparams(*sem):
    return pltpu.CompilerParams(dimension_semantics=sem, vmem_limit_bytes=VMEM_LIMIT)


def _lane_iota(shape):
    return lax.broadcasted_iota(jnp.int32, shape, len(shape) - 1)


def _layer_norm(x, g, b):
    mu = jnp.mean(x, axis=-1, keepdims=True)
    xc = x - mu
    var = jnp.mean(xc * xc, axis=-1, keepdims=True)
    return xc * lax.rsqrt(var + LN_EPS) * g + b


def _bdot(a, b):
    return jnp.dot(a.astype(jnp.bfloat16), b.astype(jnp.bfloat16), preferred_element_type=jnp.float32)


def _rope(x, cos, sin_signed, pair):
    lane = _lane_iota(x.shape)
    fwd = pltpu.roll(x, LANES - pair, 1)
    bwd = pltpu.roll(x, pair, 1)
    partner = jnp.where((lane % (2 * pair)) < pair, fwd, bwd)
    return x * cos + partner * sin_signed


def _half_rms_scale(x):
    lane = _lane_iota(x.shape)
    sq = x * x
    lo = jnp.sum(jnp.where(lane < HALF, sq, 0.0), axis=-1, keepdims=True)
    hi = jnp.sum(jnp.where(lane >= HALF, sq, 0.0), axis=-1, keepdims=True)
    inv = 1.0 / GQA_HEAD_DIM
    return jnp.where(lane < HALF, lax.rsqrt(lo * inv + RMS_EPS), lax.rsqrt(hi * inv + RMS_EPS))


def _softmax_rows(s):
    m = jnp.max(s, axis=-1, keepdims=True)
    p = jnp.exp(s - m)
    return p, jnp.sum(p, axis=-1, keepdims=True)


def _mem_kv_kernel(mem_ref, w_ref, kt_ref, v_ref):
    kv = _bdot(mem_ref[0], w_ref[...])
    width = MEM_HEADS * MEM_HEAD_DIM
    kt_ref[0] = kv[:, :width].T.astype(jnp.bfloat16)
    v_ref[0] = kv[:, width:].astype(jnp.bfloat16)


def _mem_kv(mem, w_mem_kv):
    B, M, D = mem.shape
    width = MEM_HEADS * MEM_HEAD_DIM
    return pl.pallas_call(
        _mem_kv_kernel,
        grid=(B,),
        in_specs=[pl.BlockSpec((1, M, D), lambda b: (b, 0, 0)),
                  pl.BlockSpec((D, 2 * width), lambda b: (0, 0))],
        out_specs=[pl.BlockSpec((1, width, M), lambda b: (b, 0, 0)),
                   pl.BlockSpec((1, M, width), lambda b: (b, 0, 0))],
        out_shape=[jax.ShapeDtypeStruct((B, width, M), jnp.bfloat16),
                   jax.ShapeDtypeStruct((B, M, width), jnp.bfloat16)],
        compiler_params=_cparams("arbitrary"),
        name="mem_kv",
    )(mem, w_mem_kv)


def _in_proj_kernel(x_ref, lng_ref, lnb_ref, wa_ref, gq_ref, gk_ref, gql_ref, gkvl_ref,
                    wqb_ref, wkvk_ref, wkvv_ref, ca_ref, sa_ref, cb_ref, sb_ref, mkt_ref, mv_ref,
                    h_ref, qa_ref, kta_ref, va_ref, qb_ref, ktb_ref, vb_ref, oc_ref):
    h = _layer_norm(x_ref[...], lng_ref[...], lnb_ref[...])
    h_ref[...] = h
    proj = _bdot(h, wa_ref[...])
    lane = _lane_iota((h.shape[0], LANES))
    lo_mask = lane < HALF
    ca, sa, cb, sb = ca_ref[...], sa_ref[...], cb_ref[...], sb_ref[...]

    q_scale = GQA_HEAD_DIM ** -0.5
    per_group = GQA_HEADS // GQA_KV_HEADS
    for c in range(GQA_HEADS // 2):
        slab = proj[:, A_QA + c * LANES:A_QA + (c + 1) * LANES]
        slab = _rope(slab * _half_rms_scale(slab) * gq_ref[...], ca, sa, 16) * q_scale
        swapped = pltpu.roll(slab, HALF, 1)
        for half in range(2):
            head = 2 * c + half
            group_lo = (head // per_group) == 0
            src = slab if (half == 0) == group_lo else swapped
            keep = lo_mask if group_lo else jnp.logical_not(lo_mask)
            qa_ref[:, head * LANES:(head + 1) * LANES] = jnp.where(keep, src, 0.0).astype(jnp.bfloat16)

    ka = proj[:, A_KA:A_KA + LANES]
    ka = _rope(ka * _half_rms_scale(ka) * gk_ref[...], ca, sa, 16)
    kta_ref[0] = ka.T.astype(jnp.bfloat16)
    va = proj[:, A_VA:A_VA + LANES]
    va_sw = pltpu.roll(va, HALF, 1)
    va_ref[:, :LANES] = jnp.where(lo_mask, va, va_sw).astype(jnp.bfloat16)
    va_ref[:, LANES:] = jnp.where(lo_mask, va_sw, va).astype(jnp.bfloat16)

    ql = proj[:, A_QLAT:A_KVLAT]
    ql = ql * lax.rsqrt(jnp.mean(ql * ql, axis=-1, keepdims=True) + RMS_EPS) * gql_ref[...]
    qm = _bdot(ql, wqb_ref[...])
    kvl = proj[:, A_KVLAT:A_KROPE]
    kvl = kvl * lax.rsqrt(jnp.mean(kvl * kvl, axis=-1, keepdims=True) + RMS_EPS) * gkvl_ref[...]
    kn = _bdot(kvl, wkvk_ref[...])
    vb_ref[...] = _bdot(kvl, wkvv_ref[...]).astype(jnp.bfloat16)
    k_pe = _rope(proj[:, A_KROPE:A_QM], cb, sb, 8)
    qb_scale = (MLA_NOPE_DIM + MLA_ROPE_DIM) ** -0.5
    for hd in range(MLA_HEADS):
        sl = slice(hd * LANES, (hd + 1) * LANES)
        qb_ref[:, sl] = (_rope(qm[:, sl], cb, sb, 8) * qb_scale).astype(jnp.bfloat16)
        ktb_ref[0, sl, :] = (kn[:, sl] + k_pe).T.astype(jnp.bfloat16)

    qc = proj[:, A_QM:A_END] * (MEM_HEAD_DIM ** -0.5)
    for hd in range(MEM_HEADS):
        sl = slice(hd * MEM_HEAD_DIM, (hd + 1) * MEM_HEAD_DIM)
        p, l = _softmax_rows(_bdot(qc[:, sl], mkt_ref[0, sl, :]))
        oc_ref[:, sl] = (_bdot(p, mv_ref[0, :, sl]) / l).astype(jnp.bfloat16)


def _in_proj(x2, ln_g, ln_b, wa, gq, gk, gql, gkvl, wqb, wkvk, wkvv, ca, sa, cb, sb, mkt, mv, B, S):
    N, D = x2.shape
    T = ROW_TILE
    tiles_per_seq = S // T
    row = lambda i: (i, 0)
    const = lambda i: (0, 0)
    pos = lambda i: (i % tiles_per_seq, 0)
    batch3 = lambda i: (i // tiles_per_seq, 0, 0)
    kt_map = lambda i: (i // tiles_per_seq, 0, i % tiles_per_seq)
    full = lambda a: pl.BlockSpec(a.shape, const)
    return pl.pallas_call(
        _in_proj_kernel,
        grid=(N // T,),
        in_specs=[pl.BlockSpec((T, D), row), full(ln_g), full(ln_b), full(wa), full(gq), full(gk),
                  full(gql), full(gkvl), full(wqb), full(wkvk), full(wkvv),
                  pl.BlockSpec((T, LANES), pos), pl.BlockSpec((T, LANES), pos),
                  pl.BlockSpec((T, LANES), pos), pl.BlockSpec((T, LANES), pos),
                  pl.BlockSpec((1,) + mkt.shape[1:], batch3), pl.BlockSpec((1,) + mv.shape[1:], batch3)],
        out_specs=[pl.BlockSpec((T, D), row),
                   pl.BlockSpec((T, GQA_HEADS * LANES), row),
                   pl.BlockSpec((1, LANES, T), kt_map),
                   pl.BlockSpec((T, 2 * LANES), row),
                   pl.BlockSpec((T, MLA_HEADS * LANES), row),
                   pl.BlockSpec((1, MLA_HEADS * LANES, T), kt_map),
                   pl.BlockSpec((T, MLA_HEADS * MLA_V_DIM), row),
                   pl.BlockSpec((T, MEM_HEADS * MEM_HEAD_DIM), row)],
        out_shape=[jax.ShapeDtypeStruct((N, D), jnp.float32),
                   jax.ShapeDtypeStruct((N, GQA_HEADS * LANES), jnp.bfloat16),
                   jax.ShapeDtypeStruct((B, LANES, S), jnp.bfloat16),
                   jax.ShapeDtypeStruct((N, 2 * LANES), jnp.bfloat16),
                   jax.ShapeDtypeStruct((N, MLA_HEADS * LANES), jnp.bfloat16),
                   jax.ShapeDtypeStruct((B, MLA_HEADS * LANES, S), jnp.bfloat16),
                   jax.ShapeDtypeStruct((N, MLA_HEADS * MLA_V_DIM), jnp.bfloat16),
                   jax.ShapeDtypeStruct((N, MEM_HEADS * MEM_HEAD_DIM), jnp.bfloat16)],
        compiler_params=_cparams("arbitrary"),
        name="in_proj",
    )(x2, ln_g, ln_b, wa, gq, gk, gql, gkvl, wqb, wkvk, wkvv, ca, sa, cb, sb, mkt, mv)


def _attention_kernel(q_ref, kt_ref, v_ref, o_ref, *, shared_k):
    v = v_ref[0]
    lane = _lane_iota(v.shape)
    zero = jnp.zeros_like(v)
    v_halves = (jnp.where(lane < HALF, v, zero), jnp.where(lane >= HALF, v, zero))
    out = None
    for hd in range(2):
        q = q_ref[0, :, hd * LANES:(hd + 1) * LANES]
        kt = kt_ref[0] if shared_k else kt_ref[0, hd * LANES:(hd + 1) * LANES, :]
        s = jnp.dot(q, kt, preferred_element_type=jnp.float32)
        p, l = _softmax_rows(s)
        o = jnp.dot(p.astype(jnp.bfloat16), v_halves[hd], preferred_element_type=jnp.float32) / l
        out = o if out is None else out + o
    o_ref[0] = out.astype(jnp.bfloat16)


def _attention(q, kt, v, *, shared_k, v_block_of_pair):
    B, S, qw = q.shape
    pairs = qw // (2 * LANES)
    TQ = ATTN_Q_TILE
    k_rows = LANES if shared_k else 2 * LANES
    kt_map = (lambda b, j, i: (b, 0, 0)) if shared_k else (lambda b, j, i: (b, j, 0))
    return pl.pallas_call(
        functools.partial(_attention_kernel, shared_k=shared_k),
        grid=(B, pairs, S // TQ),
        in_specs=[pl.BlockSpec((1, TQ, 2 * LANES), lambda b, j, i: (b, i, j)),
                  pl.BlockSpec((1, k_rows, S), kt_map),
                  pl.BlockSpec((1, S, LANES), lambda b, j, i: (b, 0, v_block_of_pair(j)))],
        out_specs=pl.BlockSpec((1, TQ, LANES), lambda b, j, i: (b, i, j)),
        out_shape=jax.ShapeDtypeStruct((B, S, pairs * LANES), jnp.bfloat16),
        compiler_params=_cparams("arbitrary", "arbitrary", "arbitrary"),
        name="attention_gqa" if shared_k else "attention_mla",
    )(q, kt, v)


def _split3(x):
    a = x.astype(jnp.bfloat16)
    r = x - a.astype(jnp.float32)
    b = r.astype(jnp.bfloat16)
    c = (r - b.astype(jnp.float32)).astype(jnp.bfloat16)
    return a, b, c


def _mix_out_kernel(h_ref, oa_ref, ob_ref, oc_ref, wg_ref, bg_ref, wa_ref, wb_ref, wc_ref, wo_ref,
                    g_ref, b_ref, wr_ref, br_ref, h1_ref, idx_ref, gate_ref):
    h = h_ref[...]
    gates = jax.nn.sigmoid(_bdot(h, wg_ref[...]) + bg_ref[...])
    D = D_MODEL
    merged = (gates[:, :D] * _bdot(oa_ref[...], wa_ref[...])
              + gates[:, D:2 * D] * _bdot(ob_ref[...], wb_ref[...])
              + gates[:, 2 * D:] * _bdot(oc_ref[...], wc_ref[...]))
    mix = _bdot(merged, wo_ref[...])
    h1 = _layer_norm(DEEPNORM_ALPHA * h + mix, g_ref[...], b_ref[...])
    h1_ref[...] = h1

    ha, hb, hc = _split3(h1)
    wa_, wb_, wc_ = wr_ref[0], wr_ref[1], wr_ref[2]
    dot = lambda a, b: jnp.dot(a, b, preferred_element_type=jnp.float32)
    logits = (dot(ha, wa_) + (dot(ha, wb_) + dot(hb, wa_))
              + (dot(ha, wc_) + dot(hb, wb_) + dot(hc, wa_))) + br_ref[...]
    lane = _lane_iota(logits.shape)
    logits = jnp.where(lane < N_EXPERTS, logits, NEG_INF)

    idx_out = jnp.zeros(logits.shape, jnp.int32)
    val_out = jnp.zeros(logits.shape, jnp.float32)
    top = None
    for k in range(TOP_K):
        m = jnp.max(logits, axis=-1, keepdims=True)
        idx = jnp.min(jnp.where(logits == m, lane, LANES), axis=-1, keepdims=True)
        logits = jnp.where(lane == idx, NEG_INF, logits)
        top = m if top is None else top
        idx_out = jnp.where(lane == k, idx, idx_out)
        val_out = jnp.where(lane == k, jnp.exp(m - top), val_out)
    idx_ref[...] = idx_out
    gate_ref[...] = val_out / jnp.sum(val_out, axis=-1, keepdims=True)


def _mix_out(h, oa, ob, oc, wg, bg, wa, wb, wc, wo, g, b, wr3, br):
    N, D = h.shape
    T = ROW_TILE
    row = lambda i: (i, 0)
    full = lambda a: pl.BlockSpec(a.shape, lambda i: (0,) * a.ndim)
    return pl.pallas_call(
        _mix_out_kernel,
        grid=(N // T,),
        in_specs=[pl.BlockSpec((T, D), row), pl.BlockSpec((T, oa.shape[1]), row),
                  pl.BlockSpec((T, ob.shape[1]), row), pl.BlockSpec((T, oc.shape[1]), row),
                  full(wg), full(bg), full(wa), full(wb), full(wc), full(wo), full(g), full(b),
                  full(wr3), full(br)],
        out_specs=[pl.BlockSpec((T, D), row), pl.BlockSpec((T, LANES), row), pl.BlockSpec((T, LANES), row)],
        out_shape=[jax.ShapeDtypeStruct((N, D), jnp.float32),
                   jax.ShapeDtypeStruct((N, LANES), jnp.int32),
                   jax.ShapeDtypeStruct((N, LANES), jnp.float32)],
        compiler_params=_cparams("arbitrary"),
        name="mix_out",
    )(h, oa, ob, oc, wg, bg, wa, wb, wc, wo, g, b, wr3, br)


def _route_kernel(idx_ref, pos_ref, carry_out_ref, cnt_out_ref, total_ref, carry_ref):
    @pl.when(pl.program_id(0) == 0)
    def _():
        carry_ref[...] = jnp.zeros_like(carry_ref)

    idx = idx_ref[...]
    T = idx.shape[0]
    lane = _lane_iota(idx.shape)
    sel = [lane == idx[:, k:k + 1] for k in range(TOP_K)]
    onehot = sum(s.astype(jnp.float32) for s in sel)
    r = lax.broadcasted_iota(jnp.int32, (T, T), 0)
    c = lax.broadcasted_iota(jnp.int32, (T, T), 1)
    before = (c < r).astype(jnp.bfloat16)
    prefix = jnp.dot(before, onehot.astype(jnp.bfloat16), preferred_element_type=jnp.float32)
    cnt = jnp.broadcast_to(jnp.sum(onehot, axis=0, keepdims=True), carry_ref.shape)
    cnt = jnp.floor((cnt + (SUBLANES - 1)) * (1.0 / SUBLANES)) * SUBLANES
    er = lax.broadcasted_iota(jnp.int32, (LANES, LANES), 0)
    ec = lax.broadcasted_iota(jnp.int32, (LANES, LANES), 1)
    local_start = jnp.dot(cnt.astype(jnp.bfloat16), (er < ec).astype(jnp.bfloat16),
                          preferred_element_type=jnp.float32)
    target = prefix + local_start[0:1, :]
    out = jnp.zeros(idx.shape, jnp.int32)
    for k in range(TOP_K):
        pk = jnp.sum(jnp.where(sel[k], target, 0.0), axis=-1, keepdims=True)
        out = jnp.where(lane == k, pk.astype(jnp.int32), out)
    pos_ref[...] = out
    carry_out_ref[...] = carry_ref[...].astype(jnp.int32)
    cnt_out_ref[...] = cnt.astype(jnp.int32)
    carry_ref[...] = carry_ref[...] + cnt
    total_ref[...] = carry_ref[...].astype(jnp.int32)


def _route(idx):
    N = idx.shape[0]
    T = ROW_TILE
    n_tiles = N // T
    tile8 = pl.BlockSpec((8, LANES), lambda i: (i, 0))
    return pl.pallas_call(
        _route_kernel,
        grid=(n_tiles,),
        in_specs=[pl.BlockSpec((T, LANES), lambda i: (i, 0))],
        out_specs=[pl.BlockSpec((T, LANES), lambda i: (i, 0)), tile8, tile8,
                   pl.BlockSpec((8, LANES), lambda i: (0, 0))],
        out_shape=[jax.ShapeDtypeStruct((N, LANES), jnp.int32),
                   jax.ShapeDtypeStruct((n_tiles * 8, LANES), jnp.int32),
                   jax.ShapeDtypeStruct((n_tiles * 8, LANES), jnp.int32),
                   jax.ShapeDtypeStruct((8, LANES), jnp.int32)],
        scratch_shapes=[pltpu.VMEM((8, LANES), jnp.float32)],
        compiler_params=_cparams("arbitrary"),
        name="route",
    )(idx)


STRIP_SIZES = tuple(ROW_TILE >> s for s in range((ROW_TILE // SUBLANES).bit_length()))
LOCAL_ROWS = TOP_K * ROW_TILE + ROW_TILE
assert LOCAL_ROWS >= TOP_K * ROW_TILE + N_EXPERTS * (SUBLANES - 1)


def _for_each_strip(cnt_ref, base_ref, tile, fn):
    def body(e, off):
        c = cnt_ref[tile * N_EXPERTS + e]
        d = base_ref[tile * N_EXPERTS + e]
        for bit, size in enumerate(STRIP_SIZES):
            above = c & (-2 * size)

            @pl.when((c & size) != 0)
            def _():
                fn(pl.multiple_of(off + above, SUBLANES), pl.multiple_of(d + above, SUBLANES), size, bit)
        return off + c

    lax.fori_loop(0, N_EXPERTS, body, 0)


def _dispatch_kernel(cnt_ref, base_ref, tail_ref, pos_ref, h_ref, xs_ref, buf, zeros, sems, zsem):
    i = pl.program_id(0)
    last = pl.num_programs(0) - 1
    T = h_ref.shape[0]
    cur = i % 2

    @pl.when(i == 0)
    def _():
        zeros[...] = jnp.zeros_like(zeros)

        def tail_copy(e):
            row = pl.multiple_of(jnp.maximum(tail_ref[e], 0), SUBLANES)
            return pltpu.make_async_copy(zeros, xs_ref.at[pl.ds(row, EXPERT_BLOCK), :], zsem)

        def spare_copy(blk):
            row = pl.multiple_of(blk * EXPERT_BLOCK, EXPERT_BLOCK)
            return pltpu.make_async_copy(zeros, xs_ref.at[pl.ds(row, EXPERT_BLOCK), :], zsem)

        first_spare = tail_ref[N_EXPERTS] // EXPERT_BLOCK
        for act in ("start", "wait"):
            def body(e, carry, act=act):
                @pl.when(tail_ref[e] >= 0)
                def _():
                    getattr(tail_copy(e), act)()
                return carry
            lax.fori_loop(0, N_EXPERTS, body, 0)

            def spare(blk, carry, act=act):
                getattr(spare_copy(blk), act)()
                return carry
            lax.fori_loop(first_spare, xs_ref.shape[0] // EXPERT_BLOCK, spare, 0)

    pos_t = pos_ref[...].astype(jnp.float32).T.astype(jnp.int32)
    rows = lax.broadcasted_iota(jnp.int32, (LOCAL_ROWS, T), 0)
    hit = rows == pos_t[0:1, :]
    for k in range(1, TOP_K):
        hit = jnp.logical_or(hit, rows == pos_t[k:k + 1, :])
    buf[cur] = jnp.dot(hit.astype(jnp.bfloat16), h_ref[...].astype(jnp.bfloat16),
                       preferred_element_type=jnp.float32)

    def strip(buf_slot):
        def make(local_row, slot_row, size, bit):
            return pltpu.make_async_copy(buf.at[buf_slot, pl.ds(local_row, size), :],
                                         xs_ref.at[pl.ds(slot_row, size), :], sems.at[bit])
        return make

    @pl.when(i > 0)
    def _():
        _for_each_strip(cnt_ref, base_ref, i - 1, lambda *a: strip(1 - cur)(*a).wait())

    _for_each_strip(cnt_ref, base_ref, i, lambda *a: strip(cur)(*a).start())

    @pl.when(i == last)
    def _():
        _for_each_strip(cnt_ref, base_ref, i, lambda *a: strip(cur)(*a).wait())


def _dispatch(cnt_flat, base_flat, tail, pos, h1, n_slots):
    N, D = h1.shape
    T = ROW_TILE
    return pl.pallas_call(
        _dispatch_kernel,
        grid_spec=pltpu.PrefetchScalarGridSpec(
            num_scalar_prefetch=3,
            grid=(N // T,),
            in_specs=[pl.BlockSpec((T, LANES), lambda i, c, b, t: (i, 0)),
                      pl.BlockSpec((T, D), lambda i, c, b, t: (i, 0))],
            out_specs=pl.BlockSpec(memory_space=pl.ANY),
            scratch_shapes=[pltpu.VMEM((2, LOCAL_ROWS, D), jnp.float32),
                            pltpu.VMEM((EXPERT_BLOCK, D), jnp.float32),
                            pltpu.SemaphoreType.DMA((len(STRIP_SIZES),)),
                            pltpu.SemaphoreType.DMA(())]),
        out_shape=jax.ShapeDtypeStruct((n_slots, D), jnp.float32),
        compiler_params=_cparams("arbitrary"),
        name="dispatch",
    )(cnt_flat, base_flat, tail, pos, h1)


def _experts_kernel(blk_e_ref, n_used_ref, x_ref, wi_ref, bi_ref, wo_ref, bo_ref, y_ref):
    del blk_e_ref
    i = pl.program_id(0)

    @pl.when(i < n_used_ref[0])
    def _():
        hb = _bdot(x_ref[...], wi_ref[0]) + bi_ref[0]
        De = D_MODEL
        x_glu = jnp.minimum(hb[:, :De], SWIGLU_LIMIT)
        x_lin = jnp.clip(hb[:, De:], -SWIGLU_LIMIT, SWIGLU_LIMIT)
        act = x_glu * jax.nn.sigmoid(SWIGLU_ALPHA * x_glu) * (x_lin + 1.0)
        y_ref[...] = _bdot(act, wo_ref[0]) + bo_ref[0]

    @pl.when(i >= n_used_ref[0])
    def _():
        y_ref[...] = jnp.zeros_like(y_ref)


def _experts(blk_e, n_used, xs, w_in, b_in, w_out, b_out):
    n_slots, D = xs.shape
    R = EXPERT_BLOCK
    E, _, F = w_in.shape
    return pl.pallas_call(
        _experts_kernel,
        grid_spec=pltpu.PrefetchScalarGridSpec(
            num_scalar_prefetch=2,
            grid=(n_slots // R,),
            in_specs=[pl.BlockSpec((R, D), lambda i, e, n: (jnp.minimum(i, n[0] - 1), 0)),
                      pl.BlockSpec((1, D, F), lambda i, e, n: (e[i], 0, 0)),
                      pl.BlockSpec((1, 1, F), lambda i, e, n: (e[i], 0, 0)),
                      pl.BlockSpec((1, F // 2, D), lambda i, e, n: (e[i], 0, 0)),
                      pl.BlockSpec((1, 1, D), lambda i, e, n: (e[i], 0, 0))],
            out_specs=pl.BlockSpec((R, D), lambda i, e, n: (i, 0))),
        out_shape=jax.ShapeDtypeStruct((n_slots, D), jnp.float32),
        compiler_params=_cparams("arbitrary"),
        name="experts",
    )(blk_e, n_used, xs, w_in, b_in, w_out, b_out)


def _split2(x):
    hi = x.astype(jnp.bfloat16)
    return hi, (x - hi.astype(jnp.float32)).astype(jnp.bfloat16)


def _combine_kernel(cnt_ref, base_ref, pos_ref, gate_ref, h_ref, g_ref, b_ref, ys_ref, o_ref, buf, sems):
    i = pl.program_id(0)
    last = pl.num_programs(0) - 1
    T = h_ref.shape[0]
    cur = i % 2

    def strip(buf_slot):
        def make(local_row, slot_row, size, bit):
            return pltpu.make_async_copy(ys_ref.at[pl.ds(slot_row, size), :],
                                         buf.at[buf_slot, pl.ds(local_row, size), :], sems.at[bit])
        return make

    @pl.when(i == 0)
    def _():
        buf[...] = jnp.zeros_like(buf)
        _for_each_strip(cnt_ref, base_ref, i, lambda *a: strip(cur)(*a).start())

    _for_each_strip(cnt_ref, base_ref, i, lambda *a: strip(cur)(*a).wait())

    @pl.when(i < last)
    def _():
        _for_each_strip(cnt_ref, base_ref, i + 1, lambda *a: strip(1 - cur)(*a).start())

    pos, gate = pos_ref[...], gate_ref[...]
    col = lax.broadcasted_iota(jnp.int32, (T, LOCAL_ROWS), 1)
    u = jnp.zeros(col.shape, jnp.float32)
    for k in range(TOP_K):
        u = jnp.where(col == pos[:, k:k + 1], gate[:, k:k + 1], u)
    u_hi, u_lo = _split2(u)
    y_hi, y_lo = _split2(buf[cur])
    dot = lambda a, b: jnp.dot(a, b, preferred_element_type=jnp.float32)
    ffn = dot(u_hi, y_hi) + (dot(u_lo, y_hi) + dot(u_hi, y_lo))
    o_ref[...] = _layer_norm(DEEPNORM_ALPHA * h_ref[...] + ffn, g_ref[...], b_ref[...])


def _combine(cnt_flat, base_flat, pos, gate, h1, g, b, ys):
    N, D = h1.shape
    T = ROW_TILE
    row = lambda i, c, s: (i, 0)
    const = lambda i, c, s: (0, 0)
    return pl.pallas_call(
        _combine_kernel,
        grid_spec=pltpu.PrefetchScalarGridSpec(
            num_scalar_prefetch=2,
            grid=(N // T,),
            in_specs=[pl.BlockSpec((T, LANES), row), pl.BlockSpec((T, LANES), row), pl.BlockSpec((T, D), row),
                      pl.BlockSpec(g.shape, const), pl.BlockSpec(b.shape, const),
                      pl.BlockSpec(memory_space=pl.ANY)],
            out_specs=pl.BlockSpec((T, D), row),
            scratch_shapes=[pltpu.VMEM((2, LOCAL_ROWS, D), jnp.float32),
                            pltpu.SemaphoreType.DMA((len(STRIP_SIZES),))]),
        out_shape=jax.ShapeDtypeStruct((N, D), jnp.float32),
        compiler_params=_cparams("arbitrary"),
        name="combine",
    )(cnt_flat, base_flat, pos, gate, h1, g, b, ys)


def _rope_tables(S):
    t = np.arange(S)
    row, col = (t // GRID_W).astype(np.float64), (t % GRID_W).astype(np.float64)

    def block(half):
        inv = ROPE_THETA ** (-np.arange(half, dtype=np.float64) / half)
        ar, ac = row[:, None] * inv[None, :], col[:, None] * inv[None, :]
        cos = np.concatenate([np.cos(ar), np.cos(ar), np.cos(ac), np.cos(ac)], axis=1)
        sin = np.concatenate([-np.sin(ar), np.sin(ar), -np.sin(ac), np.sin(ac)], axis=1)
        return cos, sin

    ca64, sa64 = block(GQA_HEAD_DIM // 4)
    ca, sa = np.tile(ca64, (1, 2)), np.tile(sa64, (1, 2))
    cb32, sb32 = block(MLA_ROPE_DIM // 4)
    ones, zeros = np.ones((S, MLA_NOPE_DIM)), np.zeros((S, MLA_NOPE_DIM))
    cb = np.concatenate([ones, cb32, ones[:, :LANES - MLA_NOPE_DIM - MLA_ROPE_DIM]], axis=1)
    sb = np.concatenate([zeros, sb32, zeros[:, :LANES - MLA_NOPE_DIM - MLA_ROPE_DIM]], axis=1)
    return tuple(jnp.asarray(a, jnp.float32) for a in (ca, sa, cb, sb))


def _layer(h_in_is_x, x2, mem, B, S, ln_g, ln_b, w_in_proj, b_gate, gqa_q_norm, gqa_k_norm, mla_q_norm,
           mla_kv_norm, w_mla_qb, w_mla_kvb, w_mem_kv, w_br_gqa, w_br_mla, w_br_mem, w_out,
           ln1_g, ln1_b, w_router, b_router, w_exp_in, b_exp_in, w_exp_out, b_exp_out, ln2_g, ln2_b):
    del h_in_is_x
    bf = jnp.bfloat16
    N, D = x2.shape
    row2 = lambda a: a.reshape(1, -1)

    W = w_in_proj
    zc = lambda n: jnp.zeros((D, n), W.dtype)
    wa = jnp.concatenate([W[:, :OFF_KROPE], zc(HALF), W[:, OFF_KROPE:OFF_QM],
                          zc(LANES - HALF - MLA_ROPE_DIM), W[:, OFF_QM:OFF_GATE]], axis=1).astype(bf)
    wg = W[:, OFF_GATE:].astype(bf)
    qd = MLA_NOPE_DIM + MLA_ROPE_DIM
    wqb = jnp.pad(w_mla_qb.reshape(MLA_Q_LORA, MLA_HEADS, qd),
                  ((0, 0), (0, 0), (0, LANES - qd))).reshape(MLA_Q_LORA, MLA_HEADS * LANES).astype(bf)
    kvb = w_mla_kvb.reshape(MLA_KV_LORA, MLA_HEADS, MLA_NOPE_DIM + MLA_V_DIM)
    wkvk = jnp.pad(kvb[:, :, :MLA_NOPE_DIM], ((0, 0), (0, 0), (0, LANES - MLA_NOPE_DIM))
                   ).reshape(MLA_KV_LORA, MLA_HEADS * LANES).astype(bf)
    wkvv = kvb[:, :, MLA_NOPE_DIM:].reshape(MLA_KV_LORA, MLA_HEADS * MLA_V_DIM).astype(bf)
    gq = row2(jnp.tile(gqa_q_norm, 2))
    gk = row2(jnp.tile(gqa_k_norm, 2))
    ca, sa, cb, sb = _rope_tables(S)

    mkt, mv = _mem_kv(mem, w_mem_kv.astype(bf))
    h, qa, kta, va, qb, ktb, vb, oc = _in_proj(
        x2, row2(ln_g), row2(ln_b), wa, gq, gk, row2(mla_q_norm), row2(mla_kv_norm),
        wqb, wkvk, wkvv, ca, sa, cb, sb, mkt, mv, B, S)

    pairs_per_group = GQA_HEADS // GQA_KV_HEADS // 2
    oa = _attention(qa.reshape(B, S, -1), kta, va.reshape(B, S, -1), shared_k=True,
                    v_block_of_pair=lambda j: j // pairs_per_group)
    ob = _attention(qb.reshape(B, S, -1), ktb, vb.reshape(B, S, -1), shared_k=False,
                    v_block_of_pair=lambda j: j)

    wr = jnp.pad(w_router, ((0, 0), (0, LANES - N_EXPERTS)))
    wr3 = jnp.stack(_split3(wr))
    br = row2(jnp.pad(b_router, (0, LANES - N_EXPERTS)))
    h1, idx, gate = _mix_out(h, oa.reshape(N, -1), ob.reshape(N, -1), oc, wg, row2(b_gate),
                             w_br_gqa.astype(bf), w_br_mla.astype(bf), w_br_mem.astype(bf),
                             w_out.astype(bf), row2(ln1_g), row2(ln1_b), wr3, br)

    pos, carry_t, cnt_t, total = _route(idx)
    counts = total[0, :N_EXPERTS]
    R = EXPERT_BLOCK
    padded = (counts + R - 1) // R * R
    pad_end = jnp.cumsum(padded)
    pad_start = (pad_end - padded).astype(jnp.int32)
    n_tiles = N // ROW_TILE
    max_rows = N * TOP_K + n_tiles * N_EXPERTS * (SUBLANES - 1) + N_EXPERTS * (R - 1)
    n_blocks = -(-max_rows // R)
    blk_start = jnp.arange(n_blocks, dtype=jnp.int32) * R
    blk_e = jnp.sum((pad_end[None, :] <= blk_start[:, None]).astype(jnp.int32), axis=1)
    n_used = (pad_end[-1:] // R).astype(jnp.int32)
    last_e = jnp.max(jnp.where(blk_start < pad_end[-1], blk_e, 0))
    blk_e = jnp.minimum(blk_e, last_e).astype(jnp.int32)
    per_tile = lambda a: a.reshape(n_tiles, 8, LANES)[:, 0, :N_EXPERTS]
    cnt_flat = per_tile(cnt_t).reshape(-1)
    base_flat = (per_tile(carry_t) + pad_start[None, :]).reshape(-1)
    tail = jnp.concatenate([jnp.where(padded > 0, pad_end - R, -1), pad_end[-1:]]).astype(jnp.int32)

    xs = _dispatch(cnt_flat, base_flat, tail, pos, h1, n_blocks * R)
    ys = _experts(blk_e, n_used, xs, w_exp_in.astype(bf), b_exp_in[:, None, :],
                  w_exp_out.astype(bf), b_exp_out[:, None, :])
    return _combine(cnt_flat, base_flat, pos, gate, h1, row2(ln2_g), row2(ln2_b), ys)


def kernel(x, mem, ln_in_g, ln_in_b, w_in_proj, b_gate, gqa_q_norm, gqa_k_norm, mla_q_norm, mla_kv_norm, w_mla_qb, w_mla_kvb, w_mem_kv, w_br_gqa, w_br_mla, w_br_mem, w_out, ln1_g, ln1_b, w_router, b_router, w_exp_in, b_exp_in, w_exp_out, b_exp_out, ln2_g, ln2_b):
    B, S, D = x.shape
    depth = w_in_proj.shape[0]
    assert depth == 1, "the input LayerNorm is fused into the first (only) layer's projection kernel"
    out = _layer(True, x.reshape(B * S, D), mem, B, S, ln_in_g, ln_in_b, w_in_proj[0], b_gate[0],
                 gqa_q_norm[0], gqa_k_norm[0], mla_q_norm[0], mla_kv_norm[0], w_mla_qb[0], w_mla_kvb[0],
                 w_mem_kv[0], w_br_gqa[0], w_br_mla[0], w_br_mem[0], w_out[0], ln1_g[0], ln1_b[0],
                 w_router[0], b_router[0], w_exp_in[0], b_exp_in[0], w_exp_out[0], b_exp_out[0],
                 ln2_g[0], ln2_b[0])
    return out.reshape(B, S, D)
```

```python
import functools

import jax
import jax.numpy as jnp
import numpy as np
from jax import lax
from jax.experimental import pallas as pl
from jax.experimental.pallas import tpu as pltpu

D_MODEL = 1024
MEM_LEN = 256
GRID_W = 64
ROPE_THETA = 10000.0
RMS_EPS = 1e-6
LN_EPS = 1e-5

GQA_HEADS = 8
GQA_KV_HEADS = 2
GQA_HEAD_DIM = 64
MLA_HEADS = 8
MLA_NOPE_DIM = 64
MLA_ROPE_DIM = 32
MLA_V_DIM = 64
MLA_Q_LORA = 384
MLA_KV_LORA = 256
MEM_HEADS = 4
MEM_HEAD_DIM = 128

N_EXPERTS = 32
TOP_K = 4
SWIGLU_LIMIT = 7.0
SWIGLU_ALPHA = 1.702
DEEPNORM_ALPHA = 2.0 ** 0.25

LANES = 128
SUBLANES = 8
HALF = LANES // 2
NEG_INF = float("-inf")
LOG2_E = 1.4426950408889634

OFF_QA, OFF_KA, OFF_VA, OFF_QLAT, OFF_KVLAT, OFF_KROPE, OFF_QM, OFF_GATE = (
    0, 512, 640, 768, 1152, 1408, 1440, 1952)
IN_PROJ_W = 5024
A_QA, A_KA, A_VA, A_QLAT, A_KVLAT, A_KROPE, A_QM, A_END = 0, 512, 640, 768, 1152, 1408, 1536, 2048

ROW_TILE = 256
ATTN_Q_TILE = 256
EXPERT_BLOCK = 256
VMEM_LIMIT = 48 * 1024 * 1024


def _cparams(*sem):
    return pltpu.CompilerParams(dimension_semantics=sem, vmem_limit_bytes=VMEM_LIMIT)


def _lane_iota(shape):
    return lax.broadcasted_iota(jnp.int32, shape, len(shape) - 1)


def _layer_norm(x, g, b):
    mu = jnp.mean(x, axis=-1, keepdims=True)
    xc = x - mu
    var = jnp.mean(xc * xc, axis=-1, keepdims=True)
    return xc * lax.rsqrt(var + LN_EPS) * g + b


def _bdot(a, b):
    return jnp.dot(a.astype(jnp.bfloat16), b.astype(jnp.bfloat16), preferred_element_type=jnp.float32)


def _rope(x, cos, sin_signed, pair):
    lane = _lane_iota(x.shape)
    fwd = pltpu.roll(x, LANES - pair, 1)
    bwd = pltpu.roll(x, pair, 1)
    partner = jnp.where((lane % (2 * pair)) < pair, fwd, bwd)
    return x * cos + partner * sin_signed


def _half_rms_scale(x):
    lane = _lane_iota(x.shape)
    sq = x * x
    lo = jnp.sum(jnp.where(lane < HALF, sq, 0.0), axis=-1, keepdims=True)
    hi = jnp.sum(jnp.where(lane >= HALF, sq, 0.0), axis=-1, keepdims=True)
    inv = 1.0 / GQA_HEAD_DIM
    return jnp.where(lane < HALF, lax.rsqrt(lo * inv + RMS_EPS), lax.rsqrt(hi * inv + RMS_EPS))


def _softmax_rows(s):
    m = jnp.max(s, axis=-1, keepdims=True)
    p = jnp.exp(s - m)
    return p, jnp.sum(p, axis=-1, keepdims=True)


def _mem_kv_kernel(mem_ref, w_ref, kt_ref, v_ref):
    kv = _bdot(mem_ref[0], w_ref[...])
    width = MEM_HEADS * MEM_HEAD_DIM
    kt_ref[0] = kv[:, :width].T.astype(jnp.bfloat16)
    v_ref[0] = kv[:, width:].astype(jnp.bfloat16)


def _mem_kv(mem, w_mem_kv):
    B, M, D = mem.shape
    width = MEM_HEADS * MEM_HEAD_DIM
    return pl.pallas_call(
        _mem_kv_kernel,
        grid=(B,),
        in_specs=[pl.BlockSpec((1, M, D), lambda b: (b, 0, 0)),
                  pl.BlockSpec((D, 2 * width), lambda b: (0, 0))],
        out_specs=[pl.BlockSpec((1, width, M), lambda b: (b, 0, 0)),
                   pl.BlockSpec((1, M, width), lambda b: (b, 0, 0))],
        out_shape=[jax.ShapeDtypeStruct((B, width, M), jnp.bfloat16),
                   jax.ShapeDtypeStruct((B, M, width), jnp.bfloat16)],
        compiler_params=_cparams("arbitrary"),
        name="mem_kv",
    )(mem, w_mem_kv)


def _in_proj_kernel(x_ref, lng_ref, lnb_ref, wa_ref, gq_ref, gk_ref, gql_ref, gkvl_ref,
                    wqb_ref, wkvk_ref, wkvv_ref, ca_ref, sa_ref, cb_ref, sb_ref, mkt_ref, mv_ref,
                    h_ref, qa_ref, ka_ref, va_ref, qb_ref, kb_ref, vb_ref, oc_ref):
    h = _layer_norm(x_ref[...], lng_ref[...], lnb_ref[...])
    h_ref[...] = h
    proj = _bdot(h, wa_ref[...])
    lane = _lane_iota((h.shape[0], LANES))
    lo_mask = lane < HALF
    ca, sa, cb, sb = ca_ref[...], sa_ref[...], cb_ref[...], sb_ref[...]

    q_scale = GQA_HEAD_DIM ** -0.5 * LOG2_E
    per_group = GQA_HEADS // GQA_KV_HEADS
    for c in range(GQA_HEADS // 2):
        slab = proj[:, A_QA + c * LANES:A_QA + (c + 1) * LANES]
        slab = _rope(slab * _half_rms_scale(slab) * gq_ref[...], ca, sa, 16) * q_scale
        swapped = pltpu.roll(slab, HALF, 1)
        for half in range(2):
            head = 2 * c + half
            group_lo = (head // per_group) == 0
            src = slab if (half == 0) == group_lo else swapped
            keep = lo_mask if group_lo else jnp.logical_not(lo_mask)
            qa_ref[0, head * LANES:(head + 1) * LANES, :] = jnp.where(keep, src, 0.0).T.astype(jnp.bfloat16)

    ka = proj[:, A_KA:A_KA + LANES]
    ka_ref[...] = _rope(ka * _half_rms_scale(ka) * gk_ref[...], ca, sa, 16).astype(jnp.bfloat16)
    va_ref[0] = proj[:, A_VA:A_VA + LANES].T.astype(jnp.bfloat16)

    ql = proj[:, A_QLAT:A_KVLAT]
    ql = ql * lax.rsqrt(jnp.mean(ql * ql, axis=-1, keepdims=True) + RMS_EPS) * gql_ref[...]
    qm = _bdot(ql, wqb_ref[...])
    kvl = proj[:, A_KVLAT:A_KROPE]
    kvl = kvl * lax.rsqrt(jnp.mean(kvl * kvl, axis=-1, keepdims=True) + RMS_EPS) * gkvl_ref[...]
    kn = _bdot(kvl, wkvk_ref[...])
    vb = _bdot(kvl, wkvv_ref[...])
    for c in range(vb.shape[1] // LANES):
        sl = slice(c * LANES, (c + 1) * LANES)
        vb_ref[0, sl, :] = vb[:, sl].T.astype(jnp.bfloat16)
    k_pe = _rope(proj[:, A_KROPE:A_QM], cb, sb, 8)
    qb_scale = (MLA_NOPE_DIM + MLA_ROPE_DIM) ** -0.5 * LOG2_E
    for hd in range(MLA_HEADS):
        sl = slice(hd * LANES, (hd + 1) * LANES)
        qb_ref[0, sl, :] = (_rope(qm[:, sl], cb, sb, 8) * qb_scale).T.astype(jnp.bfloat16)
        kb_ref[:, sl] = (kn[:, sl] + k_pe).astype(jnp.bfloat16)

    qc = proj[:, A_QM:A_END] * (MEM_HEAD_DIM ** -0.5)
    for hd in range(MEM_HEADS):
        sl = slice(hd * MEM_HEAD_DIM, (hd + 1) * MEM_HEAD_DIM)
        p, l = _softmax_rows(_bdot(qc[:, sl], mkt_ref[0, sl, :]))
        oc_ref[:, sl] = (_bdot(p, mv_ref[0, :, sl]) / l).astype(jnp.bfloat16)


def _in_proj(x2, ln_g, ln_b, wa, gq, gk, gql, gkvl, wqb, wkvk, wkvv, ca, sa, cb, sb, mkt, mv, B, S):
    N, D = x2.shape
    T = ROW_TILE
    tiles_per_seq = S // T
    row = lambda i: (i, 0)
    const = lambda i: (0, 0)
    pos = lambda i: (i % tiles_per_seq, 0)
    batch3 = lambda i: (i // tiles_per_seq, 0, 0)
    kt_map = lambda i: (i // tiles_per_seq, 0, i % tiles_per_seq)
    full = lambda a: pl.BlockSpec(a.shape, const)
    return pl.pallas_call(
        _in_proj_kernel,
        grid=(N // T,),
        in_specs=[pl.BlockSpec((T, D), row), full(ln_g), full(ln_b), full(wa), full(gq), full(gk),
                  full(gql), full(gkvl), full(wqb), full(wkvk), full(wkvv),
                  pl.BlockSpec((T, LANES), pos), pl.BlockSpec((T, LANES), pos),
                  pl.BlockSpec((T, LANES), pos), pl.BlockSpec((T, LANES), pos),
                  pl.BlockSpec((1,) + mkt.shape[1:], batch3), pl.BlockSpec((1,) + mv.shape[1:], batch3)],
        out_specs=[pl.BlockSpec((T, D), row),
                   pl.BlockSpec((1, GQA_HEADS * LANES, T), kt_map),
                   pl.BlockSpec((T, LANES), row),
                   pl.BlockSpec((1, LANES, T), kt_map),
                   pl.BlockSpec((1, MLA_HEADS * LANES, T), kt_map),
                   pl.BlockSpec((T, MLA_HEADS * LANES), row),
                   pl.BlockSpec((1, MLA_HEADS * MLA_V_DIM, T), kt_map),
                   pl.BlockSpec((T, MEM_HEADS * MEM_HEAD_DIM), row)],
        out_shape=[jax.ShapeDtypeStruct((N, D), jnp.float32),
                   jax.ShapeDtypeStruct((B, GQA_HEADS * LANES, S), jnp.bfloat16),
                   jax.ShapeDtypeStruct((N, LANES), jnp.bfloat16),
                   jax.ShapeDtypeStruct((B, LANES, S), jnp.bfloat16),
                   jax.ShapeDtypeStruct((B, MLA_HEADS * LANES, S), jnp.bfloat16),
                   jax.ShapeDtypeStruct((N, MLA_HEADS * LANES), jnp.bfloat16),
                   jax.ShapeDtypeStruct((B, MLA_HEADS * MLA_V_DIM, S), jnp.bfloat16),
                   jax.ShapeDtypeStruct((N, MEM_HEADS * MEM_HEAD_DIM), jnp.bfloat16)],
        compiler_params=_cparams("arbitrary"),
        name="in_proj",
    )(x2, ln_g, ln_b, wa, gq, gk, gql, gkvl, wqb, wkvk, wkvv, ca, sa, cb, sb, mkt, mv)


def _attention_kernel(qt_ref, k_ref, vt_ref, o_ref, s_even, s_odd, *, shared_kv):
    t = pl.program_id(0)

    @pl.when(t == 0)
    def _():
        s_odd[...] = jnp.zeros_like(s_odd)

    def step(s_new, s_old):
        for hd in range(2):
            qt = qt_ref[0, hd * LANES:(hd + 1) * LANES, :]
            k = k_ref[0] if shared_kv else k_ref[0, :, hd * LANES:(hd + 1) * LANES]
            s_new[hd] = jnp.dot(k, qt, preferred_element_type=jnp.float32)
        outs = []
        for hd in range(2):
            vt = vt_ref[0] if shared_kv else vt_ref[0, hd * HALF:(hd + 1) * HALF, :]
            st = s_old[hd]
            p = jnp.exp2(st - jnp.max(st, axis=0, keepdims=True))
            l = jnp.sum(p, axis=0, keepdims=True)
            outs.append(jnp.dot(vt, p.astype(jnp.bfloat16), preferred_element_type=jnp.float32) / l)
        o_ref[0] = jnp.concatenate(outs, axis=0).T.astype(jnp.bfloat16)

    @pl.when(t % 2 == 0)
    def _():
        step(s_even, s_odd)

    @pl.when(t % 2 == 1)
    def _():
        step(s_odd, s_even)


def _attention(qt, k, vt, *, shared_kv):
    B, qh, S = qt.shape
    pairs = qh // (2 * LANES)
    TQ = ATTN_Q_TILE
    nq = S // TQ
    items = B * pairs * nq
    pairs_per_group = pairs // GQA_KV_HEADS

    def item(t):
        return t // (pairs * nq), (t // nq) % pairs, t % nq

    score_item = lambda t: item(jnp.minimum(t, items - 1))
    finish_item = lambda t: item(jnp.maximum(t - 1, 0))

    def qt_map(t):
        b, j, i = score_item(t)
        return b, j, i

    def k_map(t):
        b, j, _ = score_item(t)
        return (b, 0, 0) if shared_kv else (b, 0, j)

    def vt_map(t):
        b, j, _ = finish_item(t)
        return (b, j // pairs_per_group, 0) if shared_kv else (b, j, 0)

    def o_map(t):
        b, j, i = finish_item(t)
        return b, i, j

    k_spec = pl.BlockSpec((1, S, LANES if shared_kv else 2 * LANES), k_map)
    vt_spec = pl.BlockSpec((1, HALF if shared_kv else LANES, S), vt_map)
    return pl.pallas_call(
        functools.partial(_attention_kernel, shared_kv=shared_kv),
        grid=(items + 1,),
        in_specs=[pl.BlockSpec((1, 2 * LANES, TQ), qt_map), k_spec, vt_spec],
        out_specs=pl.BlockSpec((1, TQ, LANES), o_map),
        out_shape=jax.ShapeDtypeStruct((B, S, pairs * LANES), jnp.bfloat16),
        scratch_shapes=[pltpu.VMEM((2, S, TQ), jnp.float32), pltpu.VMEM((2, S, TQ), jnp.float32)],
        compiler_params=_cparams("arbitrary"),
        name="attention_gqa" if shared_kv else "attention_mla",
    )(qt, k, vt)


def _split3(x):
    a = x.astype(jnp.bfloat16)
    r = x - a.astype(jnp.float32)
    b = r.astype(jnp.bfloat16)
    c = (r - b.astype(jnp.float32)).astype(jnp.bfloat16)
    return a, b, c


def _mix_out_kernel(h_ref, oa_ref, ob_ref, oc_ref, wg_ref, bg_ref, wa_ref, wb_ref, wc_ref, wo_ref,
                    g_ref, b_ref, wr_ref, br_ref, h1_ref, idx_ref, gate_ref):
    h = h_ref[...]
    gates = jax.nn.sigmoid(_bdot(h, wg_ref[...]) + bg_ref[...])
    D = D_MODEL
    merged = (gates[:, :D] * _bdot(oa_ref[...], wa_ref[...])
              + gates[:, D:2 * D] * _bdot(ob_ref[...], wb_ref[...])
              + gates[:, 2 * D:] * _bdot(oc_ref[...], wc_ref[...]))
    mix = _bdot(merged, wo_ref[...])
    h1 = _layer_norm(DEEPNORM_ALPHA * h + mix, g_ref[...], b_ref[...])
    h1_ref[...] = h1

    ha, hb, hc = _split3(h1)
    wa_, wb_, wc_ = wr_ref[0], wr_ref[1], wr_ref[2]
    dot = lambda a, b: jnp.dot(a, b, preferred_element_type=jnp.float32)
    logits = (dot(ha, wa_) + (dot(ha, wb_) + dot(hb, wa_))
              + (dot(ha, wc_) + dot(hb, wb_) + dot(hc, wa_))) + br_ref[...]
    lane = _lane_iota(logits.shape)
    logits = jnp.where(lane < N_EXPERTS, logits, NEG_INF)

    idx_out = jnp.zeros(logits.shape, jnp.int32)
    val_out = jnp.zeros(logits.shape, jnp.float32)
    top = None
    for k in range(TOP_K):
        m = jnp.max(logits, axis=-1, keepdims=True)
        idx = jnp.min(jnp.where(logits == m, lane, LANES), axis=-1, keepdims=True)
        logits = jnp.where(lane == idx, NEG_INF, logits)
        top = m if top is None else top
        idx_out = jnp.where(lane == k, idx, idx_out)
        val_out = jnp.where(lane == k, jnp.exp(m - top), val_out)
    idx_ref[...] = idx_out
    gate_ref[...] = val_out / jnp.sum(val_out, axis=-1, keepdims=True)


def _mix_out(h, oa, ob, oc, wg, bg, wa, wb, wc, wo, g, b, wr3, br):
    N, D = h.shape
    T = ROW_TILE
    row = lambda i: (i, 0)
    full = lambda a: pl.BlockSpec(a.shape, lambda i: (0,) * a.ndim)
    return pl.pallas_call(
        _mix_out_kernel,
        grid=(N // T,),
        in_specs=[pl.BlockSpec((T, D), row), pl.BlockSpec((T, oa.shape[1]), row),
                  pl.BlockSpec((T, ob.shape[1]), row), pl.BlockSpec((T, oc.shape[1]), row),
                  full(wg), full(bg), full(wa), full(wb), full(wc), full(wo), full(g), full(b),
                  full(wr3), full(br)],
        out_specs=[pl.BlockSpec((T, D), row), pl.BlockSpec((T, LANES), row), pl.BlockSpec((T, LANES), row)],
        out_shape=[jax.ShapeDtypeStruct((N, D), jnp.float32),
                   jax.ShapeDtypeStruct((N, LANES), jnp.int32),
                   jax.ShapeDtypeStruct((N, LANES), jnp.float32)],
        compiler_params=_cparams("arbitrary"),
        name="mix_out",
    )(h, oa, ob, oc, wg, bg, wa, wb, wc, wo, g, b, wr3, br)


def _route_kernel(idx_ref, pos_ref, carry_out_ref, cnt_out_ref, total_ref, carry_ref):
    @pl.when(pl.program_id(0) == 0)
    def _():
        carry_ref[...] = jnp.zeros_like(carry_ref)

    idx = idx_ref[...]
    T = idx.shape[0]
    lane = _lane_iota(idx.shape)
    sel = [lane == idx[:, k:k + 1] for k in range(TOP_K)]
    onehot = sum(s.astype(jnp.float32) for s in sel)
    r = lax.broadcasted_iota(jnp.int32, (T, T), 0)
    c = lax.broadcasted_iota(jnp.int32, (T, T), 1)
    before = (c < r).astype(jnp.bfloat16)
    prefix = jnp.dot(before, onehot.astype(jnp.bfloat16), preferred_element_type=jnp.float32)
    cnt = jnp.broadcast_to(jnp.sum(onehot, axis=0, keepdims=True), carry_ref.shape)
    cnt = jnp.floor((cnt + (SUBLANES - 1)) * (1.0 / SUBLANES)) * SUBLANES
    er = lax.broadcasted_iota(jnp.int32, (LANES, LANES), 0)
    ec = lax.broadcasted_iota(jnp.int32, (LANES, LANES), 1)
    local_start = jnp.dot(cnt.astype(jnp.bfloat16), (er < ec).astype(jnp.bfloat16),
                          preferred_element_type=jnp.float32)
    target = prefix + local_start[0:1, :]
    out = jnp.zeros(idx.shape, jnp.int32)
    for k in range(TOP_K):
        pk = jnp.sum(jnp.where(sel[k], target, 0.0), axis=-1, keepdims=True)
        out = jnp.where(lane == k, pk.astype(jnp.int32), out)
    pos_ref[...] = out
    carry_out_ref[...] = carry_ref[...].astype(jnp.int32)
    cnt_out_ref[...] = cnt.astype(jnp.int32)
    carry_ref[...] = carry_ref[...] + cnt
    total_ref[...] = carry_ref[...].astype(jnp.int32)


def _route(idx):
    N = idx.shape[0]
    T = ROW_TILE
    n_tiles = N // T
    tile8 = pl.BlockSpec((8, LANES), lambda i: (i, 0))
    return pl.pallas_call(
        _route_kernel,
        grid=(n_tiles,),
        in_specs=[pl.BlockSpec((T, LANES), lambda i: (i, 0))],
        out_specs=[pl.BlockSpec((T, LANES), lambda i: (i, 0)), tile8, tile8,
                   pl.BlockSpec((8, LANES), lambda i: (0, 0))],
        out_shape=[jax.ShapeDtypeStruct((N, LANES), jnp.int32),
                   jax.ShapeDtypeStruct((n_tiles * 8, LANES), jnp.int32),
                   jax.ShapeDtypeStruct((n_tiles * 8, LANES), jnp.int32),
                   jax.ShapeDtypeStruct((8, LANES), jnp.int32)],
        scratch_shapes=[pltpu.VMEM((8, LANES), jnp.float32)],
        compiler_params=_cparams("arbitrary"),
        name="route",
    )(idx)


STRIP_SIZES = tuple(ROW_TILE >> s for s in range((ROW_TILE // SUBLANES).bit_length()))
LOCAL_ROWS = TOP_K * ROW_TILE + ROW_TILE
assert LOCAL_ROWS >= TOP_K * ROW_TILE + N_EXPERTS * (SUBLANES - 1)


def _for_each_strip(cnt_ref, base_ref, tile, fn):
    def body(e, off):
        c = cnt_ref[tile * N_EXPERTS + e]
        d = base_ref[tile * N_EXPERTS + e]
        for bit, size in enumerate(STRIP_SIZES):
            above = c & (-2 * size)

            @pl.when((c & size) != 0)
            def _():
                fn(pl.multiple_of(off + above, SUBLANES), pl.multiple_of(d + above, SUBLANES), size, bit)
        return off + c

    lax.fori_loop(0, N_EXPERTS, body, 0)


def _dispatch_kernel(cnt_ref, base_ref, tail_ref, pos_ref, h_ref, xs_ref, buf, zeros, sems, zsem):
    i = pl.program_id(0)
    last = pl.num_programs(0) - 1
    T = h_ref.shape[0]
    cur = i % 2

    @pl.when(i == 0)
    def _():
        zeros[...] = jnp.zeros_like(zeros)

        def tail_copy(e):
            row = pl.multiple_of(jnp.maximum(tail_ref[e], 0), SUBLANES)
            return pltpu.make_async_copy(zeros, xs_ref.at[pl.ds(row, EXPERT_BLOCK), :], zsem)

        def spare_copy(blk):
            row = pl.multiple_of(blk * EXPERT_BLOCK, EXPERT_BLOCK)
            return pltpu.make_async_copy(zeros, xs_ref.at[pl.ds(row, EXPERT_BLOCK), :], zsem)

        first_spare = tail_ref[N_EXPERTS] // EXPERT_BLOCK
        for act in ("start", "wait"):
            def body(e, carry, act=act):
                @pl.when(tail_ref[e] >= 0)
                def _():
                    getattr(tail_copy(e), act)()
                return carry
            lax.fori_loop(0, N_EXPERTS, body, 0)

            def spare(blk, carry, act=act):
                getattr(spare_copy(blk), act)()
                return carry
            lax.fori_loop(first_spare, xs_ref.shape[0] // EXPERT_BLOCK, spare, 0)

    pos_t = pos_ref[...].astype(jnp.float32).T.astype(jnp.int32)
    rows = lax.broadcasted_iota(jnp.int32, (LOCAL_ROWS, T), 0)
    hit = rows == pos_t[0:1, :]
    for k in range(1, TOP_K):
        hit = jnp.logical_or(hit, rows == pos_t[k:k + 1, :])
    buf[cur] = jnp.dot(hit.astype(jnp.bfloat16), h_ref[...].astype(jnp.bfloat16),
                       preferred_element_type=jnp.float32)

    def strip(buf_slot):
        def make(local_row, slot_row, size, bit):
            return pltpu.make_async_copy(buf.at[buf_slot, pl.ds(local_row, size), :],
                                         xs_ref.at[pl.ds(slot_row, size), :], sems.at[bit])
        return make

    @pl.when(i > 0)
    def _():
        _for_each_strip(cnt_ref, base_ref, i - 1, lambda *a: strip(1 - cur)(*a).wait())

    _for_each_strip(cnt_ref, base_ref, i, lambda *a: strip(cur)(*a).start())

    @pl.when(i == last)
    def _():
        _for_each_strip(cnt_ref, base_ref, i, lambda *a: strip(cur)(*a).wait())


def _dispatch(cnt_flat, base_flat, tail, pos, h1, n_slots):
    N, D = h1.shape
    T = ROW_TILE
    return pl.pallas_call(
        _dispatch_kernel,
        grid_spec=pltpu.PrefetchScalarGridSpec(
            num_scalar_prefetch=3,
            grid=(N // T,),
            in_specs=[pl.BlockSpec((T, LANES), lambda i, c, b, t: (i, 0)),
                      pl.BlockSpec((T, D), lambda i, c, b, t: (i, 0))],
            out_specs=pl.BlockSpec(memory_space=pl.ANY),
            scratch_shapes=[pltpu.VMEM((2, LOCAL_ROWS, D), jnp.float32),
                            pltpu.VMEM((EXPERT_BLOCK, D), jnp.float32),
                            pltpu.SemaphoreType.DMA((len(STRIP_SIZES),)),
                            pltpu.SemaphoreType.DMA(())]),
        out_shape=jax.ShapeDtypeStruct((n_slots, D), jnp.float32),
        compiler_params=_cparams("arbitrary"),
        name="dispatch",
    )(cnt_flat, base_flat, tail, pos, h1)


def _experts_kernel(blk_e_ref, n_used_ref, x_ref, wi_ref, bi_ref, wo_ref, bo_ref, y_ref):
    del blk_e_ref
    i = pl.program_id(0)

    @pl.when(i < n_used_ref[0])
    def _():
        hb = _bdot(x_ref[...], wi_ref[0]) + bi_ref[0]
        De = D_MODEL
        x_glu = jnp.minimum(hb[:, :De], SWIGLU_LIMIT)
        x_lin = jnp.clip(hb[:, De:], -SWIGLU_LIMIT, SWIGLU_LIMIT)
        act = x_glu * jax.nn.sigmoid(SWIGLU_ALPHA * x_glu) * (x_lin + 1.0)
        y_ref[...] = _bdot(act, wo_ref[0]) + bo_ref[0]

    @pl.when(i >= n_used_ref[0])
    def _():
        y_ref[...] = jnp.zeros_like(y_ref)


def _experts(blk_e, n_used, xs, w_in, b_in, w_out, b_out):
    n_slots, D = xs.shape
    R = EXPERT_BLOCK
    E, _, F = w_in.shape
    return pl.pallas_call(
        _experts_kernel,
        grid_spec=pltpu.PrefetchScalarGridSpec(
            num_scalar_prefetch=2,
            grid=(n_slots // R,),
            in_specs=[pl.BlockSpec((R, D), lambda i, e, n: (jnp.minimum(i, n[0] - 1), 0)),
                      pl.BlockSpec((1, D, F), lambda i, e, n: (e[i], 0, 0)),
                      pl.BlockSpec((1, 1, F), lambda i, e, n: (e[i], 0, 0)),
                      pl.BlockSpec((1, F // 2, D), lambda i, e, n: (e[i], 0, 0)),
                      pl.BlockSpec((1, 1, D), lambda i, e, n: (e[i], 0, 0))],
            out_specs=pl.BlockSpec((R, D), lambda i, e, n: (i, 0))),
        out_shape=jax.ShapeDtypeStruct((n_slots, D), jnp.float32),
        compiler_params=_cparams("arbitrary"),
        name="experts",
    )(blk_e, n_used, xs, w_in, b_in, w_out, b_out)


def _split2(x):
    hi = x.astype(jnp.bfloat16)
    return hi, (x - hi.astype(jnp.float32)).astype(jnp.bfloat16)


def _combine_kernel(cnt_ref, base_ref, pos_ref, gate_ref, h_ref, g_ref, b_ref, ys_ref, o_ref, buf, sems):
    i = pl.program_id(0)
    last = pl.num_programs(0) - 1
    T = h_ref.shape[0]
    cur = i % 2

    def strip(buf_slot):
        def make(local_row, slot_row, size, bit):
            return pltpu.make_async_copy(ys_ref.at[pl.ds(slot_row, size), :],
                                         buf.at[buf_slot, pl.ds(local_row, size), :], sems.at[bit])
        return make

    @pl.when(i == 0)
    def _():
        buf[...] = jnp.zeros_like(buf)
        _for_each_strip(cnt_ref, base_ref, i, lambda *a: strip(cur)(*a).start())

    _for_each_strip(cnt_ref, base_ref, i, lambda *a: strip(cur)(*a).wait())

    @pl.when(i < last)
    def _():
        _for_each_strip(cnt_ref, base_ref, i + 1, lambda *a: strip(1 - cur)(*a).start())

    pos, gate = pos_ref[...], gate_ref[...]
    col = lax.broadcasted_iota(jnp.int32, (T, LOCAL_ROWS), 1)
    u = jnp.zeros(col.shape, jnp.float32)
    for k in range(TOP_K):
        u = jnp.where(col == pos[:, k:k + 1], gate[:, k:k + 1], u)
    u_hi, u_lo = _split2(u)
    y_hi, y_lo = _split2(buf[cur])
    dot = lambda a, b: jnp.dot(a, b, preferred_element_type=jnp.float32)
    ffn = dot(u_hi, y_hi) + (dot(u_lo, y_hi) + dot(u_hi, y_lo))
    o_ref[...] = _layer_norm(DEEPNORM_ALPHA * h_ref[...] + ffn, g_ref[...], b_ref[...])


def _combine(cnt_flat, base_flat, pos, gate, h1, g, b, ys):
    N, D = h1.shape
    T = ROW_TILE
    row = lambda i, c, s: (i, 0)
    const = lambda i, c, s: (0, 0)
    return pl.pallas_call(
        _combine_kernel,
        grid_spec=pltpu.PrefetchScalarGridSpec(
            num_scalar_prefetch=2,
            grid=(N // T,),
            in_specs=[pl.BlockSpec((T, LANES), row), pl.BlockSpec((T, LANES), row), pl.BlockSpec((T, D), row),
                      pl.BlockSpec(g.shape, const), pl.BlockSpec(b.shape, const),
                      pl.BlockSpec(memory_space=pl.ANY)],
            out_specs=pl.BlockSpec((T, D), row),
            scratch_shapes=[pltpu.VMEM((2, LOCAL_ROWS, D), jnp.float32),
                            pltpu.SemaphoreType.DMA((len(STRIP_SIZES),))]),
        out_shape=jax.ShapeDtypeStruct((N, D), jnp.float32),
        compiler_params=_cparams("arbitrary"),
        name="combine",
    )(cnt_flat, base_flat, pos, gate, h1, g, b, ys)


def _rope_tables(S):
    t = np.arange(S)
    row, col = (t // GRID_W).astype(np.float64), (t % GRID_W).astype(np.float64)

    def block(half):
        inv = ROPE_THETA ** (-np.arange(half, dtype=np.float64) / half)
        ar, ac = row[:, None] * inv[None, :], col[:, None] * inv[None, :]
        cos = np.concatenate([np.cos(ar), np.cos(ar), np.cos(ac), np.cos(ac)], axis=1)
        sin = np.concatenate([-np.sin(ar), np.sin(ar), -np.sin(ac), np.sin(ac)], axis=1)
        return cos, sin

    ca64, sa64 = block(GQA_HEAD_DIM // 4)
    ca, sa = np.tile(ca64, (1, 2)), np.tile(sa64, (1, 2))
    cb32, sb32 = block(MLA_ROPE_DIM // 4)
    ones, zeros = np.ones((S, MLA_NOPE_DIM)), np.zeros((S, MLA_NOPE_DIM))
    cb = np.concatenate([ones, cb32, ones[:, :LANES - MLA_NOPE_DIM - MLA_ROPE_DIM]], axis=1)
    sb = np.concatenate([zeros, sb32, zeros[:, :LANES - MLA_NOPE_DIM - MLA_ROPE_DIM]], axis=1)
    return tuple(jnp.asarray(a, jnp.float32) for a in (ca, sa, cb, sb))


def _layer(h_in_is_x, x2, mem, B, S, ln_g, ln_b, w_in_proj, b_gate, gqa_q_norm, gqa_k_norm, mla_q_norm,
           mla_kv_norm, w_mla_qb, w_mla_kvb, w_mem_kv, w_br_gqa, w_br_mla, w_br_mem, w_out,
           ln1_g, ln1_b, w_router, b_router, w_exp_in, b_exp_in, w_exp_out, b_exp_out, ln2_g, ln2_b):
    del h_in_is_x
    bf = jnp.bfloat16
    N, D = x2.shape
    row2 = lambda a: a.reshape(1, -1)

    W = w_in_proj
    zc = lambda n: jnp.zeros((D, n), W.dtype)
    wa = jnp.concatenate([W[:, :OFF_KROPE], zc(HALF), W[:, OFF_KROPE:OFF_QM],
                          zc(LANES - HALF - MLA_ROPE_DIM), W[:, OFF_QM:OFF_GATE]], axis=1).astype(bf)
    wg = W[:, OFF_GATE:].astype(bf)
    qd = MLA_NOPE_DIM + MLA_ROPE_DIM
    wqb = jnp.pad(w_mla_qb.reshape(MLA_Q_LORA, MLA_HEADS, qd),
                  ((0, 0), (0, 0), (0, LANES - qd))).reshape(MLA_Q_LORA, MLA_HEADS * LANES).astype(bf)
    kvb = w_mla_kvb.reshape(MLA_KV_LORA, MLA_HEADS, MLA_NOPE_DIM + MLA_V_DIM)
    wkvk = jnp.pad(kvb[:, :, :MLA_NOPE_DIM], ((0, 0), (0, 0), (0, LANES - MLA_NOPE_DIM))
                   ).reshape(MLA_KV_LORA, MLA_HEADS * LANES).astype(bf)
    wkvv = kvb[:, :, MLA_NOPE_DIM:].reshape(MLA_KV_LORA, MLA_HEADS * MLA_V_DIM).astype(bf)
    gq = row2(jnp.tile(gqa_q_norm, 2))
    gk = row2(jnp.tile(gqa_k_norm, 2))
    ca, sa, cb, sb = _rope_tables(S)

    mkt, mv = _mem_kv(mem, w_mem_kv.astype(bf))
    h, qat, ka, vat, qbt, kb, vbt, oc = _in_proj(
        x2, row2(ln_g), row2(ln_b), wa, gq, gk, row2(mla_q_norm), row2(mla_kv_norm),
        wqb, wkvk, wkvv, ca, sa, cb, sb, mkt, mv, B, S)

    oa = _attention(qat, ka.reshape(B, S, -1), vat, shared_kv=True)
    ob = _attention(qbt, kb.reshape(B, S, -1), vbt, shared_kv=False)

    wr = jnp.pad(w_router, ((0, 0), (0, LANES - N_EXPERTS)))
    wr3 = jnp.stack(_split3(wr))
    br = row2(jnp.pad(b_router, (0, LANES - N_EXPERTS)))
    h1, idx, gate = _mix_out(h, oa.reshape(N, -1), ob.reshape(N, -1), oc, wg, row2(b_gate),
                             w_br_gqa.astype(bf), w_br_mla.astype(bf), w_br_mem.astype(bf),
                             w_out.astype(bf), row2(ln1_g), row2(ln1_b), wr3, br)

    pos, carry_t, cnt_t, total = _route(idx)
    counts = total[0, :N_EXPERTS]
    R = EXPERT_BLOCK
    padded = (counts + R - 1) // R * R
    pad_end = jnp.cumsum(padded)
    pad_start = (pad_end - padded).astype(jnp.int32)
    n_tiles = N // ROW_TILE
    max_rows = N * TOP_K + n_tiles * N_EXPERTS * (SUBLANES - 1) + N_EXPERTS * (R - 1)
    n_blocks = -(-max_rows // R)
    blk_start = jnp.arange(n_blocks, dtype=jnp.int32) * R
    blk_e = jnp.sum((pad_end[None, :] <= blk_start[:, None]).astype(jnp.int32), axis=1)
    n_used = (pad_end[-1:] // R).astype(jnp.int32)
    last_e = jnp.max(jnp.where(blk_start < pad_end[-1], blk_e, 0))
    blk_e = jnp.minimum(blk_e, last_e).astype(jnp.int32)
    per_tile = lambda a: a.reshape(n_tiles, 8, LANES)[:, 0, :N_EXPERTS]
    cnt_flat = per_tile(cnt_t).reshape(-1)
    base_flat = (per_tile(carry_t) + pad_start[None, :]).reshape(-1)
    tail = jnp.concatenate([jnp.where(padded > 0, pad_end - R, -1), pad_end[-1:]]).astype(jnp.int32)

    xs = _dispatch(cnt_flat, base_flat, tail, pos, h1, n_blocks * R)
    ys = _experts(blk_e, n_used, xs, w_exp_in.astype(bf), b_exp_in[:, None, :],
                  w_exp_out.astype(bf), b_exp_out[:, None, :])
    return _combine(cnt_flat, base_flat, pos, gate, h1, row2(ln2_g), row2(ln2_b), ys)


def kernel(x, mem, ln_in_g, ln_in_b, w_in_proj, b_gate, gqa_q_norm, gqa_k_norm, mla_q_norm, mla_kv_norm, w_mla_qb, w_mla_kvb, w_mem_kv, w_br_gqa, w_br_mla, w_br_mem, w_out, ln1_g, ln1_b, w_router, b_router, w_exp_in, b_exp_in, w_exp_out, b_exp_out, ln2_g, ln2_b):
    B, S, D = x.shape
    depth = w_in_proj.shape[0]
    assert depth == 1, "the input LayerNorm is fused into the first (only) layer's projection kernel"
    out = _layer(True, x.reshape(B * S, D), mem, B, S, ln_in_g, ln_in_b, w_in_proj[0], b_gate[0],
                 gqa_q_norm[0], gqa_k_norm[0], mla_q_norm[0], mla_kv_norm[0], w_mla_qb[0], w_mla_kvb[0],
                 w_mem_kv[0], w_br_gqa[0], w_br_mla[0], w_br_mem[0], w_out[0], ln1_g[0], ln1_b[0],
                 w_router[0], b_router[0], w_exp_in[0], b_exp_in[0], w_exp_out[0], b_exp_out[0],
                 ln2_g[0], ln2_b[0])
    return out.reshape(B, S, D)
```

```python
import functools

import jax
import jax.numpy as jnp
import numpy as np
from jax import lax
from jax.experimental import pallas as pl
from jax.experimental.pallas import tpu as pltpu

D_MODEL = 1024
MEM_LEN = 256
GRID_W = 64
ROPE_THETA = 10000.0
RMS_EPS = 1e-6
LN_EPS = 1e-5

GQA_HEADS = 8
GQA_KV_HEADS = 2
GQA_HEAD_DIM = 64
MLA_HEADS = 8
MLA_NOPE_DIM = 64
MLA_ROPE_DIM = 32
MLA_V_DIM = 64
MLA_Q_LORA = 384
MLA_KV_LORA = 256
MEM_HEADS = 4
MEM_HEAD_DIM = 128

N_EXPERTS = 32
TOP_K = 4
SWIGLU_LIMIT = 7.0
SWIGLU_ALPHA = 1.702
DEEPNORM_ALPHA = 2.0 ** 0.25

LANES = 128
SUBLANES = 8
HALF = LANES // 2
NEG_INF = float("-inf")
LOG2_E = 1.4426950408889634

OFF_QA, OFF_KA, OFF_VA, OFF_QLAT, OFF_KVLAT, OFF_KROPE, OFF_QM, OFF_GATE = (
    0, 512, 640, 768, 1152, 1408, 1440, 1952)
IN_PROJ_W = 5024
A_QA, A_KA, A_VA, A_QLAT, A_KVLAT, A_KROPE, A_QM, A_END = 0, 512, 640, 768, 1152, 1408, 1536, 2048

ROW_TILE = 256
ATTN_Q_TILE = 256
EXPERT_BLOCK = 256
VMEM_LIMIT = 48 * 1024 * 1024
EXPERTS_VMEM_LIMIT = 56 * 1024 * 1024


def _cparams(*sem):
    return pltpu.CompilerParams(dimension_semantics=sem, vmem_limit_bytes=VMEM_LIMIT)


def _lane_iota(shape):
    return lax.broadcasted_iota(jnp.int32, shape, len(shape) - 1)


def _layer_norm(x, g, b):
    mu = jnp.mean(x, axis=-1, keepdims=True)
    xc = x - mu
    var = jnp.mean(xc * xc, axis=-1, keepdims=True)
    return xc * lax.rsqrt(var + LN_EPS) * g + b


def _bdot(a, b):
    return jnp.dot(a.astype(jnp.bfloat16), b.astype(jnp.bfloat16), preferred_element_type=jnp.float32)


def _rope(x, cos, sin_signed, pair):
    lane = _lane_iota(x.shape)
    fwd = pltpu.roll(x, LANES - pair, 1)
    bwd = pltpu.roll(x, pair, 1)
    partner = jnp.where((lane % (2 * pair)) < pair, fwd, bwd)
    return x * cos + partner * sin_signed


def _half_rms_scale(x):
    lane = _lane_iota(x.shape)
    sq = x * x
    lo = jnp.sum(jnp.where(lane < HALF, sq, 0.0), axis=-1, keepdims=True)
    hi = jnp.sum(jnp.where(lane >= HALF, sq, 0.0), axis=-1, keepdims=True)
    inv = 1.0 / GQA_HEAD_DIM
    return jnp.where(lane < HALF, lax.rsqrt(lo * inv + RMS_EPS), lax.rsqrt(hi * inv + RMS_EPS))


def _softmax_rows(s):
    m = jnp.max(s, axis=-1, keepdims=True)
    p = jnp.exp(s - m)
    return p, jnp.sum(p, axis=-1, keepdims=True)


def _mem_kv_kernel(mem_ref, w_ref, kt_ref, v_ref):
    kv = _bdot(mem_ref[0], w_ref[...])
    width = MEM_HEADS * MEM_HEAD_DIM
    kt_ref[0] = kv[:, :width].T.astype(jnp.bfloat16)
    v_ref[0] = kv[:, width:].astype(jnp.bfloat16)


def _mem_kv(mem, w_mem_kv):
    B, M, D = mem.shape
    width = MEM_HEADS * MEM_HEAD_DIM
    return pl.pallas_call(
        _mem_kv_kernel,
        grid=(B,),
        in_specs=[pl.BlockSpec((1, M, D), lambda b: (b, 0, 0)),
                  pl.BlockSpec((D, 2 * width), lambda b: (0, 0))],
        out_specs=[pl.BlockSpec((1, width, M), lambda b: (b, 0, 0)),
                   pl.BlockSpec((1, M, width), lambda b: (b, 0, 0))],
        out_shape=[jax.ShapeDtypeStruct((B, width, M), jnp.bfloat16),
                   jax.ShapeDtypeStruct((B, M, width), jnp.bfloat16)],
        compiler_params=_cparams("arbitrary"),
        name="mem_kv",
    )(mem, w_mem_kv)


def _in_proj_kernel(x_ref, lng_ref, lnb_ref, wa_ref, gq_ref, gk_ref, gql_ref, gkvl_ref,
                    wqb_ref, wkvk_ref, wkvv_ref, ca_ref, sa_ref, cb_ref, sb_ref, mkt_ref, mv_ref,
                    h_ref, qa_ref, ka_ref, va_ref, qb_ref, kb_ref, vb_ref, oc_ref):
    h = _layer_norm(x_ref[...], lng_ref[...], lnb_ref[...])
    h_ref[...] = h
    proj = _bdot(h, wa_ref[...])
    lane = _lane_iota((h.shape[0], LANES))
    lo_mask = lane < HALF
    ca, sa, cb, sb = ca_ref[...], sa_ref[...], cb_ref[...], sb_ref[...]

    q_scale = GQA_HEAD_DIM ** -0.5 * LOG2_E
    per_group = GQA_HEADS // GQA_KV_HEADS
    for c in range(GQA_HEADS // 2):
        slab = proj[:, A_QA + c * LANES:A_QA + (c + 1) * LANES]
        slab = _rope(slab * _half_rms_scale(slab) * gq_ref[...], ca, sa, 16) * q_scale
        swapped = pltpu.roll(slab, HALF, 1)
        for half in range(2):
            head = 2 * c + half
            group_lo = (head // per_group) == 0
            src = slab if (half == 0) == group_lo else swapped
            keep = lo_mask if group_lo else jnp.logical_not(lo_mask)
            qa_ref[0, head * LANES:(head + 1) * LANES, :] = jnp.where(keep, src, 0.0).astype(jnp.bfloat16).T

    ka = proj[:, A_KA:A_KA + LANES]
    ka_ref[...] = _rope(ka * _half_rms_scale(ka) * gk_ref[...], ca, sa, 16).astype(jnp.bfloat16)
    va_ref[0] = proj[:, A_VA:A_VA + LANES].astype(jnp.bfloat16).T

    ql = proj[:, A_QLAT:A_KVLAT]
    ql = ql * lax.rsqrt(jnp.mean(ql * ql, axis=-1, keepdims=True) + RMS_EPS) * gql_ref[...]
    qm = _bdot(ql, wqb_ref[...])
    kvl = proj[:, A_KVLAT:A_KROPE]
    kvl = kvl * lax.rsqrt(jnp.mean(kvl * kvl, axis=-1, keepdims=True) + RMS_EPS) * gkvl_ref[...]
    kn = _bdot(kvl, wkvk_ref[...])
    vb = _bdot(kvl, wkvv_ref[...])
    for c in range(vb.shape[1] // LANES):
        sl = slice(c * LANES, (c + 1) * LANES)
        vb_ref[0, sl, :] = vb[:, sl].astype(jnp.bfloat16).T
    k_pe = _rope(proj[:, A_KROPE:A_QM], cb, sb, 8)
    qb_scale = (MLA_NOPE_DIM + MLA_ROPE_DIM) ** -0.5 * LOG2_E
    for hd in range(MLA_HEADS):
        sl = slice(hd * LANES, (hd + 1) * LANES)
        qb_ref[0, sl, :] = (_rope(qm[:, sl], cb, sb, 8) * qb_scale).astype(jnp.bfloat16).T
        kb_ref[:, sl] = (kn[:, sl] + k_pe).astype(jnp.bfloat16)

    qc = proj[:, A_QM:A_END] * (MEM_HEAD_DIM ** -0.5)
    for hd in range(MEM_HEADS):
        sl = slice(hd * MEM_HEAD_DIM, (hd + 1) * MEM_HEAD_DIM)
        p, l = _softmax_rows(_bdot(qc[:, sl], mkt_ref[0, sl, :]))
        oc_ref[:, sl] = (_bdot(p, mv_ref[0, :, sl]) / l).astype(jnp.bfloat16)


def _in_proj(x2, ln_g, ln_b, wa, gq, gk, gql, gkvl, wqb, wkvk, wkvv, ca, sa, cb, sb, mkt, mv, B, S):
    N, D = x2.shape
    T = ROW_TILE
    tiles_per_seq = S // T
    row = lambda i: (i, 0)
    const = lambda i: (0, 0)
    pos = lambda i: (i % tiles_per_seq, 0)
    batch3 = lambda i: (i // tiles_per_seq, 0, 0)
    kt_map = lambda i: (i // tiles_per_seq, 0, i % tiles_per_seq)
    full = lambda a: pl.BlockSpec(a.shape, const)
    return pl.pallas_call(
        _in_proj_kernel,
        grid=(N // T,),
        in_specs=[pl.BlockSpec((T, D), row), full(ln_g), full(ln_b), full(wa), full(gq), full(gk),
                  full(gql), full(gkvl), full(wqb), full(wkvk), full(wkvv),
                  pl.BlockSpec((T, LANES), pos), pl.BlockSpec((T, LANES), pos),
                  pl.BlockSpec((T, LANES), pos), pl.BlockSpec((T, LANES), pos),
                  pl.BlockSpec((1,) + mkt.shape[1:], batch3), pl.BlockSpec((1,) + mv.shape[1:], batch3)],
        out_specs=[pl.BlockSpec((T, D), row),
                   pl.BlockSpec((1, GQA_HEADS * LANES, T), kt_map),
                   pl.BlockSpec((T, LANES), row),
                   pl.BlockSpec((1, LANES, T), kt_map),
                   pl.BlockSpec((1, MLA_HEADS * LANES, T), kt_map),
                   pl.BlockSpec((T, MLA_HEADS * LANES), row),
                   pl.BlockSpec((1, MLA_HEADS * MLA_V_DIM, T), kt_map),
                   pl.BlockSpec((T, MEM_HEADS * MEM_HEAD_DIM), row)],
        out_shape=[jax.ShapeDtypeStruct((N, D), jnp.float32),
                   jax.ShapeDtypeStruct((B, GQA_HEADS * LANES, S), jnp.bfloat16),
                   jax.ShapeDtypeStruct((N, LANES), jnp.bfloat16),
                   jax.ShapeDtypeStruct((B, LANES, S), jnp.bfloat16),
                   jax.ShapeDtypeStruct((B, MLA_HEADS * LANES, S), jnp.bfloat16),
                   jax.ShapeDtypeStruct((N, MLA_HEADS * LANES), jnp.bfloat16),
                   jax.ShapeDtypeStruct((B, MLA_HEADS * MLA_V_DIM, S), jnp.bfloat16),
                   jax.ShapeDtypeStruct((N, MEM_HEADS * MEM_HEAD_DIM), jnp.bfloat16)],
        compiler_params=_cparams("arbitrary"),
        name="in_proj",
    )(x2, ln_g, ln_b, wa, gq, gk, gql, gkvl, wqb, wkvk, wkvv, ca, sa, cb, sb, mkt, mv)


def _attention_kernel(qt_ref, k_ref, vt_ref, o_ref, s_even, s_odd, *, shared_kv):
    t = pl.program_id(0)

    @pl.when(t == 0)
    def _():
        s_odd[...] = jnp.zeros_like(s_odd)

    def step(s_new, s_old):
        for hd in range(2):
            qt = qt_ref[0, hd * LANES:(hd + 1) * LANES, :]
            k = k_ref[0] if shared_kv else k_ref[0, :, hd * LANES:(hd + 1) * LANES]
            s_new[hd] = jnp.dot(k, qt, preferred_element_type=jnp.float32)
        outs = []
        for hd in range(2):
            vt = vt_ref[0] if shared_kv else vt_ref[0, hd * HALF:(hd + 1) * HALF, :]
            st = s_old[hd]
            p = jnp.exp2(st - jnp.max(st, axis=0, keepdims=True))
            l = jnp.sum(p, axis=0, keepdims=True)
            outs.append(jnp.dot(vt, p.astype(jnp.bfloat16), preferred_element_type=jnp.float32) / l)
        o_ref[0] = jnp.concatenate(outs, axis=0).T.astype(jnp.bfloat16)

    @pl.when(t % 2 == 0)
    def _():
        step(s_even, s_odd)

    @pl.when(t % 2 == 1)
    def _():
        step(s_odd, s_even)


def _attention(qt, k, vt, *, shared_kv):
    B, qh, S = qt.shape
    pairs = qh // (2 * LANES)
    TQ = ATTN_Q_TILE
    nq = S // TQ
    items = B * pairs * nq
    pairs_per_group = pairs // GQA_KV_HEADS

    def item(t):
        return t // (pairs * nq), (t // nq) % pairs, t % nq

    score_item = lambda t: item(jnp.minimum(t, items - 1))
    finish_item = lambda t: item(jnp.maximum(t - 1, 0))

    def qt_map(t):
        b, j, i = score_item(t)
        return b, j, i

    def k_map(t):
        b, j, _ = score_item(t)
        return (b, 0, 0) if shared_kv else (b, 0, j)

    def vt_map(t):
        b, j, _ = finish_item(t)
        return (b, j // pairs_per_group, 0) if shared_kv else (b, j, 0)

    def o_map(t):
        b, j, i = finish_item(t)
        return b, i, j

    k_spec = pl.BlockSpec((1, S, LANES if shared_kv else 2 * LANES), k_map)
    vt_spec = pl.BlockSpec((1, HALF if shared_kv else LANES, S), vt_map)
    return pl.pallas_call(
        functools.partial(_attention_kernel, shared_kv=shared_kv),
        grid=(items + 1,),
        in_specs=[pl.BlockSpec((1, 2 * LANES, TQ), qt_map), k_spec, vt_spec],
        out_specs=pl.BlockSpec((1, TQ, LANES), o_map),
        out_shape=jax.ShapeDtypeStruct((B, S, pairs * LANES), jnp.bfloat16),
        scratch_shapes=[pltpu.VMEM((2, S, TQ), jnp.float32), pltpu.VMEM((2, S, TQ), jnp.float32)],
        compiler_params=_cparams("arbitrary"),
        name="attention_gqa" if shared_kv else "attention_mla",
    )(qt, k, vt)


def _split2(x):
    hi = x.astype(jnp.bfloat16)
    return hi, (x - hi.astype(jnp.float32)).astype(jnp.bfloat16)


def _mix_out_kernel(h_ref, oa_ref, ob_ref, oc_ref, wg_ref, bg_ref, wa_ref, wb_ref, wc_ref, wo_ref,
                    g_ref, b_ref, wr_ref, br_ref, h1_ref, idx_ref, gate_ref):
    h = h_ref[...]
    gates = jax.nn.sigmoid(_bdot(h, wg_ref[...]) + bg_ref[...])
    D = D_MODEL
    merged = (gates[:, :D] * _bdot(oa_ref[...], wa_ref[...])
              + gates[:, D:2 * D] * _bdot(ob_ref[...], wb_ref[...])
              + gates[:, 2 * D:] * _bdot(oc_ref[...], wc_ref[...]))
    mix = _bdot(merged, wo_ref[...])
    h1 = _layer_norm(DEEPNORM_ALPHA * h + mix, g_ref[...], b_ref[...])
    h1_ref[...] = h1

    h_hi, h_lo = _split2(h1)
    w_hi, w_lo = wr_ref[0], wr_ref[1]
    dot = lambda a, b: jnp.dot(a, b, preferred_element_type=jnp.float32)
    logits = dot(h_hi, w_hi) + (dot(h_hi, w_lo) + dot(h_lo, w_hi)) + br_ref[...]
    lane = _lane_iota(logits.shape)
    logits = jnp.where(lane < N_EXPERTS, logits, NEG_INF)

    idx_out = jnp.zeros(logits.shape, jnp.int32)
    val_out = jnp.zeros(logits.shape, jnp.float32)
    top = None
    for k in range(TOP_K):
        m = jnp.max(logits, axis=-1, keepdims=True)
        idx = jnp.min(jnp.where(logits == m, lane, LANES), axis=-1, keepdims=True)
        logits = jnp.where(lane == idx, NEG_INF, logits)
        top = m if top is None else top
        idx_out = jnp.where(lane == k, idx, idx_out)
        val_out = jnp.where(lane == k, jnp.exp(m - top), val_out)
    idx_ref[...] = idx_out
    gate_ref[...] = val_out / jnp.sum(val_out, axis=-1, keepdims=True)


def _mix_out(h, oa, ob, oc, wg, bg, wa, wb, wc, wo, g, b, wr3, br):
    N, D = h.shape
    T = ROW_TILE
    row = lambda i: (i, 0)
    full = lambda a: pl.BlockSpec(a.shape, lambda i: (0,) * a.ndim)
    return pl.pallas_call(
        _mix_out_kernel,
        grid=(N // T,),
        in_specs=[pl.BlockSpec((T, D), row), pl.BlockSpec((T, oa.shape[1]), row),
                  pl.BlockSpec((T, ob.shape[1]), row), pl.BlockSpec((T, oc.shape[1]), row),
                  full(wg), full(bg), full(wa), full(wb), full(wc), full(wo), full(g), full(b),
                  full(wr3), full(br)],
        out_specs=[pl.BlockSpec((T, D), row), pl.BlockSpec((T, LANES), row), pl.BlockSpec((T, LANES), row)],
        out_shape=[jax.ShapeDtypeStruct((N, D), jnp.float32),
                   jax.ShapeDtypeStruct((N, LANES), jnp.int32),
                   jax.ShapeDtypeStruct((N, LANES), jnp.float32)],
        compiler_params=_cparams("arbitrary"),
        name="mix_out",
    )(h, oa, ob, oc, wg, bg, wa, wb, wc, wo, g, b, wr3, br)


def _route_kernel(idx_ref, pos_ref, carry_out_ref, cnt_out_ref, total_ref, carry_ref):
    @pl.when(pl.program_id(0) == 0)
    def _():
        carry_ref[...] = jnp.zeros_like(carry_ref)

    idx = idx_ref[...]
    T = idx.shape[0]
    lane = _lane_iota(idx.shape)
    sel = [lane == idx[:, k:k + 1] for k in range(TOP_K)]
    onehot = sum(s.astype(jnp.float32) for s in sel)
    r = lax.broadcasted_iota(jnp.int32, (T, T), 0)
    c = lax.broadcasted_iota(jnp.int32, (T, T), 1)
    before = (c < r).astype(jnp.bfloat16)
    prefix = jnp.dot(before, onehot.astype(jnp.bfloat16), preferred_element_type=jnp.float32)
    cnt = jnp.broadcast_to(jnp.sum(onehot, axis=0, keepdims=True), carry_ref.shape)
    cnt = jnp.floor((cnt + (SUBLANES - 1)) * (1.0 / SUBLANES)) * SUBLANES
    er = lax.broadcasted_iota(jnp.int32, (LANES, LANES), 0)
    ec = lax.broadcasted_iota(jnp.int32, (LANES, LANES), 1)
    local_start = jnp.dot(cnt.astype(jnp.bfloat16), (er < ec).astype(jnp.bfloat16),
                          preferred_element_type=jnp.float32)
    target = prefix + local_start[0:1, :]
    out = jnp.zeros(idx.shape, jnp.int32)
    for k in range(TOP_K):
        pk = jnp.sum(jnp.where(sel[k], target, 0.0), axis=-1, keepdims=True)
        out = jnp.where(lane == k, pk.astype(jnp.int32), out)
    pos_ref[...] = out
    carry_out_ref[...] = carry_ref[...].astype(jnp.int32)
    cnt_out_ref[...] = cnt.astype(jnp.int32)
    carry_ref[...] = carry_ref[...] + cnt
    total_ref[...] = carry_ref[...].astype(jnp.int32)


def _route(idx):
    N = idx.shape[0]
    T = ROW_TILE
    n_tiles = N // T
    tile8 = pl.BlockSpec((8, LANES), lambda i: (i, 0))
    return pl.pallas_call(
        _route_kernel,
        grid=(n_tiles,),
        in_specs=[pl.BlockSpec((T, LANES), lambda i: (i, 0))],
        out_specs=[pl.BlockSpec((T, LANES), lambda i: (i, 0)), tile8, tile8,
                   pl.BlockSpec((8, LANES), lambda i: (0, 0))],
        out_shape=[jax.ShapeDtypeStruct((N, LANES), jnp.int32),
                   jax.ShapeDtypeStruct((n_tiles * 8, LANES), jnp.int32),
                   jax.ShapeDtypeStruct((n_tiles * 8, LANES), jnp.int32),
                   jax.ShapeDtypeStruct((8, LANES), jnp.int32)],
        scratch_shapes=[pltpu.VMEM((8, LANES), jnp.float32)],
        compiler_params=_cparams("arbitrary"),
        name="route",
    )(idx)


STRIP_SIZES = tuple(ROW_TILE >> s for s in range((ROW_TILE // SUBLANES).bit_length()))
LOCAL_ROWS = TOP_K * ROW_TILE + ROW_TILE
assert LOCAL_ROWS >= TOP_K * ROW_TILE + N_EXPERTS * (SUBLANES - 1)


HI16 = 0xFFFF0000


def _pack_rows(x):
    half = x.shape[1] // 2
    lo = lax.bitcast_convert_type(x[:, :half], jnp.uint32) >> 16
    hi = lax.bitcast_convert_type(x[:, half:], jnp.uint32) & jnp.uint32(HI16)
    return lo | hi


def _unpack_rows(w):
    lo = lax.bitcast_convert_type(w << 16, jnp.float32)
    hi = lax.bitcast_convert_type(w & jnp.uint32(HI16), jnp.float32)
    return jnp.concatenate([lo, hi], axis=1).astype(jnp.bfloat16)


def _round_bf16(x):
    return x.astype(jnp.bfloat16).astype(jnp.float32)


def _for_each_strip(cnt_ref, base_ref, tile, fn):
    def body(e, off):
        c = cnt_ref[tile * N_EXPERTS + e]
        d = base_ref[tile * N_EXPERTS + e]
        for bit, size in enumerate(STRIP_SIZES):
            above = c & (-2 * size)

            @pl.when((c & size) != 0)
            def _():
                fn(pl.multiple_of(off + above, SUBLANES), pl.multiple_of(d + above, SUBLANES), size, bit)
        return off + c

    lax.fori_loop(0, N_EXPERTS, body, 0)


def _dispatch_kernel(cnt_ref, base_ref, tail_ref, pos_ref, h_ref, xs_ref, buf, zeros, sems, zsem):
    i = pl.program_id(0)
    last = pl.num_programs(0) - 1
    T = h_ref.shape[0]
    cur = i % 2

    @pl.when(i == 0)
    def _():
        zeros[...] = jnp.zeros_like(zeros)

        def tail_copy(e):
            row = pl.multiple_of(jnp.maximum(tail_ref[e], 0), SUBLANES)
            return pltpu.make_async_copy(zeros, xs_ref.at[pl.ds(row, EXPERT_BLOCK), :], zsem)

        def spare_copy(blk):
            row = pl.multiple_of(blk * EXPERT_BLOCK, EXPERT_BLOCK)
            return pltpu.make_async_copy(zeros, xs_ref.at[pl.ds(row, EXPERT_BLOCK), :], zsem)

        first_spare = tail_ref[N_EXPERTS] // EXPERT_BLOCK
        for act in ("start", "wait"):
            def body(e, carry, act=act):
                @pl.when(tail_ref[e] >= 0)
                def _():
                    getattr(tail_copy(e), act)()
                return carry
            lax.fori_loop(0, N_EXPERTS, body, 0)

            def spare(blk, carry, act=act):
                getattr(spare_copy(blk), act)()
                return carry
            lax.fori_loop(first_spare, xs_ref.shape[0] // EXPERT_BLOCK, spare, 0)

    pos_t = pos_ref[...].astype(jnp.float32).T.astype(jnp.int32)
    rows = lax.broadcasted_iota(jnp.int32, (LOCAL_ROWS, T), 0)
    hit = rows == pos_t[0:1, :]
    for k in range(1, TOP_K):
        hit = jnp.logical_or(hit, rows == pos_t[k:k + 1, :])
    buf[cur] = _pack_rows(jnp.dot(hit.astype(jnp.bfloat16), h_ref[...].astype(jnp.bfloat16),
                                  preferred_element_type=jnp.float32))

    def strip(buf_slot):
        def make(local_row, slot_row, size, bit):
            return pltpu.make_async_copy(buf.at[buf_slot, pl.ds(local_row, size), :],
                                         xs_ref.at[pl.ds(slot_row, size), :], sems.at[bit])
        return make

    @pl.when(i > 0)
    def _():
        _for_each_strip(cnt_ref, base_ref, i - 1, lambda *a: strip(1 - cur)(*a).wait())

    _for_each_strip(cnt_ref, base_ref, i, lambda *a: strip(cur)(*a).start())

    @pl.when(i == last)
    def _():
        _for_each_strip(cnt_ref, base_ref, i, lambda *a: strip(cur)(*a).wait())


def _dispatch(cnt_flat, base_flat, tail, pos, h1, n_slots):
    N, D = h1.shape
    T = ROW_TILE
    return pl.pallas_call(
        _dispatch_kernel,
        grid_spec=pltpu.PrefetchScalarGridSpec(
            num_scalar_prefetch=3,
            grid=(N // T,),
            in_specs=[pl.BlockSpec((T, LANES), lambda i, c, b, t: (i, 0)),
                      pl.BlockSpec((T, D), lambda i, c, b, t: (i, 0))],
            out_specs=pl.BlockSpec(memory_space=pl.ANY),
            scratch_shapes=[pltpu.VMEM((2, LOCAL_ROWS, D // 2), jnp.uint32),
                            pltpu.VMEM((EXPERT_BLOCK, D // 2), jnp.uint32),
                            pltpu.SemaphoreType.DMA((len(STRIP_SIZES),)),
                            pltpu.SemaphoreType.DMA(())]),
        out_shape=jax.ShapeDtypeStruct((n_slots, D // 2), jnp.uint32),
        compiler_params=_cparams("arbitrary"),
        name="dispatch",
    )(cnt_flat, base_flat, tail, pos, h1)


def _experts_kernel(blk_e_ref, n_used_ref, x_ref, wi_ref, bi_ref, wo_ref, bo_ref, y_ref, wi_bf, wo_bf):
    i = pl.program_id(0)
    used = i < n_used_ref[0]
    new_expert = jnp.logical_or(i == 0, blk_e_ref[i] != blk_e_ref[jnp.maximum(i - 1, 0)])

    @pl.when(jnp.logical_and(used, new_expert))
    def _():
        wi_bf[...] = wi_ref[0].astype(jnp.bfloat16)
        wo_bf[...] = wo_ref[0].astype(jnp.bfloat16)

    @pl.when(used)
    def _():
        hb = jnp.dot(_unpack_rows(x_ref[...]), wi_bf[...], preferred_element_type=jnp.float32) + bi_ref[0]
        De = D_MODEL
        x_glu = jnp.minimum(hb[:, :De], SWIGLU_LIMIT)
        x_lin = jnp.clip(hb[:, De:], -SWIGLU_LIMIT, SWIGLU_LIMIT)
        act = x_glu * jax.nn.sigmoid(SWIGLU_ALPHA * x_glu) * (x_lin + 1.0)
        y = jnp.dot(act.astype(jnp.bfloat16), wo_bf[...], preferred_element_type=jnp.float32) + bo_ref[0]
        y_ref[...] = _pack_rows(_round_bf16(y))

    @pl.when(jnp.logical_not(used))
    def _():
        y_ref[...] = jnp.zeros_like(y_ref)


def _experts(blk_e, n_used, xs, w_in, b_in, w_out, b_out):
    n_slots, half = xs.shape
    R = EXPERT_BLOCK
    E, D, F = w_in.shape
    return pl.pallas_call(
        _experts_kernel,
        grid_spec=pltpu.PrefetchScalarGridSpec(
            num_scalar_prefetch=2,
            grid=(n_slots // R,),
            in_specs=[pl.BlockSpec((R, half), lambda i, e, n: (jnp.minimum(i, n[0] - 1), 0)),
                      pl.BlockSpec((1, D, F), lambda i, e, n: (e[i], 0, 0)),
                      pl.BlockSpec((1, 1, F), lambda i, e, n: (e[i], 0, 0)),
                      pl.BlockSpec((1, F // 2, D), lambda i, e, n: (e[i], 0, 0)),
                      pl.BlockSpec((1, 1, D), lambda i, e, n: (e[i], 0, 0))],
            out_specs=pl.BlockSpec((R, half), lambda i, e, n: (i, 0)),
            scratch_shapes=[pltpu.VMEM((D, F), jnp.bfloat16), pltpu.VMEM((F // 2, D), jnp.bfloat16)]),
        out_shape=jax.ShapeDtypeStruct((n_slots, half), jnp.uint32),
        compiler_params=pltpu.CompilerParams(dimension_semantics=("arbitrary",),
                                             vmem_limit_bytes=EXPERTS_VMEM_LIMIT),
        name="experts",
    )(blk_e, n_used, xs, w_in, b_in, w_out, b_out)


def _combine_kernel(cnt_ref, base_ref, pos_ref, gate_ref, h_ref, g_ref, b_ref, ys_ref, o_ref, buf, sems):
    i = pl.program_id(0)
    last = pl.num_programs(0) - 1
    T = h_ref.shape[0]
    cur = i % 2

    def strip(buf_slot):
        def make(local_row, slot_row, size, bit):
            return pltpu.make_async_copy(ys_ref.at[pl.ds(slot_row, size), :],
                                         buf.at[buf_slot, pl.ds(local_row, size), :], sems.at[bit])
        return make

    @pl.when(i == 0)
    def _():
        buf[...] = jnp.zeros_like(buf)
        _for_each_strip(cnt_ref, base_ref, i, lambda *a: strip(cur)(*a).start())

    _for_each_strip(cnt_ref, base_ref, i, lambda *a: strip(cur)(*a).wait())

    @pl.when(i < last)
    def _():
        _for_each_strip(cnt_ref, base_ref, i + 1, lambda *a: strip(1 - cur)(*a).start())

    pos, gate = pos_ref[...], gate_ref[...]
    col = lax.broadcasted_iota(jnp.int32, (T, LOCAL_ROWS), 1)
    u = jnp.zeros(col.shape, jnp.float32)
    for k in range(TOP_K):
        u = jnp.where(col == pos[:, k:k + 1], gate[:, k:k + 1], u)
    u_hi, u_lo = _split2(u)
    y = _unpack_rows(buf[cur])
    dot = lambda a, b: jnp.dot(a, b, preferred_element_type=jnp.float32)
    ffn = dot(u_hi, y) + dot(u_lo, y)
    o_ref[...] = _layer_norm(DEEPNORM_ALPHA * h_ref[...] + ffn, g_ref[...], b_ref[...])


def _combine(cnt_flat, base_flat, pos, gate, h1, g, b, ys):
    N, D = h1.shape
    T = ROW_TILE
    row = lambda i, c, s: (i, 0)
    const = lambda i, c, s: (0, 0)
    return pl.pallas_call(
        _combine_kernel,
        grid_spec=pltpu.PrefetchScalarGridSpec(
            num_scalar_prefetch=2,
            grid=(N // T,),
            in_specs=[pl.BlockSpec((T, LANES), row), pl.BlockSpec((T, LANES), row), pl.BlockSpec((T, D), row),
                      pl.BlockSpec(g.shape, const), pl.BlockSpec(b.shape, const),
                      pl.BlockSpec(memory_space=pl.ANY)],
            out_specs=pl.BlockSpec((T, D), row),
            scratch_shapes=[pltpu.VMEM((2, LOCAL_ROWS, D // 2), jnp.uint32),
                            pltpu.SemaphoreType.DMA((len(STRIP_SIZES),))]),
        out_shape=jax.ShapeDtypeStruct((N, D), jnp.float32),
        compiler_params=_cparams("arbitrary"),
        name="combine",
    )(cnt_flat, base_flat, pos, gate, h1, g, b, ys)


def _rope_tables(S):
    t = np.arange(S)
    row, col = (t // GRID_W).astype(np.float64), (t % GRID_W).astype(np.float64)

    def block(half):
        inv = ROPE_THETA ** (-np.arange(half, dtype=np.float64) / half)
        ar, ac = row[:, None] * inv[None, :], col[:, None] * inv[None, :]
        cos = np.concatenate([np.cos(ar), np.cos(ar), np.cos(ac), np.cos(ac)], axis=1)
        sin = np.concatenate([-np.sin(ar), np.sin(ar), -np.sin(ac), np.sin(ac)], axis=1)
        return cos, sin

    ca64, sa64 = block(GQA_HEAD_DIM // 4)
    ca, sa = np.tile(ca64, (1, 2)), np.tile(sa64, (1, 2))
    cb32, sb32 = block(MLA_ROPE_DIM // 4)
    ones, zeros = np.ones((S, MLA_NOPE_DIM)), np.zeros((S, MLA_NOPE_DIM))
    cb = np.concatenate([ones, cb32, ones[:, :LANES - MLA_NOPE_DIM - MLA_ROPE_DIM]], axis=1)
    sb = np.concatenate([zeros, sb32, zeros[:, :LANES - MLA_NOPE_DIM - MLA_ROPE_DIM]], axis=1)
    return tuple(jnp.asarray(a, jnp.float32) for a in (ca, sa, cb, sb))


def _layer(h_in_is_x, x2, mem, B, S, ln_g, ln_b, w_in_proj, b_gate, gqa_q_norm, gqa_k_norm, mla_q_norm,
           mla_kv_norm, w_mla_qb, w_mla_kvb, w_mem_kv, w_br_gqa, w_br_mla, w_br_mem, w_out,
           ln1_g, ln1_b, w_router, b_router, w_exp_in, b_exp_in, w_exp_out, b_exp_out, ln2_g, ln2_b):
    del h_in_is_x
    bf = jnp.bfloat16
    N, D = x2.shape
    row2 = lambda a: a.reshape(1, -1)

    W = w_in_proj
    zc = lambda n: jnp.zeros((D, n), W.dtype)
    wa = jnp.concatenate([W[:, :OFF_KROPE], zc(HALF), W[:, OFF_KROPE:OFF_QM],
                          zc(LANES - HALF - MLA_ROPE_DIM), W[:, OFF_QM:OFF_GATE]], axis=1).astype(bf)
    wg = W[:, OFF_GATE:].astype(bf)
    qd = MLA_NOPE_DIM + MLA_ROPE_DIM
    wqb = jnp.pad(w_mla_qb.reshape(MLA_Q_LORA, MLA_HEADS, qd),
                  ((0, 0), (0, 0), (0, LANES - qd))).reshape(MLA_Q_LORA, MLA_HEADS * LANES).astype(bf)
    kvb = w_mla_kvb.reshape(MLA_KV_LORA, MLA_HEADS, MLA_NOPE_DIM + MLA_V_DIM)
    wkvk = jnp.pad(kvb[:, :, :MLA_NOPE_DIM], ((0, 0), (0, 0), (0, LANES - MLA_NOPE_DIM))
                   ).reshape(MLA_KV_LORA, MLA_HEADS * LANES).astype(bf)
    wkvv = kvb[:, :, MLA_NOPE_DIM:].reshape(MLA_KV_LORA, MLA_HEADS * MLA_V_DIM).astype(bf)
    gq = row2(jnp.tile(gqa_q_norm, 2))
    gk = row2(jnp.tile(gqa_k_norm, 2))
    ca, sa, cb, sb = _rope_tables(S)

    mkt, mv = _mem_kv(mem, w_mem_kv.astype(bf))
    h, qat, ka, vat, qbt, kb, vbt, oc = _in_proj(
        x2, row2(ln_g), row2(ln_b), wa, gq, gk, row2(mla_q_norm), row2(mla_kv_norm),
        wqb, wkvk, wkvv, ca, sa, cb, sb, mkt, mv, B, S)

    oa = _attention(qat, ka.reshape(B, S, -1), vat, shared_kv=True)
    ob = _attention(qbt, kb.reshape(B, S, -1), vbt, shared_kv=False)

    wr = jnp.pad(w_router, ((0, 0), (0, LANES - N_EXPERTS)))
    wr3 = jnp.stack(_split2(wr))
    br = row2(jnp.pad(b_router, (0, LANES - N_EXPERTS)))
    h1, idx, gate = _mix_out(h, oa.reshape(N, -1), ob.reshape(N, -1), oc, wg, row2(b_gate),
                             w_br_gqa.astype(bf), w_br_mla.astype(bf), w_br_mem.astype(bf),
                             w_out.astype(bf), row2(ln1_g), row2(ln1_b), wr3, br)

    pos, carry_t, cnt_t, total = _route(idx)
    counts = total[0, :N_EXPERTS]
    R = EXPERT_BLOCK
    padded = (counts + R - 1) // R * R
    pad_end = jnp.cumsum(padded)
    pad_start = (pad_end - padded).astype(jnp.int32)
    n_tiles = N // ROW_TILE
    max_rows = N * TOP_K + n_tiles * N_EXPERTS * (SUBLANES - 1) + N_EXPERTS * (R - 1)
    n_blocks = -(-max_rows // R)
    blk_start = jnp.arange(n_blocks, dtype=jnp.int32) * R
    blk_e = jnp.sum((pad_end[None, :] <= blk_start[:, None]).astype(jnp.int32), axis=1)
    n_used = (pad_end[-1:] // R).astype(jnp.int32)
    last_e = jnp.max(jnp.where(blk_start < pad_end[-1], blk_e, 0))
    blk_e = jnp.minimum(blk_e, last_e).astype(jnp.int32)
    per_tile = lambda a: a.reshape(n_tiles, 8, LANES)[:, 0, :N_EXPERTS]
    cnt_flat = per_tile(cnt_t).reshape(-1)
    base_flat = (per_tile(carry_t) + pad_start[None, :]).reshape(-1)
    tail = jnp.concatenate([jnp.where(padded > 0, pad_end - R, -1), pad_end[-1:]]).astype(jnp.int32)

    xs = _dispatch(cnt_flat, base_flat, tail, pos, h1, n_blocks * R)
    ys = _experts(blk_e, n_used, xs, w_exp_in, b_exp_in[:, None, :], w_exp_out, b_exp_out[:, None, :])
    return _combine(cnt_flat, base_flat, pos, gate, h1, row2(ln2_g), row2(ln2_b), ys)


def kernel(x, mem, ln_in_g, ln_in_b, w_in_proj, b_gate, gqa_q_norm, gqa_k_norm, mla_q_norm, mla_kv_norm, w_mla_qb, w_mla_kvb, w_mem_kv, w_br_gqa, w_br_mla, w_br_mem, w_out, ln1_g, ln1_b, w_router, b_router, w_exp_in, b_exp_in, w_exp_out, b_exp_out, ln2_g, ln2_b):
    B, S, D = x.shape
    depth = w_in_proj.shape[0]
    assert depth == 1, "the input LayerNorm is fused into the first (only) layer's projection kernel"
    out = _layer(True, x.reshape(B * S, D), mem, B, S, ln_in_g, ln_in_b, w_in_proj[0], b_gate[0],
                 gqa_q_norm[0], gqa_k_norm[0], mla_q_norm[0], mla_kv_norm[0], w_mla_qb[0], w_mla_kvb[0],
                 w_mem_kv[0], w_br_gqa[0], w_br_mla[0], w_br_mem[0], w_out[0], ln1_g[0], ln1_b[0],
                 w_router[0], b_router[0], w_exp_in[0], b_exp_in[0], w_exp_out[0], b_exp_out[0],
                 ln2_g[0], ln2_b[0])
    return out.reshape(B, S, D)
```

```python
import functools

import jax
import jax.numpy as jnp
import numpy as np
from jax import lax
from jax.experimental import pallas as pl
from jax.experimental.pallas import tpu as pltpu

D_MODEL = 1024
MEM_LEN = 256
GRID_W = 64
ROPE_THETA = 10000.0
RMS_EPS = 1e-6
LN_EPS = 1e-5

GQA_HEADS = 8
GQA_KV_HEADS = 2
GQA_HEAD_DIM = 64
MLA_HEADS = 8
MLA_NOPE_DIM = 64
MLA_ROPE_DIM = 32
MLA_V_DIM = 64
MLA_Q_LORA = 384
MLA_KV_LORA = 256
MEM_HEADS = 4
MEM_HEAD_DIM = 128

N_EXPERTS = 32
TOP_K = 4
SWIGLU_LIMIT = 7.0
SWIGLU_ALPHA = 1.702
DEEPNORM_ALPHA = 2.0 ** 0.25

LANES = 128
SUBLANES = 8
BF16_SUBLANES = 16
HALF = LANES // 2
NEG_INF = float("-inf")
LOG2_E = 1.4426950408889634

OFF_QA, OFF_KA, OFF_VA, OFF_QLAT, OFF_KVLAT, OFF_KROPE, OFF_QM, OFF_GATE = (
    0, 512, 640, 768, 1152, 1408, 1440, 1952)
IN_PROJ_W = 5024
A_QA, A_KA, A_VA, A_QLAT, A_KVLAT, A_KROPE, A_QM, A_END = 0, 512, 640, 768, 1152, 1408, 1536, 2048

ROW_TILE = 256
ATTN_Q_TILE = 256
ATTN_K_CHUNK = 512
EXPERT_BLOCK = 256
VMEM_LIMIT = 48 * 1024 * 1024
EXPERTS_VMEM_LIMIT = 56 * 1024 * 1024


def _cparams(*sem):
    return pltpu.CompilerParams(dimension_semantics=sem, vmem_limit_bytes=VMEM_LIMIT)


def _lane_iota(shape):
    return lax.broadcasted_iota(jnp.int32, shape, len(shape) - 1)


def _layer_norm(x, g, b):
    mu = jnp.mean(x, axis=-1, keepdims=True)
    xc = x - mu
    var = jnp.mean(xc * xc, axis=-1, keepdims=True)
    return xc * lax.rsqrt(var + LN_EPS) * g + b


def _bdot(a, b):
    return jnp.dot(a.astype(jnp.bfloat16), b.astype(jnp.bfloat16), preferred_element_type=jnp.float32)


def _rope(x, cos, sin_signed, pair):
    lane = _lane_iota(x.shape)
    fwd = pltpu.roll(x, LANES - pair, 1)
    bwd = pltpu.roll(x, pair, 1)
    partner = jnp.where((lane % (2 * pair)) < pair, fwd, bwd)
    return x * cos + partner * sin_signed


def _half_rms_scale(x):
    lane = _lane_iota(x.shape)
    sq = x * x
    lo = jnp.sum(jnp.where(lane < HALF, sq, 0.0), axis=-1, keepdims=True)
    hi = jnp.sum(jnp.where(lane >= HALF, sq, 0.0), axis=-1, keepdims=True)
    inv = 1.0 / GQA_HEAD_DIM
    return jnp.where(lane < HALF, lax.rsqrt(lo * inv + RMS_EPS), lax.rsqrt(hi * inv + RMS_EPS))


def _softmax_rows(s):
    m = jnp.max(s, axis=-1, keepdims=True)
    p = jnp.exp(s - m)
    return p, jnp.sum(p, axis=-1, keepdims=True)


def _mem_kv_kernel(mem_ref, w_ref, kt_ref, v_ref):
    kv = _bdot(mem_ref[0], w_ref[...])
    width = MEM_HEADS * MEM_HEAD_DIM
    kt_ref[0] = kv[:, :width].T.astype(jnp.bfloat16)
    v_ref[0] = kv[:, width:].astype(jnp.bfloat16)


def _mem_kv(mem, w_mem_kv):
    B, M, D = mem.shape
    width = MEM_HEADS * MEM_HEAD_DIM
    return pl.pallas_call(
        _mem_kv_kernel,
        grid=(B,),
        in_specs=[pl.BlockSpec((1, M, D), lambda b: (b, 0, 0)),
                  pl.BlockSpec((D, 2 * width), lambda b: (0, 0))],
        out_specs=[pl.BlockSpec((1, width, M), lambda b: (b, 0, 0)),
                   pl.BlockSpec((1, M, width), lambda b: (b, 0, 0))],
        out_shape=[jax.ShapeDtypeStruct((B, width, M), jnp.bfloat16),
                   jax.ShapeDtypeStruct((B, M, width), jnp.bfloat16)],
        compiler_params=_cparams("arbitrary"),
        name="mem_kv",
    )(mem, w_mem_kv)


def _in_proj_kernel(x_ref, lng_ref, lnb_ref, wa_ref, gq_ref, gk_ref, gql_ref, gkvl_ref,
                    wqb_ref, wkvk_ref, wkvv_ref, ca_ref, sa_ref, cb_ref, sb_ref, mkt_ref, mv_ref,
                    h_ref, qa_ref, ka_ref, va_ref, qb_ref, kb_ref, vb_ref, oc_ref):
    h = _layer_norm(x_ref[...], lng_ref[...], lnb_ref[...])
    h_ref[...] = h
    proj = _bdot(h, wa_ref[...])
    lane = _lane_iota((h.shape[0], LANES))
    lo_mask = lane < HALF
    ca, sa, cb, sb = ca_ref[...], sa_ref[...], cb_ref[...], sb_ref[...]

    q_scale = GQA_HEAD_DIM ** -0.5 * LOG2_E
    per_group = GQA_HEADS // GQA_KV_HEADS
    for c in range(GQA_HEADS // 2):
        slab = proj[:, A_QA + c * LANES:A_QA + (c + 1) * LANES]
        slab = _rope(slab * _half_rms_scale(slab) * gq_ref[...], ca, sa, 16) * q_scale
        swapped = pltpu.roll(slab, HALF, 1)
        for half in range(2):
            head = 2 * c + half
            group_lo = (head // per_group) == 0
            src = slab if (half == 0) == group_lo else swapped
            keep = lo_mask if group_lo else jnp.logical_not(lo_mask)
            qa_ref[0, head * LANES:(head + 1) * LANES, :] = jnp.where(keep, src, 0.0).astype(jnp.bfloat16).T

    ka = proj[:, A_KA:A_KA + LANES]
    ka_ref[...] = _rope(ka * _half_rms_scale(ka) * gk_ref[...], ca, sa, 16).astype(jnp.bfloat16)
    va_ref[0] = proj[:, A_VA:A_VA + LANES].astype(jnp.bfloat16).T

    ql = proj[:, A_QLAT:A_KVLAT]
    ql = ql * lax.rsqrt(jnp.mean(ql * ql, axis=-1, keepdims=True) + RMS_EPS) * gql_ref[...]
    qm = _bdot(ql, wqb_ref[...])
    kvl = proj[:, A_KVLAT:A_KROPE]
    kvl = kvl * lax.rsqrt(jnp.mean(kvl * kvl, axis=-1, keepdims=True) + RMS_EPS) * gkvl_ref[...]
    kn = _bdot(kvl, wkvk_ref[...])
    vb = _bdot(kvl, wkvv_ref[...])
    for c in range(vb.shape[1] // LANES):
        sl = slice(c * LANES, (c + 1) * LANES)
        vb_ref[0, sl, :] = vb[:, sl].astype(jnp.bfloat16).T
    k_pe = _rope(proj[:, A_KROPE:A_QM], cb, sb, 8)
    qb_scale = (MLA_NOPE_DIM + MLA_ROPE_DIM) ** -0.5 * LOG2_E
    for hd in range(MLA_HEADS):
        sl = slice(hd * LANES, (hd + 1) * LANES)
        qb_ref[0, sl, :] = (_rope(qm[:, sl], cb, sb, 8) * qb_scale).astype(jnp.bfloat16).T
        kb_ref[:, sl] = (kn[:, sl] + k_pe).astype(jnp.bfloat16)

    qc = proj[:, A_QM:A_END] * (MEM_HEAD_DIM ** -0.5)
    for hd in range(MEM_HEADS):
        sl = slice(hd * MEM_HEAD_DIM, (hd + 1) * MEM_HEAD_DIM)
        p, l = _softmax_rows(_bdot(qc[:, sl], mkt_ref[0, sl, :]))
        oc_ref[:, sl] = (_bdot(p, mv_ref[0, :, sl]) / l).astype(jnp.bfloat16)


def _in_proj(x2, ln_g, ln_b, wa, gq, gk, gql, gkvl, wqb, wkvk, wkvv, ca, sa, cb, sb, mkt, mv, B, S):
    N, D = x2.shape
    T = ROW_TILE
    tiles_per_seq = S // T
    row = lambda i: (i, 0)
    const = lambda i: (0, 0)
    pos = lambda i: (i % tiles_per_seq, 0)
    batch3 = lambda i: (i // tiles_per_seq, 0, 0)
    kt_map = lambda i: (i // tiles_per_seq, 0, i % tiles_per_seq)
    full = lambda a: pl.BlockSpec(a.shape, const)
    return pl.pallas_call(
        _in_proj_kernel,
        grid=(N // T,),
        in_specs=[pl.BlockSpec((T, D), row), full(ln_g), full(ln_b), full(wa), full(gq), full(gk),
                  full(gql), full(gkvl), full(wqb), full(wkvk), full(wkvv),
                  pl.BlockSpec((T, LANES), pos), pl.BlockSpec((T, LANES), pos),
                  pl.BlockSpec((T, LANES), pos), pl.BlockSpec((T, LANES), pos),
                  pl.BlockSpec((1,) + mkt.shape[1:], batch3), pl.BlockSpec((1,) + mv.shape[1:], batch3)],
        out_specs=[pl.BlockSpec((T, D), row),
                   pl.BlockSpec((1, GQA_HEADS * LANES, T), kt_map),
                   pl.BlockSpec((T, LANES), row),
                   pl.BlockSpec((1, LANES, T), kt_map),
                   pl.BlockSpec((1, MLA_HEADS * LANES, T), kt_map),
                   pl.BlockSpec((T, MLA_HEADS * LANES), row),
                   pl.BlockSpec((1, MLA_HEADS * MLA_V_DIM, T), kt_map),
                   pl.BlockSpec((T, MEM_HEADS * MEM_HEAD_DIM), row)],
        out_shape=[jax.ShapeDtypeStruct((N, D), jnp.float32),
                   jax.ShapeDtypeStruct((B, GQA_HEADS * LANES, S), jnp.bfloat16),
                   jax.ShapeDtypeStruct((N, LANES), jnp.bfloat16),
                   jax.ShapeDtypeStruct((B, LANES, S), jnp.bfloat16),
                   jax.ShapeDtypeStruct((B, MLA_HEADS * LANES, S), jnp.bfloat16),
                   jax.ShapeDtypeStruct((N, MLA_HEADS * LANES), jnp.bfloat16),
                   jax.ShapeDtypeStruct((B, MLA_HEADS * MLA_V_DIM, S), jnp.bfloat16),
                   jax.ShapeDtypeStruct((N, MEM_HEADS * MEM_HEAD_DIM), jnp.bfloat16)],
        compiler_params=_cparams("arbitrary"),
        name="in_proj",
    )(x2, ln_g, ln_b, wa, gq, gk, gql, gkvl, wqb, wkvk, wkvv, ca, sa, cb, sb, mkt, mv)


def _attention_kernel(qt_ref, k_ref, vt_ref, o_ref, s_even, s_odd, *, shared_kv):
    t = pl.program_id(0)

    @pl.when(t == 0)
    def _():
        s_odd[...] = jnp.zeros_like(s_odd)

    def step(s_new, s_old):
        S, TQ = s_new.shape[1:]
        m = [jnp.full((1, TQ), NEG_INF, jnp.float32)] * 2
        ones = jnp.ones((BF16_SUBLANES, ATTN_K_CHUNK), jnp.bfloat16)
        acc = [jnp.zeros((HALF + BF16_SUBLANES, TQ), jnp.float32)] * 2
        for c in range(S // ATTN_K_CHUNK):
            rows = slice(c * ATTN_K_CHUNK, (c + 1) * ATTN_K_CHUNK)
            for hd in range(2):
                qt = qt_ref[0, hd * LANES:(hd + 1) * LANES, :]
                k = k_ref[0, rows, :] if shared_kv else k_ref[0, rows, hd * LANES:(hd + 1) * LANES]
                s_new[hd, rows, :] = jnp.dot(k, qt, preferred_element_type=jnp.float32)
                vt = vt_ref[0, :, rows] if shared_kv else vt_ref[0, hd * HALF:(hd + 1) * HALF, rows]
                vt = jnp.concatenate([vt, ones], axis=0)
                sc = s_old[hd, rows, :]
                m_new = jnp.maximum(m[hd], jnp.max(sc, axis=0, keepdims=True))
                p = jnp.exp2(sc - m_new).astype(jnp.bfloat16)
                corr = jnp.exp2(m[hd] - m_new)
                acc[hd] = acc[hd] * corr + jnp.dot(vt, p, preferred_element_type=jnp.float32)
                m[hd] = m_new
        outs = [acc[hd][:HALF] / acc[hd][HALF:HALF + 1] for hd in range(2)]
        o_ref[0] = jnp.concatenate(outs, axis=0).T.astype(jnp.bfloat16)

    @pl.when(t % 2 == 0)
    def _():
        step(s_even, s_odd)

    @pl.when(t % 2 == 1)
    def _():
        step(s_odd, s_even)


def _attention(qt, k, vt, *, shared_kv):
    B, qh, S = qt.shape
    pairs = qh // (2 * LANES)
    TQ = ATTN_Q_TILE
    nq = S // TQ
    items = B * pairs * nq
    pairs_per_group = pairs // GQA_KV_HEADS

    def item(t):
        return t // (pairs * nq), (t // nq) % pairs, t % nq

    score_item = lambda t: item(jnp.minimum(t, items - 1))
    finish_item = lambda t: item(jnp.maximum(t - 1, 0))

    def qt_map(t):
        b, j, i = score_item(t)
        return b, j, i

    def k_map(t):
        b, j, _ = score_item(t)
        return (b, 0, 0) if shared_kv else (b, 0, j)

    def vt_map(t):
        b, j, _ = finish_item(t)
        return (b, j // pairs_per_group, 0) if shared_kv else (b, j, 0)

    def o_map(t):
        b, j, i = finish_item(t)
        return b, i, j

    k_spec = pl.BlockSpec((1, S, LANES if shared_kv else 2 * LANES), k_map)
    vt_spec = pl.BlockSpec((1, HALF if shared_kv else LANES, S), vt_map)
    return pl.pallas_call(
        functools.partial(_attention_kernel, shared_kv=shared_kv),
        grid=(items + 1,),
        in_specs=[pl.BlockSpec((1, 2 * LANES, TQ), qt_map), k_spec, vt_spec],
        out_specs=pl.BlockSpec((1, TQ, LANES), o_map),
        out_shape=jax.ShapeDtypeStruct((B, S, pairs * LANES), jnp.bfloat16),
        scratch_shapes=[pltpu.VMEM((2, S, TQ), jnp.float32), pltpu.VMEM((2, S, TQ), jnp.float32)],
        compiler_params=_cparams("arbitrary"),
        name="attention_gqa" if shared_kv else "attention_mla",
    )(qt, k, vt)


def _split2(x):
    hi = x.astype(jnp.bfloat16)
    return hi, (x - hi.astype(jnp.float32)).astype(jnp.bfloat16)


def _mix_out_kernel(h_ref, oa_ref, ob_ref, oc_ref, wg_ref, bg_ref, wa_ref, wb_ref, wc_ref, wo_ref,
                    g_ref, b_ref, wr_ref, br_ref, h1_ref, idx_ref, gate_ref):
    h = h_ref[...]
    gates = jax.nn.sigmoid(_bdot(h, wg_ref[...]) + bg_ref[...])
    D = D_MODEL
    merged = (gates[:, :D] * _bdot(oa_ref[...], wa_ref[...])
              + gates[:, D:2 * D] * _bdot(ob_ref[...], wb_ref[...])
              + gates[:, 2 * D:] * _bdot(oc_ref[...], wc_ref[...]))
    mix = _bdot(merged, wo_ref[...])
    h1 = _layer_norm(DEEPNORM_ALPHA * h + mix, g_ref[...], b_ref[...])
    h1_ref[...] = h1

    h_hi, h_lo = _split2(h1)
    w_hi, w_lo = wr_ref[0], wr_ref[1]
    dot = lambda a, b: jnp.dot(a, b, preferred_element_type=jnp.float32)
    logits = dot(h_hi, w_hi) + (dot(h_hi, w_lo) + dot(h_lo, w_hi)) + br_ref[...]
    lane = _lane_iota(logits.shape)
    logits = jnp.where(lane < N_EXPERTS, logits, NEG_INF)

    idx_out = jnp.zeros(logits.shape, jnp.int32)
    val_out = jnp.zeros(logits.shape, jnp.float32)
    top = None
    for k in range(TOP_K):
        m = jnp.max(logits, axis=-1, keepdims=True)
        idx = jnp.min(jnp.where(logits == m, lane, LANES), axis=-1, keepdims=True)
        logits = jnp.where(lane == idx, NEG_INF, logits)
        top = m if top is None else top
        idx_out = jnp.where(lane == k, idx, idx_out)
        val_out = jnp.where(lane == k, jnp.exp(m - top), val_out)
    idx_ref[...] = idx_out
    gate_ref[...] = val_out / jnp.sum(val_out, axis=-1, keepdims=True)


def _mix_out(h, oa, ob, oc, wg, bg, wa, wb, wc, wo, g, b, wr3, br):
    N, D = h.shape
    T = ROW_TILE
    row = lambda i: (i, 0)
    full = lambda a: pl.BlockSpec(a.shape, lambda i: (0,) * a.ndim)
    return pl.pallas_call(
        _mix_out_kernel,
        grid=(N // T,),
        in_specs=[pl.BlockSpec((T, D), row), pl.BlockSpec((T, oa.shape[1]), row),
                  pl.BlockSpec((T, ob.shape[1]), row), pl.BlockSpec((T, oc.shape[1]), row),
                  full(wg), full(bg), full(wa), full(wb), full(wc), full(wo), full(g), full(b),
                  full(wr3), full(br)],
        out_specs=[pl.BlockSpec((T, D), row), pl.BlockSpec((T, LANES), row), pl.BlockSpec((T, LANES), row)],
        out_shape=[jax.ShapeDtypeStruct((N, D), jnp.float32),
                   jax.ShapeDtypeStruct((N, LANES), jnp.int32),
                   jax.ShapeDtypeStruct((N, LANES), jnp.float32)],
        compiler_params=_cparams("arbitrary"),
        name="mix_out",
    )(h, oa, ob, oc, wg, bg, wa, wb, wc, wo, g, b, wr3, br)


def _route_kernel(idx_ref, pos_ref, carry_out_ref, cnt_out_ref, total_ref, carry_ref):
    @pl.when(pl.program_id(0) == 0)
    def _():
        carry_ref[...] = jnp.zeros_like(carry_ref)

    idx = idx_ref[...]
    T = idx.shape[0]
    lane = _lane_iota(idx.shape)
    sel = [lane == idx[:, k:k + 1] for k in range(TOP_K)]
    onehot = sum(s.astype(jnp.float32) for s in sel)
    r = lax.broadcasted_iota(jnp.int32, (T, T), 0)
    c = lax.broadcasted_iota(jnp.int32, (T, T), 1)
    before = (c < r).astype(jnp.bfloat16)
    prefix = jnp.dot(before, onehot.astype(jnp.bfloat16), preferred_element_type=jnp.float32)
    cnt = jnp.broadcast_to(jnp.sum(onehot, axis=0, keepdims=True), carry_ref.shape)
    cnt = jnp.floor((cnt + (SUBLANES - 1)) * (1.0 / SUBLANES)) * SUBLANES
    er = lax.broadcasted_iota(jnp.int32, (LANES, LANES), 0)
    ec = lax.broadcasted_iota(jnp.int32, (LANES, LANES), 1)
    local_start = jnp.dot(cnt.astype(jnp.bfloat16), (er < ec).astype(jnp.bfloat16),
                          preferred_element_type=jnp.float32)
    target = prefix + local_start[0:1, :]
    out = jnp.zeros(idx.shape, jnp.int32)
    for k in range(TOP_K):
        pk = jnp.sum(jnp.where(sel[k], target, 0.0), axis=-1, keepdims=True)
        out = jnp.where(lane == k, pk.astype(jnp.int32), out)
    pos_ref[...] = out
    carry_out_ref[...] = carry_ref[...].astype(jnp.int32)
    cnt_out_ref[...] = cnt.astype(jnp.int32)
    carry_ref[...] = carry_ref[...] + cnt
    total_ref[...] = carry_ref[...].astype(jnp.int32)


def _route(idx):
    N = idx.shape[0]
    T = ROW_TILE
    n_tiles = N // T
    tile8 = pl.BlockSpec((8, LANES), lambda i: (i, 0))
    return pl.pallas_call(
        _route_kernel,
        grid=(n_tiles,),
        in_specs=[pl.BlockSpec((T, LANES), lambda i: (i, 0))],
        out_specs=[pl.BlockSpec((T, LANES), lambda i: (i, 0)), tile8, tile8,
                   pl.BlockSpec((8, LANES), lambda i: (0, 0))],
        out_shape=[jax.ShapeDtypeStruct((N, LANES), jnp.int32),
                   jax.ShapeDtypeStruct((n_tiles * 8, LANES), jnp.int32),
                   jax.ShapeDtypeStruct((n_tiles * 8, LANES), jnp.int32),
                   jax.ShapeDtypeStruct((8, LANES), jnp.int32)],
        scratch_shapes=[pltpu.VMEM((8, LANES), jnp.float32)],
        compiler_params=_cparams("arbitrary"),
        name="route",
    )(idx)


STRIP_SIZES = tuple(ROW_TILE >> s for s in range((ROW_TILE // SUBLANES).bit_length()))
LOCAL_ROWS = TOP_K * ROW_TILE + ROW_TILE
assert LOCAL_ROWS >= TOP_K * ROW_TILE + N_EXPERTS * (SUBLANES - 1)


HI16 = 0xFFFF0000


def _pack_rows(x):
    half = x.shape[1] // 2
    lo = lax.bitcast_convert_type(x[:, :half], jnp.uint32) >> 16
    hi = lax.bitcast_convert_type(x[:, half:], jnp.uint32) & jnp.uint32(HI16)
    return lo | hi


def _unpack_rows(w):
    lo = lax.bitcast_convert_type(w << 16, jnp.float32)
    hi = lax.bitcast_convert_type(w & jnp.uint32(HI16), jnp.float32)
    return jnp.concatenate([lo, hi], axis=1).astype(jnp.bfloat16)


def _round_bf16(x):
    return x.astype(jnp.bfloat16).astype(jnp.float32)


def _for_each_strip(cnt_ref, base_ref, tile, fn):
    def body(e, off):
        c = cnt_ref[tile * N_EXPERTS + e]
        d = base_ref[tile * N_EXPERTS + e]
        for bit, size in enumerate(STRIP_SIZES):
            above = c & (-2 * size)

            @pl.when((c & size) != 0)
            def _():
                fn(pl.multiple_of(off + above, SUBLANES), pl.multiple_of(d + above, SUBLANES), size, bit)
        return off + c

    lax.fori_loop(0, N_EXPERTS, body, 0)


def _dispatch_kernel(cnt_ref, base_ref, tail_ref, pos_ref, h_ref, xs_ref, buf, zeros, sems, zsem):
    i = pl.program_id(0)
    last = pl.num_programs(0) - 1
    T = h_ref.shape[0]
    cur = i % 2

    @pl.when(i == 0)
    def _():
        zeros[...] = jnp.zeros_like(zeros)

        def tail_copy(e):
            row = pl.multiple_of(jnp.maximum(tail_ref[e], 0), SUBLANES)
            return pltpu.make_async_copy(zeros, xs_ref.at[pl.ds(row, EXPERT_BLOCK), :], zsem)

        def spare_copy(blk):
            row = pl.multiple_of(blk * EXPERT_BLOCK, EXPERT_BLOCK)
            return pltpu.make_async_copy(zeros, xs_ref.at[pl.ds(row, EXPERT_BLOCK), :], zsem)

        first_spare = tail_ref[N_EXPERTS] // EXPERT_BLOCK
        for act in ("start", "wait"):
            def body(e, carry, act=act):
                @pl.when(tail_ref[e] >= 0)
                def _():
                    getattr(tail_copy(e), act)()
                return carry
            lax.fori_loop(0, N_EXPERTS, body, 0)

            def spare(blk, carry, act=act):
                getattr(spare_copy(blk), act)()
                return carry
            lax.fori_loop(first_spare, xs_ref.shape[0] // EXPERT_BLOCK, spare, 0)

    pos_t = pos_ref[...].astype(jnp.float32).T.astype(jnp.int32)
    rows = lax.broadcasted_iota(jnp.int32, (LOCAL_ROWS, T), 0)
    hit = rows == pos_t[0:1, :]
    for k in range(1, TOP_K):
        hit = jnp.logical_or(hit, rows == pos_t[k:k + 1, :])
    buf[cur] = _pack_rows(jnp.dot(hit.astype(jnp.bfloat16), h_ref[...].astype(jnp.bfloat16),
                                  preferred_element_type=jnp.float32))

    def strip(buf_slot):
        def make(local_row, slot_row, size, bit):
            return pltpu.make_async_copy(buf.at[buf_slot, pl.ds(local_row, size), :],
                                         xs_ref.at[pl.ds(slot_row, size), :], sems.at[bit])
        return make

    @pl.when(i > 0)
    def _():
        _for_each_strip(cnt_ref, base_ref, i - 1, lambda *a: strip(1 - cur)(*a).wait())

    _for_each_strip(cnt_ref, base_ref, i, lambda *a: strip(cur)(*a).start())

    @pl.when(i == last)
    def _():
        _for_each_strip(cnt_ref, base_ref, i, lambda *a: strip(cur)(*a).wait())


def _dispatch(cnt_flat, base_flat, tail, pos, h1, n_slots):
    N, D = h1.shape
    T = ROW_TILE
    return pl.pallas_call(
        _dispatch_kernel,
        grid_spec=pltpu.PrefetchScalarGridSpec(
            num_scalar_prefetch=3,
            grid=(N // T,),
            in_specs=[pl.BlockSpec((T, LANES), lambda i, c, b, t: (i, 0)),
                      pl.BlockSpec((T, D), lambda i, c, b, t: (i, 0))],
            out_specs=pl.BlockSpec(memory_space=pl.ANY),
            scratch_shapes=[pltpu.VMEM((2, LOCAL_ROWS, D // 2), jnp.uint32),
                            pltpu.VMEM((EXPERT_BLOCK, D // 2), jnp.uint32),
                            pltpu.SemaphoreType.DMA((len(STRIP_SIZES),)),
                            pltpu.SemaphoreType.DMA(())]),
        out_shape=jax.ShapeDtypeStruct((n_slots, D // 2), jnp.uint32),
        compiler_params=_cparams("arbitrary"),
        name="dispatch",
    )(cnt_flat, base_flat, tail, pos, h1)


def _experts_kernel(blk_e_ref, n_used_ref, x_ref, wi_ref, bi_ref, wo_ref, bo_ref, y_ref, wi_bf, wo_bf):
    i = pl.program_id(0)
    used = i < n_used_ref[0]
    new_expert = jnp.logical_or(i == 0, blk_e_ref[i] != blk_e_ref[jnp.maximum(i - 1, 0)])

    @pl.when(jnp.logical_and(used, new_expert))
    def _():
        wi_bf[...] = wi_ref[0].astype(jnp.bfloat16)
        wo_bf[...] = wo_ref[0].astype(jnp.bfloat16)

    @pl.when(used)
    def _():
        hb = jnp.dot(_unpack_rows(x_ref[...]), wi_bf[...], preferred_element_type=jnp.float32) + bi_ref[0]
        De = D_MODEL
        x_glu = jnp.minimum(hb[:, :De], SWIGLU_LIMIT)
        x_lin = jnp.clip(hb[:, De:], -SWIGLU_LIMIT, SWIGLU_LIMIT)
        act = x_glu * jax.nn.sigmoid(SWIGLU_ALPHA * x_glu) * (x_lin + 1.0)
        y = jnp.dot(act.astype(jnp.bfloat16), wo_bf[...], preferred_element_type=jnp.float32) + bo_ref[0]
        y_ref[...] = _pack_rows(_round_bf16(y))

    @pl.when(jnp.logical_not(used))
    def _():
        y_ref[...] = jnp.zeros_like(y_ref)


def _experts(blk_e, n_used, xs, w_in, b_in, w_out, b_out):
    n_slots, half = xs.shape
    R = EXPERT_BLOCK
    E, D, F = w_in.shape
    return pl.pallas_call(
        _experts_kernel,
        grid_spec=pltpu.PrefetchScalarGridSpec(
            num_scalar_prefetch=2,
            grid=(n_slots // R,),
            in_specs=[pl.BlockSpec((R, half), lambda i, e, n: (jnp.minimum(i, n[0] - 1), 0)),
                      pl.BlockSpec((1, D, F), lambda i, e, n: (e[i], 0, 0)),
                      pl.BlockSpec((1, 1, F), lambda i, e, n: (e[i], 0, 0)),
                      pl.BlockSpec((1, F // 2, D), lambda i, e, n: (e[i], 0, 0)),
                      pl.BlockSpec((1, 1, D), lambda i, e, n: (e[i], 0, 0))],
            out_specs=pl.BlockSpec((R, half), lambda i, e, n: (i, 0)),
            scratch_shapes=[pltpu.VMEM((D, F), jnp.bfloat16), pltpu.VMEM((F // 2, D), jnp.bfloat16)]),
        out_shape=jax.ShapeDtypeStruct((n_slots, half), jnp.uint32),
        compiler_params=pltpu.CompilerParams(dimension_semantics=("arbitrary",),
                                             vmem_limit_bytes=EXPERTS_VMEM_LIMIT),
        name="experts",
    )(blk_e, n_used, xs, w_in, b_in, w_out, b_out)


def _combine_kernel(cnt_ref, base_ref, pos_ref, gate_ref, h_ref, g_ref, b_ref, ys_ref, o_ref, buf, sems):
    i = pl.program_id(0)
    last = pl.num_programs(0) - 1
    T = h_ref.shape[0]
    cur = i % 2

    def strip(buf_slot):
        def make(local_row, slot_row, size, bit):
            return pltpu.make_async_copy(ys_ref.at[pl.ds(slot_row, size), :],
                                         buf.at[buf_slot, pl.ds(local_row, size), :], sems.at[bit])
        return make

    @pl.when(i == 0)
    def _():
        buf[...] = jnp.zeros_like(buf)
        _for_each_strip(cnt_ref, base_ref, i, lambda *a: strip(cur)(*a).start())

    _for_each_strip(cnt_ref, base_ref, i, lambda *a: strip(cur)(*a).wait())

    @pl.when(i < last)
    def _():
        _for_each_strip(cnt_ref, base_ref, i + 1, lambda *a: strip(1 - cur)(*a).start())

    pos, gate = pos_ref[...], gate_ref[...]
    col = lax.broadcasted_iota(jnp.int32, (T, LOCAL_ROWS), 1)
    u = jnp.zeros(col.shape, jnp.float32)
    for k in range(TOP_K):
        u = jnp.where(col == pos[:, k:k + 1], gate[:, k:k + 1], u)
    u_hi, u_lo = _split2(u)
    y = _unpack_rows(buf[cur])
    dot = lambda a, b: jnp.dot(a, b, preferred_element_type=jnp.float32)
    ffn = dot(u_hi, y) + dot(u_lo, y)
    o_ref[...] = _layer_norm(DEEPNORM_ALPHA * h_ref[...] + ffn, g_ref[...], b_ref[...])


def _combine(cnt_flat, base_flat, pos, gate, h1, g, b, ys):
    N, D = h1.shape
    T = ROW_TILE
    row = lambda i, c, s: (i, 0)
    const = lambda i, c, s: (0, 0)
    return pl.pallas_call(
        _combine_kernel,
        grid_spec=pltpu.PrefetchScalarGridSpec(
            num_scalar_prefetch=2,
            grid=(N // T,),
            in_specs=[pl.BlockSpec((T, LANES), row), pl.BlockSpec((T, LANES), row), pl.BlockSpec((T, D), row),
                      pl.BlockSpec(g.shape, const), pl.BlockSpec(b.shape, const),
                      pl.BlockSpec(memory_space=pl.ANY)],
            out_specs=pl.BlockSpec((T, D), row),
            scratch_shapes=[pltpu.VMEM((2, LOCAL_ROWS, D // 2), jnp.uint32),
                            pltpu.SemaphoreType.DMA((len(STRIP_SIZES),))]),
        out_shape=jax.ShapeDtypeStruct((N, D), jnp.float32),
        compiler_params=_cparams("arbitrary"),
        name="combine",
    )(cnt_flat, base_flat, pos, gate, h1, g, b, ys)


def _rope_tables(S):
    t = np.arange(S)
    row, col = (t // GRID_W).astype(np.float64), (t % GRID_W).astype(np.float64)

    def block(half):
        inv = ROPE_THETA ** (-np.arange(half, dtype=np.float64) / half)
        ar, ac = row[:, None] * inv[None, :], col[:, None] * inv[None, :]
        cos = np.concatenate([np.cos(ar), np.cos(ar), np.cos(ac), np.cos(ac)], axis=1)
        sin = np.concatenate([-np.sin(ar), np.sin(ar), -np.sin(ac), np.sin(ac)], axis=1)
        return cos, sin

    ca64, sa64 = block(GQA_HEAD_DIM // 4)
    ca, sa = np.tile(ca64, (1, 2)), np.tile(sa64, (1, 2))
    cb32, sb32 = block(MLA_ROPE_DIM // 4)
    ones, zeros = np.ones((S, MLA_NOPE_DIM)), np.zeros((S, MLA_NOPE_DIM))
    cb = np.concatenate([ones, cb32, ones[:, :LANES - MLA_NOPE_DIM - MLA_ROPE_DIM]], axis=1)
    sb = np.concatenate([zeros, sb32, zeros[:, :LANES - MLA_NOPE_DIM - MLA_ROPE_DIM]], axis=1)
    return tuple(jnp.asarray(a, jnp.float32) for a in (ca, sa, cb, sb))


def _layer(h_in_is_x, x2, mem, B, S, ln_g, ln_b, w_in_proj, b_gate, gqa_q_norm, gqa_k_norm, mla_q_norm,
           mla_kv_norm, w_mla_qb, w_mla_kvb, w_mem_kv, w_br_gqa, w_br_mla, w_br_mem, w_out,
           ln1_g, ln1_b, w_router, b_router, w_exp_in, b_exp_in, w_exp_out, b_exp_out, ln2_g, ln2_b):
    del h_in_is_x
    bf = jnp.bfloat16
    N, D = x2.shape
    row2 = lambda a: a.reshape(1, -1)

    W = w_in_proj
    zc = lambda n: jnp.zeros((D, n), W.dtype)
    wa = jnp.concatenate([W[:, :OFF_KROPE], zc(HALF), W[:, OFF_KROPE:OFF_QM],
                          zc(LANES - HALF - MLA_ROPE_DIM), W[:, OFF_QM:OFF_GATE]], axis=1).astype(bf)
    wg = W[:, OFF_GATE:].astype(bf)
    qd = MLA_NOPE_DIM + MLA_ROPE_DIM
    wqb = jnp.pad(w_mla_qb.reshape(MLA_Q_LORA, MLA_HEADS, qd),
                  ((0, 0), (0, 0), (0, LANES - qd))).reshape(MLA_Q_LORA, MLA_HEADS * LANES).astype(bf)
    kvb = w_mla_kvb.reshape(MLA_KV_LORA, MLA_HEADS, MLA_NOPE_DIM + MLA_V_DIM)
    wkvk = jnp.pad(kvb[:, :, :MLA_NOPE_DIM], ((0, 0), (0, 0), (0, LANES - MLA_NOPE_DIM))
                   ).reshape(MLA_KV_LORA, MLA_HEADS * LANES).astype(bf)
    wkvv = kvb[:, :, MLA_NOPE_DIM:].reshape(MLA_KV_LORA, MLA_HEADS * MLA_V_DIM).astype(bf)
    gq = row2(jnp.tile(gqa_q_norm, 2))
    gk = row2(jnp.tile(gqa_k_norm, 2))
    ca, sa, cb, sb = _rope_tables(S)

    mkt, mv = _mem_kv(mem, w_mem_kv.astype(bf))
    h, qat, ka, vat, qbt, kb, vbt, oc = _in_proj(
        x2, row2(ln_g), row2(ln_b), wa, gq, gk, row2(mla_q_norm), row2(mla_kv_norm),
        wqb, wkvk, wkvv, ca, sa, cb, sb, mkt, mv, B, S)

    oa = _attention(qat, ka.reshape(B, S, -1), vat, shared_kv=True)
    ob = _attention(qbt, kb.reshape(B, S, -1), vbt, shared_kv=False)

    wr = jnp.pad(w_router, ((0, 0), (0, LANES - N_EXPERTS)))
    wr3 = jnp.stack(_split2(wr))
    br = row2(jnp.pad(b_router, (0, LANES - N_EXPERTS)))
    h1, idx, gate = _mix_out(h, oa.reshape(N, -1), ob.reshape(N, -1), oc, wg, row2(b_gate),
                             w_br_gqa.astype(bf), w_br_mla.astype(bf), w_br_mem.astype(bf),
                             w_out.astype(bf), row2(ln1_g), row2(ln1_b), wr3, br)

    pos, carry_t, cnt_t, total = _route(idx)
    counts = total[0, :N_EXPERTS]
    R = EXPERT_BLOCK
    padded = (counts + R - 1) // R * R
    pad_end = jnp.cumsum(padded)
    pad_start = (pad_end - padded).astype(jnp.int32)
    n_tiles = N // ROW_TILE
    max_rows = N * TOP_K + n_tiles * N_EXPERTS * (SUBLANES - 1) + N_EXPERTS * (R - 1)
    n_blocks = -(-max_rows // R)
    blk_start = jnp.arange(n_blocks, dtype=jnp.int32) * R
    blk_e = jnp.sum((pad_end[None, :] <= blk_start[:, None]).astype(jnp.int32), axis=1)
    n_used = (pad_end[-1:] // R).astype(jnp.int32)
    last_e = jnp.max(jnp.where(blk_start < pad_end[-1], blk_e, 0))
    blk_e = jnp.minimum(blk_e, last_e).astype(jnp.int32)
    per_tile = lambda a: a.reshape(n_tiles, 8, LANES)[:, 0, :N_EXPERTS]
    cnt_flat = per_tile(cnt_t).reshape(-1)
    base_flat = (per_tile(carry_t) + pad_start[None, :]).reshape(-1)
    tail = jnp.concatenate([jnp.where(padded > 0, pad_end - R, -1), pad_end[-1:]]).astype(jnp.int32)

    xs = _dispatch(cnt_flat, base_flat, tail, pos, h1, n_blocks * R)
    ys = _experts(blk_e, n_used, xs, w_exp_in, b_exp_in[:, None, :], w_exp_out, b_exp_out[:, None, :])
    return _combine(cnt_flat, base_flat, pos, gate, h1, row2(ln2_g), row2(ln2_b), ys)


def kernel(x, mem, ln_in_g, ln_in_b, w_in_proj, b_gate, gqa_q_norm, gqa_k_norm, mla_q_norm, mla_kv_norm, w_mla_qb, w_mla_kvb, w_mem_kv, w_br_gqa, w_br_mla, w_br_mem, w_out, ln1_g, ln1_b, w_router, b_router, w_exp_in, b_exp_in, w_exp_out, b_exp_out, ln2_g, ln2_b):
    B, S, D = x.shape
    depth = w_in_proj.shape[0]
    assert depth == 1, "the input LayerNorm is fused into the first (only) layer's projection kernel"
    out = _layer(True, x.reshape(B * S, D), mem, B, S, ln_in_g, ln_in_b, w_in_proj[0], b_gate[0],
                 gqa_q_norm[0], gqa_k_norm[0], mla_q_norm[0], mla_kv_norm[0], w_mla_qb[0], w_mla_kvb[0],
                 w_mem_kv[0], w_br_gqa[0], w_br_mla[0], w_br_mem[0], w_out[0], ln1_g[0], ln1_b[0],
                 w_router[0], b_router[0], w_exp_in[0], b_exp_in[0], w_exp_out[0], b_exp_out[0],
                 ln2_g[0], ln2_b[0])
    return out.reshape(B, S, D)
```

```python
import functools

import jax
import jax.numpy as jnp
import numpy as np
from jax import lax
from jax.experimental import pallas as pl
from jax.experimental.pallas import tpu as pltpu

D_MODEL = 1024
MEM_LEN = 256
GRID_W = 64
ROPE_THETA = 10000.0
RMS_EPS = 1e-6
LN_EPS = 1e-5

GQA_HEADS = 8
GQA_KV_HEADS = 2
GQA_HEAD_DIM = 64
MLA_HEADS = 8
MLA_NOPE_DIM = 64
MLA_ROPE_DIM = 32
MLA_V_DIM = 64
MLA_Q_LORA = 384
MLA_KV_LORA = 256
MEM_HEADS = 4
MEM_HEAD_DIM = 128

N_EXPERTS = 32
TOP_K = 4
SWIGLU_LIMIT = 7.0
SWIGLU_ALPHA = 1.702
DEEPNORM_ALPHA = 2.0 ** 0.25

LANES = 128
SUBLANES = 8
BF16_SUBLANES = 16
HALF = LANES // 2
NEG_INF = float("-inf")
LOG2_E = 1.4426950408889634

OFF_QA, OFF_KA, OFF_VA, OFF_QLAT, OFF_KVLAT, OFF_KROPE, OFF_QM, OFF_GATE = (
    0, 512, 640, 768, 1152, 1408, 1440, 1952)
IN_PROJ_W = 5024
A_QA, A_KA, A_VA, A_QLAT, A_KVLAT, A_KROPE, A_QM, A_END = 0, 512, 640, 768, 1152, 1408, 1536, 2048

ROW_TILE = 256
MIX_TILE = 512
MIX_COLS = 256
ATTN_Q_TILE = 256
ATTN_K_CHUNK = 512
EXPERT_BLOCK = 256
VMEM_LIMIT = 48 * 1024 * 1024
EXPERTS_VMEM_LIMIT = 56 * 1024 * 1024


def _cparams(*sem):
    return pltpu.CompilerParams(dimension_semantics=sem, vmem_limit_bytes=VMEM_LIMIT)


def _lane_iota(shape):
    return lax.broadcasted_iota(jnp.int32, shape, len(shape) - 1)


def _layer_norm(x, g, b):
    mu = jnp.mean(x, axis=-1, keepdims=True)
    xc = x - mu
    var = jnp.mean(xc * xc, axis=-1, keepdims=True)
    return xc * lax.rsqrt(var + LN_EPS) * g + b


def _bdot(a, b):
    return jnp.dot(a.astype(jnp.bfloat16), b.astype(jnp.bfloat16), preferred_element_type=jnp.float32)


def _rope(x, cos, sin_signed, pair):
    lane = _lane_iota(x.shape)
    fwd = pltpu.roll(x, LANES - pair, 1)
    bwd = pltpu.roll(x, pair, 1)
    partner = jnp.where((lane % (2 * pair)) < pair, fwd, bwd)
    return x * cos + partner * sin_signed


def _half_rms_scale(x):
    lane = _lane_iota(x.shape)
    sq = x * x
    lo = jnp.sum(jnp.where(lane < HALF, sq, 0.0), axis=-1, keepdims=True)
    hi = jnp.sum(jnp.where(lane >= HALF, sq, 0.0), axis=-1, keepdims=True)
    inv = 1.0 / GQA_HEAD_DIM
    return jnp.where(lane < HALF, lax.rsqrt(lo * inv + RMS_EPS), lax.rsqrt(hi * inv + RMS_EPS))


def _softmax_rows(s):
    m = jnp.max(s, axis=-1, keepdims=True)
    p = jnp.exp(s - m)
    return p, jnp.sum(p, axis=-1, keepdims=True)


def _mem_kv_kernel(mem_ref, w_ref, kt_ref, v_ref):
    kv = _bdot(mem_ref[0], w_ref[...])
    width = MEM_HEADS * MEM_HEAD_DIM
    kt_ref[0] = kv[:, :width].T.astype(jnp.bfloat16)
    v_ref[0] = kv[:, width:].astype(jnp.bfloat16)


def _mem_kv(mem, w_mem_kv):
    B, M, D = mem.shape
    width = MEM_HEADS * MEM_HEAD_DIM
    return pl.pallas_call(
        _mem_kv_kernel,
        grid=(B,),
        in_specs=[pl.BlockSpec((1, M, D), lambda b: (b, 0, 0)),
                  pl.BlockSpec((D, 2 * width), lambda b: (0, 0))],
        out_specs=[pl.BlockSpec((1, width, M), lambda b: (b, 0, 0)),
                   pl.BlockSpec((1, M, width), lambda b: (b, 0, 0))],
        out_shape=[jax.ShapeDtypeStruct((B, width, M), jnp.bfloat16),
                   jax.ShapeDtypeStruct((B, M, width), jnp.bfloat16)],
        compiler_params=_cparams("arbitrary"),
        name="mem_kv",
    )(mem, w_mem_kv)


def _in_proj_kernel(x_ref, lng_ref, lnb_ref, wa_ref, gq_ref, gk_ref, gql_ref, gkvl_ref,
                    wqb_ref, wkvk_ref, wkvv_ref, ca_ref, sa_ref, cb_ref, sb_ref, mkt_ref, mv_ref,
                    h_ref, qa_ref, ka_ref, va_ref, qb_ref, kb_ref, vb_ref, oc_ref):
    h = _layer_norm(x_ref[...], lng_ref[...], lnb_ref[...])
    h_ref[...] = h
    proj = _bdot(h, wa_ref[...])
    lane = _lane_iota((h.shape[0], LANES))
    lo_mask = lane < HALF
    ca, sa, cb, sb = ca_ref[...], sa_ref[...], cb_ref[...], sb_ref[...]

    q_scale = GQA_HEAD_DIM ** -0.5 * LOG2_E
    per_group = GQA_HEADS // GQA_KV_HEADS
    for c in range(GQA_HEADS // 2):
        slab = proj[:, A_QA + c * LANES:A_QA + (c + 1) * LANES]
        slab = _rope(slab * _half_rms_scale(slab) * gq_ref[...], ca, sa, 16) * q_scale
        swapped = pltpu.roll(slab, HALF, 1)
        for half in range(2):
            head = 2 * c + half
            group_lo = (head // per_group) == 0
            src = slab if (half == 0) == group_lo else swapped
            keep = lo_mask if group_lo else jnp.logical_not(lo_mask)
            qa_ref[0, head * LANES:(head + 1) * LANES, :] = jnp.where(keep, src, 0.0).astype(jnp.bfloat16).T

    ka = proj[:, A_KA:A_KA + LANES]
    ka_ref[...] = _rope(ka * _half_rms_scale(ka) * gk_ref[...], ca, sa, 16).astype(jnp.bfloat16)
    va_ref[0] = proj[:, A_VA:A_VA + LANES].astype(jnp.bfloat16).T

    ql = proj[:, A_QLAT:A_KVLAT]
    ql = ql * lax.rsqrt(jnp.mean(ql * ql, axis=-1, keepdims=True) + RMS_EPS) * gql_ref[...]
    qm = _bdot(ql, wqb_ref[...])
    kvl = proj[:, A_KVLAT:A_KROPE]
    kvl = kvl * lax.rsqrt(jnp.mean(kvl * kvl, axis=-1, keepdims=True) + RMS_EPS) * gkvl_ref[...]
    kn = _bdot(kvl, wkvk_ref[...])
    vb = _bdot(kvl, wkvv_ref[...])
    for c in range(vb.shape[1] // LANES):
        sl = slice(c * LANES, (c + 1) * LANES)
        vb_ref[0, sl, :] = vb[:, sl].astype(jnp.bfloat16).T
    k_pe = _rope(proj[:, A_KROPE:A_QM], cb, sb, 8)
    qb_scale = (MLA_NOPE_DIM + MLA_ROPE_DIM) ** -0.5 * LOG2_E
    for hd in range(MLA_HEADS):
        sl = slice(hd * LANES, (hd + 1) * LANES)
        qb_ref[0, sl, :] = (_rope(qm[:, sl], cb, sb, 8) * qb_scale).astype(jnp.bfloat16).T
        kb_ref[:, sl] = (kn[:, sl] + k_pe).astype(jnp.bfloat16)

    qc = proj[:, A_QM:A_END] * (MEM_HEAD_DIM ** -0.5)
    for hd in range(MEM_HEADS):
        sl = slice(hd * MEM_HEAD_DIM, (hd + 1) * MEM_HEAD_DIM)
        p, l = _softmax_rows(_bdot(qc[:, sl], mkt_ref[0, sl, :]))
        oc_ref[:, sl] = (_bdot(p, mv_ref[0, :, sl]) / l).astype(jnp.bfloat16)


def _in_proj(x2, ln_g, ln_b, wa, gq, gk, gql, gkvl, wqb, wkvk, wkvv, ca, sa, cb, sb, mkt, mv, B, S):
    N, D = x2.shape
    T = ROW_TILE
    tiles_per_seq = S // T
    row = lambda i: (i, 0)
    const = lambda i: (0, 0)
    pos = lambda i: (i % tiles_per_seq, 0)
    batch3 = lambda i: (i // tiles_per_seq, 0, 0)
    kt_map = lambda i: (i // tiles_per_seq, 0, i % tiles_per_seq)
    full = lambda a: pl.BlockSpec(a.shape, const)
    return pl.pallas_call(
        _in_proj_kernel,
        grid=(N // T,),
        in_specs=[pl.BlockSpec((T, D), row), full(ln_g), full(ln_b), full(wa), full(gq), full(gk),
                  full(gql), full(gkvl), full(wqb), full(wkvk), full(wkvv),
                  pl.BlockSpec((T, LANES), pos), pl.BlockSpec((T, LANES), pos),
                  pl.BlockSpec((T, LANES), pos), pl.BlockSpec((T, LANES), pos),
                  pl.BlockSpec((1,) + mkt.shape[1:], batch3), pl.BlockSpec((1,) + mv.shape[1:], batch3)],
        out_specs=[pl.BlockSpec((T, D), row),
                   pl.BlockSpec((1, GQA_HEADS * LANES, T), kt_map),
                   pl.BlockSpec((T, LANES), row),
                   pl.BlockSpec((1, LANES, T), kt_map),
                   pl.BlockSpec((1, MLA_HEADS * LANES, T), kt_map),
                   pl.BlockSpec((T, MLA_HEADS * LANES), row),
                   pl.BlockSpec((1, MLA_HEADS * MLA_V_DIM, T), kt_map),
                   pl.BlockSpec((T, MEM_HEADS * MEM_HEAD_DIM), row)],
        out_shape=[jax.ShapeDtypeStruct((N, D), jnp.float32),
                   jax.ShapeDtypeStruct((B, GQA_HEADS * LANES, S), jnp.bfloat16),
                   jax.ShapeDtypeStruct((N, LANES), jnp.bfloat16),
                   jax.ShapeDtypeStruct((B, LANES, S), jnp.bfloat16),
                   jax.ShapeDtypeStruct((B, MLA_HEADS * LANES, S), jnp.bfloat16),
                   jax.ShapeDtypeStruct((N, MLA_HEADS * LANES), jnp.bfloat16),
                   jax.ShapeDtypeStruct((B, MLA_HEADS * MLA_V_DIM, S), jnp.bfloat16),
                   jax.ShapeDtypeStruct((N, MEM_HEADS * MEM_HEAD_DIM), jnp.bfloat16)],
        compiler_params=_cparams("arbitrary"),
        name="in_proj",
    )(x2, ln_g, ln_b, wa, gq, gk, gql, gkvl, wqb, wkvk, wkvv, ca, sa, cb, sb, mkt, mv)


def _attention_kernel(qt_ref, k_ref, vt_ref, o_ref, s_even, s_odd, *, shared_kv):
    t = pl.program_id(0)

    @pl.when(t == 0)
    def _():
        s_odd[...] = jnp.zeros_like(s_odd)

    def step(s_new, s_old):
        S, TQ = s_new.shape[1:]
        m = [jnp.full((1, TQ), NEG_INF, jnp.float32)] * 2
        ones = jnp.ones((BF16_SUBLANES, ATTN_K_CHUNK), jnp.bfloat16)
        acc = [jnp.zeros((HALF + BF16_SUBLANES, TQ), jnp.float32)] * 2
        for c in range(S // ATTN_K_CHUNK):
            rows = slice(c * ATTN_K_CHUNK, (c + 1) * ATTN_K_CHUNK)
            for hd in range(2):
                qt = qt_ref[0, hd * LANES:(hd + 1) * LANES, :]
                k = k_ref[0, rows, :] if shared_kv else k_ref[0, rows, hd * LANES:(hd + 1) * LANES]
                s_new[hd, rows, :] = jnp.dot(k, qt, preferred_element_type=jnp.float32)
                vt = vt_ref[0, :, rows] if shared_kv else vt_ref[0, hd * HALF:(hd + 1) * HALF, rows]
                vt = jnp.concatenate([vt, ones], axis=0)
                sc = s_old[hd, rows, :]
                m_new = jnp.maximum(m[hd], jnp.max(sc, axis=0, keepdims=True))
                p = jnp.exp2(sc - m_new).astype(jnp.bfloat16)
                corr = jnp.exp2(m[hd] - m_new)
                acc[hd] = acc[hd] * corr + jnp.dot(vt, p, preferred_element_type=jnp.float32)
                m[hd] = m_new
        outs = [acc[hd][:HALF] / acc[hd][HALF:HALF + 1] for hd in range(2)]
        o_ref[0] = jnp.concatenate(outs, axis=0).T.astype(jnp.bfloat16)

    @pl.when(t % 2 == 0)
    def _():
        step(s_even, s_odd)

    @pl.when(t % 2 == 1)
    def _():
        step(s_odd, s_even)


def _attention(qt, k, vt, *, shared_kv):
    B, qh, S = qt.shape
    pairs = qh // (2 * LANES)
    TQ = ATTN_Q_TILE
    nq = S // TQ
    items = B * pairs * nq
    pairs_per_group = pairs // GQA_KV_HEADS

    def item(t):
        return t // (pairs * nq), (t // nq) % pairs, t % nq

    score_item = lambda t: item(jnp.minimum(t, items - 1))
    finish_item = lambda t: item(jnp.maximum(t - 1, 0))

    def qt_map(t):
        b, j, i = score_item(t)
        return b, j, i

    def k_map(t):
        b, j, _ = score_item(t)
        return (b, 0, 0) if shared_kv else (b, 0, j)

    def vt_map(t):
        b, j, _ = finish_item(t)
        return (b, j // pairs_per_group, 0) if shared_kv else (b, j, 0)

    def o_map(t):
        b, j, i = finish_item(t)
        return b, i, j

    k_spec = pl.BlockSpec((1, S, LANES if shared_kv else 2 * LANES), k_map)
    vt_spec = pl.BlockSpec((1, HALF if shared_kv else LANES, S), vt_map)
    return pl.pallas_call(
        functools.partial(_attention_kernel, shared_kv=shared_kv),
        grid=(items + 1,),
        in_specs=[pl.BlockSpec((1, 2 * LANES, TQ), qt_map), k_spec, vt_spec],
        out_specs=pl.BlockSpec((1, TQ, LANES), o_map),
        out_shape=jax.ShapeDtypeStruct((B, S, pairs * LANES), jnp.bfloat16),
        scratch_shapes=[pltpu.VMEM((2, S, TQ), jnp.float32), pltpu.VMEM((2, S, TQ), jnp.float32)],
        compiler_params=_cparams("arbitrary"),
        name="attention_gqa" if shared_kv else "attention_mla",
    )(qt, k, vt)


def _split2(x):
    hi = x.astype(jnp.bfloat16)
    return hi, (x - hi.astype(jnp.float32)).astype(jnp.bfloat16)


def _mix_out_kernel(h_ref, oa_ref, ob_ref, oc_ref, wg_ref, bg_ref, wa_ref, wb_ref, wc_ref, wo_ref,
                    g_ref, b_ref, wr_ref, br_ref, h1_ref, idx_ref, gate_ref):
    h = h_ref[...]
    hb = h.astype(jnp.bfloat16)
    D = D_MODEL
    dot = lambda a, b: jnp.dot(a, b, preferred_element_type=jnp.float32)
    merged = []
    for n in range(D // MIX_COLS):
        acc = None
        for i, (o_ref, w_ref) in enumerate(((oa_ref, wa_ref), (ob_ref, wb_ref), (oc_ref, wc_ref))):
            gate_cols = slice(i * D + n * MIX_COLS, i * D + (n + 1) * MIX_COLS)
            gate = jax.nn.sigmoid(dot(hb, wg_ref[:, gate_cols]) + bg_ref[:, gate_cols])
            term = gate * dot(o_ref[...], w_ref[:, n * MIX_COLS:(n + 1) * MIX_COLS])
            acc = term if acc is None else acc + term
        merged.append(acc.astype(jnp.bfloat16))
    mix = dot(jnp.concatenate(merged, axis=1), wo_ref[...])
    h1 = _layer_norm(DEEPNORM_ALPHA * h + mix, g_ref[...], b_ref[...])
    h1_ref[...] = h1

    h_hi, h_lo = _split2(h1)
    w_hi, w_lo = wr_ref[0], wr_ref[1]
    dot = lambda a, b: jnp.dot(a, b, preferred_element_type=jnp.float32)
    logits = dot(h_hi, w_hi) + (dot(h_hi, w_lo) + dot(h_lo, w_hi)) + br_ref[...]
    lane = _lane_iota(logits.shape)
    logits = jnp.where(lane < N_EXPERTS, logits, NEG_INF)

    idx_out = jnp.zeros(logits.shape, jnp.int32)
    val_out = jnp.zeros(logits.shape, jnp.float32)
    top = None
    for k in range(TOP_K):
        m = jnp.max(logits, axis=-1, keepdims=True)
        idx = jnp.min(jnp.where(logits == m, lane, LANES), axis=-1, keepdims=True)
        logits = jnp.where(lane == idx, NEG_INF, logits)
        top = m if top is None else top
        idx_out = jnp.where(lane == k, idx, idx_out)
        val_out = jnp.where(lane == k, jnp.exp(m - top), val_out)
    idx_ref[...] = idx_out
    gate_ref[...] = val_out / jnp.sum(val_out, axis=-1, keepdims=True)


def _mix_out(h, oa, ob, oc, wg, bg, wa, wb, wc, wo, g, b, wr3, br):
    N, D = h.shape
    T = MIX_TILE
    row = lambda i: (i, 0)
    full = lambda a: pl.BlockSpec(a.shape, lambda i: (0,) * a.ndim)
    return pl.pallas_call(
        _mix_out_kernel,
        grid=(N // T,),
        in_specs=[pl.BlockSpec((T, D), row), pl.BlockSpec((T, oa.shape[1]), row),
                  pl.BlockSpec((T, ob.shape[1]), row), pl.BlockSpec((T, oc.shape[1]), row),
                  full(wg), full(bg), full(wa), full(wb), full(wc), full(wo), full(g), full(b),
                  full(wr3), full(br)],
        out_specs=[pl.BlockSpec((T, D), row), pl.BlockSpec((T, LANES), row), pl.BlockSpec((T, LANES), row)],
        out_shape=[jax.ShapeDtypeStruct((N, D), jnp.float32),
                   jax.ShapeDtypeStruct((N, LANES), jnp.int32),
                   jax.ShapeDtypeStruct((N, LANES), jnp.float32)],
        compiler_params=_cparams("arbitrary"),
        name="mix_out",
    )(h, oa, ob, oc, wg, bg, wa, wb, wc, wo, g, b, wr3, br)


def _route_kernel(idx_ref, pos_ref, carry_out_ref, cnt_out_ref, total_ref, carry_ref):
    @pl.when(pl.program_id(0) == 0)
    def _():
        carry_ref[...] = jnp.zeros_like(carry_ref)

    idx = idx_ref[...]
    T = idx.shape[0]
    lane = _lane_iota(idx.shape)
    sel = [lane == idx[:, k:k + 1] for k in range(TOP_K)]
    onehot = sum(s.astype(jnp.float32) for s in sel)
    r = lax.broadcasted_iota(jnp.int32, (T, T), 0)
    c = lax.broadcasted_iota(jnp.int32, (T, T), 1)
    before = (c < r).astype(jnp.bfloat16)
    prefix = jnp.dot(before, onehot.astype(jnp.bfloat16), preferred_element_type=jnp.float32)
    cnt = jnp.broadcast_to(jnp.sum(onehot, axis=0, keepdims=True), carry_ref.shape)
    cnt = jnp.floor((cnt + (SUBLANES - 1)) * (1.0 / SUBLANES)) * SUBLANES
    er = lax.broadcasted_iota(jnp.int32, (LANES, LANES), 0)
    ec = lax.broadcasted_iota(jnp.int32, (LANES, LANES), 1)
    local_start = jnp.dot(cnt.astype(jnp.bfloat16), (er < ec).astype(jnp.bfloat16),
                          preferred_element_type=jnp.float32)
    target = prefix + local_start[0:1, :]
    out = jnp.zeros(idx.shape, jnp.int32)
    for k in range(TOP_K):
        pk = jnp.sum(jnp.where(sel[k], target, 0.0), axis=-1, keepdims=True)
        out = jnp.where(lane == k, pk.astype(jnp.int32), out)
    pos_ref[...] = out
    carry_out_ref[...] = carry_ref[...].astype(jnp.int32)
    cnt_out_ref[...] = cnt.astype(jnp.int32)
    carry_ref[...] = carry_ref[...] + cnt
    total_ref[...] = carry_ref[...].astype(jnp.int32)


def _route(idx):
    N = idx.shape[0]
    T = ROW_TILE
    n_tiles = N // T
    tile8 = pl.BlockSpec((8, LANES), lambda i: (i, 0))
    return pl.pallas_call(
        _route_kernel,
        grid=(n_tiles,),
        in_specs=[pl.BlockSpec((T, LANES), lambda i: (i, 0))],
        out_specs=[pl.BlockSpec((T, LANES), lambda i: (i, 0)), tile8, tile8,
                   pl.BlockSpec((8, LANES), lambda i: (0, 0))],
        out_shape=[jax.ShapeDtypeStruct((N, LANES), jnp.int32),
                   jax.ShapeDtypeStruct((n_tiles * 8, LANES), jnp.int32),
                   jax.ShapeDtypeStruct((n_tiles * 8, LANES), jnp.int32),
                   jax.ShapeDtypeStruct((8, LANES), jnp.int32)],
        scratch_shapes=[pltpu.VMEM((8, LANES), jnp.float32)],
        compiler_params=_cparams("arbitrary"),
        name="route",
    )(idx)


STRIP_SIZES = tuple(ROW_TILE >> s for s in range((ROW_TILE // SUBLANES).bit_length()))
LOCAL_ROWS = TOP_K * ROW_TILE + ROW_TILE
STRIP_BUFFERS = 3
assert LOCAL_ROWS >= TOP_K * ROW_TILE + N_EXPERTS * (SUBLANES - 1)


HI16 = 0xFFFF0000


def _pack_rows(x):
    half = x.shape[1] // 2
    lo = lax.bitcast_convert_type(x[:, :half], jnp.uint32) >> 16
    hi = lax.bitcast_convert_type(x[:, half:], jnp.uint32) & jnp.uint32(HI16)
    return lo | hi


def _unpack_rows(w):
    lo = lax.bitcast_convert_type(w << 16, jnp.float32)
    hi = lax.bitcast_convert_type(w & jnp.uint32(HI16), jnp.float32)
    return jnp.concatenate([lo, hi], axis=1).astype(jnp.bfloat16)


def _round_bf16(x):
    return x.astype(jnp.bfloat16).astype(jnp.float32)


def _for_each_strip(cnt_ref, base_ref, tile, fn):
    def body(e, off):
        c = cnt_ref[tile * N_EXPERTS + e]
        d = base_ref[tile * N_EXPERTS + e]
        for bit, size in enumerate(STRIP_SIZES):
            above = c & (-2 * size)

            @pl.when((c & size) != 0)
            def _():
                fn(pl.multiple_of(off + above, SUBLANES), pl.multiple_of(d + above, SUBLANES), size, bit)
        return off + c

    lax.fori_loop(0, N_EXPERTS, body, 0)


def _dispatch_kernel(cnt_ref, base_ref, tail_ref, pos_ref, h_ref, xs_ref, buf, zeros, sems, zsem):
    i = pl.program_id(0)
    last = pl.num_programs(0) - 1
    T = h_ref.shape[0]
    cur = i % STRIP_BUFFERS

    @pl.when(i == 0)
    def _():
        zeros[...] = jnp.zeros_like(zeros)

        def tail_copy(e):
            row = pl.multiple_of(jnp.maximum(tail_ref[e], 0), SUBLANES)
            return pltpu.make_async_copy(zeros, xs_ref.at[pl.ds(row, EXPERT_BLOCK), :], zsem)

        def spare_copy(blk):
            row = pl.multiple_of(blk * EXPERT_BLOCK, EXPERT_BLOCK)
            return pltpu.make_async_copy(zeros, xs_ref.at[pl.ds(row, EXPERT_BLOCK), :], zsem)

        first_spare = tail_ref[N_EXPERTS] // EXPERT_BLOCK
        for act in ("start", "wait"):
            def body(e, carry, act=act):
                @pl.when(tail_ref[e] >= 0)
                def _():
                    getattr(tail_copy(e), act)()
                return carry
            lax.fori_loop(0, N_EXPERTS, body, 0)

            def spare(blk, carry, act=act):
                getattr(spare_copy(blk), act)()
                return carry
            lax.fori_loop(first_spare, xs_ref.shape[0] // EXPERT_BLOCK, spare, 0)

    pos_t = pos_ref[...].astype(jnp.float32).T.astype(jnp.int32)
    rows = lax.broadcasted_iota(jnp.int32, (LOCAL_ROWS, T), 0)
    hit = rows == pos_t[0:1, :]
    for k in range(1, TOP_K):
        hit = jnp.logical_or(hit, rows == pos_t[k:k + 1, :])
    buf[cur] = _pack_rows(jnp.dot(hit.astype(jnp.bfloat16), h_ref[...].astype(jnp.bfloat16),
                                  preferred_element_type=jnp.float32))

    def strip(tile):
        slot = tile % STRIP_BUFFERS

        def make(local_row, slot_row, size, bit):
            return pltpu.make_async_copy(buf.at[slot, pl.ds(local_row, size), :],
                                         xs_ref.at[pl.ds(slot_row, size), :], sems.at[slot, bit])
        return make

    def drain(tile):
        _for_each_strip(cnt_ref, base_ref, tile, lambda *a: strip(tile)(*a).wait())

    @pl.when(i >= STRIP_BUFFERS - 1)
    def _():
        drain(i - (STRIP_BUFFERS - 1))

    _for_each_strip(cnt_ref, base_ref, i, lambda *a: strip(i)(*a).start())

    @pl.when(i == last)
    def _():
        for back in range(STRIP_BUFFERS - 2, -1, -1):
            @pl.when(i - back >= 0)
            def _(back=back):
                drain(i - back)


def _dispatch(cnt_flat, base_flat, tail, pos, h1, n_slots):
    N, D = h1.shape
    T = ROW_TILE
    return pl.pallas_call(
        _dispatch_kernel,
        grid_spec=pltpu.PrefetchScalarGridSpec(
            num_scalar_prefetch=3,
            grid=(N // T,),
            in_specs=[pl.BlockSpec((T, LANES), lambda i, c, b, t: (i, 0)),
                      pl.BlockSpec((T, D), lambda i, c, b, t: (i, 0))],
            out_specs=pl.BlockSpec(memory_space=pl.ANY),
            scratch_shapes=[pltpu.VMEM((STRIP_BUFFERS, LOCAL_ROWS, D // 2), jnp.uint32),
                            pltpu.VMEM((EXPERT_BLOCK, D // 2), jnp.uint32),
                            pltpu.SemaphoreType.DMA((STRIP_BUFFERS, len(STRIP_SIZES))),
                            pltpu.SemaphoreType.DMA(())]),
        out_shape=jax.ShapeDtypeStruct((n_slots, D // 2), jnp.uint32),
        compiler_params=_cparams("arbitrary"),
        name="dispatch",
    )(cnt_flat, base_flat, tail, pos, h1)


def _experts_kernel(blk_e_ref, n_used_ref, x_ref, wi_ref, bi_ref, wo_ref, bo_ref, y_ref, wi_bf, wo_bf):
    i = pl.program_id(0)
    used = i < n_used_ref[0]
    new_expert = jnp.logical_or(i == 0, blk_e_ref[i] != blk_e_ref[jnp.maximum(i - 1, 0)])

    @pl.when(jnp.logical_and(used, new_expert))
    def _():
        wi_bf[...] = wi_ref[0].astype(jnp.bfloat16)
        wo_bf[...] = wo_ref[0].astype(jnp.bfloat16)

    @pl.when(used)
    def _():
        hb = jnp.dot(_unpack_rows(x_ref[...]), wi_bf[...], preferred_element_type=jnp.float32) + bi_ref[0]
        De = D_MODEL
        x_glu = jnp.minimum(hb[:, :De], SWIGLU_LIMIT)
        x_lin = jnp.clip(hb[:, De:], -SWIGLU_LIMIT, SWIGLU_LIMIT)
        act = x_glu * jax.nn.sigmoid(SWIGLU_ALPHA * x_glu) * (x_lin + 1.0)
        y = jnp.dot(act.astype(jnp.bfloat16), wo_bf[...], preferred_element_type=jnp.float32) + bo_ref[0]
        y_ref[...] = _pack_rows(_round_bf16(y))

    @pl.when(jnp.logical_not(used))
    def _():
        y_ref[...] = jnp.zeros_like(y_ref)


def _experts(blk_e, n_used, xs, w_in, b_in, w_out, b_out):
    n_slots, half = xs.shape
    R = EXPERT_BLOCK
    E, D, F = w_in.shape
    return pl.pallas_call(
        _experts_kernel,
        grid_spec=pltpu.PrefetchScalarGridSpec(
            num_scalar_prefetch=2,
            grid=(n_slots // R,),
            in_specs=[pl.BlockSpec((R, half), lambda i, e, n: (jnp.minimum(i, n[0] - 1), 0)),
                      pl.BlockSpec((1, D, F), lambda i, e, n: (e[i], 0, 0)),
                      pl.BlockSpec((1, 1, F), lambda i, e, n: (e[i], 0, 0)),
                      pl.BlockSpec((1, F // 2, D), lambda i, e, n: (e[i], 0, 0)),
                      pl.BlockSpec((1, 1, D), lambda i, e, n: (e[i], 0, 0))],
            out_specs=pl.BlockSpec((R, half), lambda i, e, n: (i, 0)),
            scratch_shapes=[pltpu.VMEM((D, F), jnp.bfloat16), pltpu.VMEM((F // 2, D), jnp.bfloat16)]),
        out_shape=jax.ShapeDtypeStruct((n_slots, half), jnp.uint32),
        compiler_params=pltpu.CompilerParams(dimension_semantics=("arbitrary",),
                                             vmem_limit_bytes=EXPERTS_VMEM_LIMIT),
        name="experts",
    )(blk_e, n_used, xs, w_in, b_in, w_out, b_out)


def _combine_kernel(cnt_ref, base_ref, pos_ref, gate_ref, h_ref, g_ref, b_ref, ys_ref, o_ref, buf, sems):
    i = pl.program_id(0)
    last = pl.num_programs(0) - 1
    T = h_ref.shape[0]
    cur = i % STRIP_BUFFERS

    def strip(tile):
        slot = tile % STRIP_BUFFERS

        def make(local_row, slot_row, size, bit):
            return pltpu.make_async_copy(ys_ref.at[pl.ds(slot_row, size), :],
                                         buf.at[slot, pl.ds(local_row, size), :], sems.at[slot, bit])
        return make

    def fetch(tile):
        _for_each_strip(cnt_ref, base_ref, tile, lambda *a: strip(tile)(*a).start())

    @pl.when(i == 0)
    def _():
        buf[...] = jnp.zeros_like(buf)
        for ahead in range(STRIP_BUFFERS - 1):
            @pl.when(ahead <= last)
            def _(ahead=ahead):
                fetch(ahead)

    @pl.when(i + (STRIP_BUFFERS - 1) <= last)
    def _():
        fetch(i + (STRIP_BUFFERS - 1))

    _for_each_strip(cnt_ref, base_ref, i, lambda *a: strip(i)(*a).wait())

    pos, gate = pos_ref[...], gate_ref[...]
    col = lax.broadcasted_iota(jnp.int32, (T, LOCAL_ROWS), 1)
    u = jnp.zeros(col.shape, jnp.float32)
    for k in range(TOP_K):
        u = jnp.where(col == pos[:, k:k + 1], gate[:, k:k + 1], u)
    u_hi, u_lo = _split2(u)
    y = _unpack_rows(buf[cur])
    dot = lambda a, b: jnp.dot(a, b, preferred_element_type=jnp.float32)
    ffn = dot(u_hi, y) + dot(u_lo, y)
    o_ref[...] = _layer_norm(DEEPNORM_ALPHA * h_ref[...] + ffn, g_ref[...], b_ref[...])


def _combine(cnt_flat, base_flat, pos, gate, h1, g, b, ys):
    N, D = h1.shape
    T = ROW_TILE
    row = lambda i, c, s: (i, 0)
    const = lambda i, c, s: (0, 0)
    return pl.pallas_call(
        _combine_kernel,
        grid_spec=pltpu.PrefetchScalarGridSpec(
            num_scalar_prefetch=2,
            grid=(N // T,),
            in_specs=[pl.BlockSpec((T, LANES), row), pl.BlockSpec((T, LANES), row), pl.BlockSpec((T, D), row),
                      pl.BlockSpec(g.shape, const), pl.BlockSpec(b.shape, const),
                      pl.BlockSpec(memory_space=pl.ANY)],
            out_specs=pl.BlockSpec((T, D), row),
            scratch_shapes=[pltpu.VMEM((STRIP_BUFFERS, LOCAL_ROWS, D // 2), jnp.uint32),
                            pltpu.SemaphoreType.DMA((STRIP_BUFFERS, len(STRIP_SIZES)))]),
        out_shape=jax.ShapeDtypeStruct((N, D), jnp.float32),
        compiler_params=_cparams("arbitrary"),
        name="combine",
    )(cnt_flat, base_flat, pos, gate, h1, g, b, ys)


def _rope_tables(S):
    t = np.arange(S)
    row, col = (t // GRID_W).astype(np.float64), (t % GRID_W).astype(np.float64)

    def block(half):
        inv = ROPE_THETA ** (-np.arange(half, dtype=np.float64) / half)
        ar, ac = row[:, None] * inv[None, :], col[:, None] * inv[None, :]
        cos = np.concatenate([np.cos(ar), np.cos(ar), np.cos(ac), np.cos(ac)], axis=1)
        sin = np.concatenate([-np.sin(ar), np.sin(ar), -np.sin(ac), np.sin(ac)], axis=1)
        return cos, sin

    ca64, sa64 = block(GQA_HEAD_DIM // 4)
    ca, sa = np.tile(ca64, (1, 2)), np.tile(sa64, (1, 2))
    cb32, sb32 = block(MLA_ROPE_DIM // 4)
    ones, zeros = np.ones((S, MLA_NOPE_DIM)), np.zeros((S, MLA_NOPE_DIM))
    cb = np.concatenate([ones, cb32, ones[:, :LANES - MLA_NOPE_DIM - MLA_ROPE_DIM]], axis=1)
    sb = np.concatenate([zeros, sb32, zeros[:, :LANES - MLA_NOPE_DIM - MLA_ROPE_DIM]], axis=1)
    return tuple(jnp.asarray(a, jnp.float32) for a in (ca, sa, cb, sb))


def _layer(h_in_is_x, x2, mem, B, S, ln_g, ln_b, w_in_proj, b_gate, gqa_q_norm, gqa_k_norm, mla_q_norm,
           mla_kv_norm, w_mla_qb, w_mla_kvb, w_mem_kv, w_br_gqa, w_br_mla, w_br_mem, w_out,
           ln1_g, ln1_b, w_router, b_router, w_exp_in, b_exp_in, w_exp_out, b_exp_out, ln2_g, ln2_b):
    del h_in_is_x
    bf = jnp.bfloat16
    N, D = x2.shape
    row2 = lambda a: a.reshape(1, -1)

    W = w_in_proj
    zc = lambda n: jnp.zeros((D, n), W.dtype)
    wa = jnp.concatenate([W[:, :OFF_KROPE], zc(HALF), W[:, OFF_KROPE:OFF_QM],
                          zc(LANES - HALF - MLA_ROPE_DIM), W[:, OFF_QM:OFF_GATE]], axis=1).astype(bf)
    wg = W[:, OFF_GATE:].astype(bf)
    qd = MLA_NOPE_DIM + MLA_ROPE_DIM
    wqb = jnp.pad(w_mla_qb.reshape(MLA_Q_LORA, MLA_HEADS, qd),
                  ((0, 0), (0, 0), (0, LANES - qd))).reshape(MLA_Q_LORA, MLA_HEADS * LANES).astype(bf)
    kvb = w_mla_kvb.reshape(MLA_KV_LORA, MLA_HEADS, MLA_NOPE_DIM + MLA_V_DIM)
    wkvk = jnp.pad(kvb[:, :, :MLA_NOPE_DIM], ((0, 0), (0, 0), (0, LANES - MLA_NOPE_DIM))
                   ).reshape(MLA_KV_LORA, MLA_HEADS * LANES).astype(bf)
    wkvv = kvb[:, :, MLA_NOPE_DIM:].reshape(MLA_KV_LORA, MLA_HEADS * MLA_V_DIM).astype(bf)
    gq = row2(jnp.tile(gqa_q_norm, 2))
    gk = row2(jnp.tile(gqa_k_norm, 2))
    ca, sa, cb, sb = _rope_tables(S)

    mkt, mv = _mem_kv(mem, w_mem_kv.astype(bf))
    h, qat, ka, vat, qbt, kb, vbt, oc = _in_proj(
        x2, row2(ln_g), row2(ln_b), wa, gq, gk, row2(mla_q_norm), row2(mla_kv_norm),
        wqb, wkvk, wkvv, ca, sa, cb, sb, mkt, mv, B, S)

    oa = _attention(qat, ka.reshape(B, S, -1), vat, shared_kv=True)
    ob = _attention(qbt, kb.reshape(B, S, -1), vbt, shared_kv=False)

    wr = jnp.pad(w_router, ((0, 0), (0, LANES - N_EXPERTS)))
    wr3 = jnp.stack(_split2(wr))
    br = row2(jnp.pad(b_router, (0, LANES - N_EXPERTS)))
    h1, idx, gate = _mix_out(h, oa.reshape(N, -1), ob.reshape(N, -1), oc, wg, row2(b_gate),
                             w_br_gqa.astype(bf), w_br_mla.astype(bf), w_br_mem.astype(bf),
                             w_out.astype(bf), row2(ln1_g), row2(ln1_b), wr3, br)

    pos, carry_t, cnt_t, total = _route(idx)
    counts = total[0, :N_EXPERTS]
    R = EXPERT_BLOCK
    padded = (counts + R - 1) // R * R
    pad_end = jnp.cumsum(padded)
    pad_start = (pad_end - padded).astype(jnp.int32)
    n_tiles = N // ROW_TILE
    max_rows = N * TOP_K + n_tiles * N_EXPERTS * (SUBLANES - 1) + N_EXPERTS * (R - 1)
    n_blocks = -(-max_rows // R)
    blk_start = jnp.arange(n_blocks, dtype=jnp.int32) * R
    blk_e = jnp.sum((pad_end[None, :] <= blk_start[:, None]).astype(jnp.int32), axis=1)
    n_used = (pad_end[-1:] // R).astype(jnp.int32)
    last_e = jnp.max(jnp.where(blk_start < pad_end[-1], blk_e, 0))
    blk_e = jnp.minimum(blk_e, last_e).astype(jnp.int32)
    per_tile = lambda a: a.reshape(n_tiles, 8, LANES)[:, 0, :N_EXPERTS]
    cnt_flat = per_tile(cnt_t).reshape(-1)
    base_flat = (per_tile(carry_t) + pad_start[None, :]).reshape(-1)
    tail = jnp.concatenate([jnp.where(padded > 0, pad_end - R, -1), pad_end[-1:]]).astype(jnp.int32)

    xs = _dispatch(cnt_flat, base_flat, tail, pos, h1, n_blocks * R)
    ys = _experts(blk_e, n_used, xs, w_exp_in, b_exp_in[:, None, :], w_exp_out, b_exp_out[:, None, :])
    return _combine(cnt_flat, base_flat, pos, gate, h1, row2(ln2_g), row2(ln2_b), ys)


def kernel(x, mem, ln_in_g, ln_in_b, w_in_proj, b_gate, gqa_q_norm, gqa_k_norm, mla_q_norm, mla_kv_norm, w_mla_qb, w_mla_kvb, w_mem_kv, w_br_gqa, w_br_mla, w_br_mem, w_out, ln1_g, ln1_b, w_router, b_router, w_exp_in, b_exp_in, w_exp_out, b_exp_out, ln2_g, ln2_b):
    B, S, D = x.shape
    depth = w_in_proj.shape[0]
    assert depth == 1, "the input LayerNorm is fused into the first (only) layer's projection kernel"
    out = _layer(True, x.reshape(B * S, D), mem, B, S, ln_in_g, ln_in_b, w_in_proj[0], b_gate[0],
                 gqa_q_norm[0], gqa_k_norm[0], mla_q_norm[0], mla_kv_norm[0], w_mla_qb[0], w_mla_kvb[0],
                 w_mem_kv[0], w_br_gqa[0], w_br_mla[0], w_br_mem[0], w_out[0], ln1_g[0], ln1_b[0],
                 w_router[0], b_router[0], w_exp_in[0], b_exp_in[0], w_exp_out[0], b_exp_out[0],
                 ln2_g[0], ln2_b[0])
    return out.reshape(B, S, D)
```

```python
import functools

import jax
import jax.numpy as jnp
import numpy as np
from jax import lax
from jax.experimental import pallas as pl
from jax.experimental.pallas import tpu as pltpu

D_MODEL = 1024
MEM_LEN = 256
GRID_W = 64
ROPE_THETA = 10000.0
RMS_EPS = 1e-6
LN_EPS = 1e-5

GQA_HEADS = 8
GQA_KV_HEADS = 2
GQA_HEAD_DIM = 64
MLA_HEADS = 8
MLA_NOPE_DIM = 64
MLA_ROPE_DIM = 32
MLA_V_DIM = 64
MLA_Q_LORA = 384
MLA_KV_LORA = 256
MEM_HEADS = 4
MEM_HEAD_DIM = 128

N_EXPERTS = 32
TOP_K = 4
SWIGLU_LIMIT = 7.0
SWIGLU_ALPHA = 1.702
DEEPNORM_ALPHA = 2.0 ** 0.25

LANES = 128
SUBLANES = 8
BF16_SUBLANES = 16
HALF = LANES // 2
NEG_INF = float("-inf")
LOG2_E = 1.4426950408889634

OFF_QA, OFF_KA, OFF_VA, OFF_QLAT, OFF_KVLAT, OFF_KROPE, OFF_QM, OFF_GATE = (
    0, 512, 640, 768, 1152, 1408, 1440, 1952)
IN_PROJ_W = 5024
A_QA, A_KA, A_VA, A_QLAT, A_KVLAT, A_KROPE, A_QM, A_END = 0, 512, 640, 768, 1152, 1408, 1536, 2048

ROW_TILE = 256
MIX_TILE = 512
MIX_COLS = 256
ATTN_Q_TILE = 256
ATTN_K_CHUNK = 512
EXPERT_BLOCK = 256
VMEM_LIMIT = 48 * 1024 * 1024
EXPERTS_VMEM_LIMIT = 56 * 1024 * 1024


def _cparams(*sem):
    return pltpu.CompilerParams(dimension_semantics=sem, vmem_limit_bytes=VMEM_LIMIT)


def _lane_iota(shape):
    return lax.broadcasted_iota(jnp.int32, shape, len(shape) - 1)


def _layer_norm(x, g, b):
    mu = jnp.mean(x, axis=-1, keepdims=True)
    xc = x - mu
    var = jnp.mean(xc * xc, axis=-1, keepdims=True)
    return xc * lax.rsqrt(var + LN_EPS) * g + b


def _bdot(a, b):
    return jnp.dot(a.astype(jnp.bfloat16), b.astype(jnp.bfloat16), preferred_element_type=jnp.float32)


def _rope(x, cos, sin_signed, pair):
    lane = _lane_iota(x.shape)
    fwd = pltpu.roll(x, LANES - pair, 1)
    bwd = pltpu.roll(x, pair, 1)
    partner = jnp.where((lane % (2 * pair)) < pair, fwd, bwd)
    return x * cos + partner * sin_signed


def _half_rms_scale(x):
    lane = _lane_iota(x.shape)
    sq = x * x
    lo = jnp.sum(jnp.where(lane < HALF, sq, 0.0), axis=-1, keepdims=True)
    hi = jnp.sum(jnp.where(lane >= HALF, sq, 0.0), axis=-1, keepdims=True)
    inv = 1.0 / GQA_HEAD_DIM
    return jnp.where(lane < HALF, lax.rsqrt(lo * inv + RMS_EPS), lax.rsqrt(hi * inv + RMS_EPS))


def _softmax_rows(s):
    m = jnp.max(s, axis=-1, keepdims=True)
    p = jnp.exp(s - m)
    return p, jnp.sum(p, axis=-1, keepdims=True)


def _mem_kv_kernel(mem_ref, w_ref, kt_ref, v_ref):
    kv = _bdot(mem_ref[0], w_ref[...])
    width = MEM_HEADS * MEM_HEAD_DIM
    kt_ref[0] = kv[:, :width].T.astype(jnp.bfloat16)
    v_ref[0] = kv[:, width:].astype(jnp.bfloat16)


def _mem_kv(mem, w_mem_kv):
    B, M, D = mem.shape
    width = MEM_HEADS * MEM_HEAD_DIM
    return pl.pallas_call(
        _mem_kv_kernel,
        grid=(B,),
        in_specs=[pl.BlockSpec((1, M, D), lambda b: (b, 0, 0)),
                  pl.BlockSpec((D, 2 * width), lambda b: (0, 0))],
        out_specs=[pl.BlockSpec((1, width, M), lambda b: (b, 0, 0)),
                   pl.BlockSpec((1, M, width), lambda b: (b, 0, 0))],
        out_shape=[jax.ShapeDtypeStruct((B, width, M), jnp.bfloat16),
                   jax.ShapeDtypeStruct((B, M, width), jnp.bfloat16)],
        compiler_params=_cparams("arbitrary"),
        name="mem_kv",
    )(mem, w_mem_kv)


def _in_proj_kernel(x_ref, lng_ref, lnb_ref, wa_ref, gq_ref, gk_ref, gql_ref, gkvl_ref,
                    wqb_ref, wkvk_ref, wkvv_ref, ca_ref, sa_ref, cb_ref, sb_ref, mkt_ref, mv_ref,
                    h_ref, qa_ref, ka_ref, va_ref, qb_ref, kb_ref, vb_ref, oc_ref):
    h = _layer_norm(x_ref[...], lng_ref[...], lnb_ref[...])
    h_ref[...] = h
    proj = _bdot(h, wa_ref[...])
    lane = _lane_iota((h.shape[0], LANES))
    lo_mask = lane < HALF
    ca, sa, cb, sb = ca_ref[...], sa_ref[...], cb_ref[...], sb_ref[...]

    q_scale = GQA_HEAD_DIM ** -0.5 * LOG2_E
    per_group = GQA_HEADS // GQA_KV_HEADS
    for c in range(GQA_HEADS // 2):
        slab = proj[:, A_QA + c * LANES:A_QA + (c + 1) * LANES]
        slab = _rope(slab * _half_rms_scale(slab) * gq_ref[...], ca, sa, 16) * q_scale
        swapped = pltpu.roll(slab, HALF, 1)
        for half in range(2):
            head = 2 * c + half
            group_lo = (head // per_group) == 0
            src = slab if (half == 0) == group_lo else swapped
            keep = lo_mask if group_lo else jnp.logical_not(lo_mask)
            qa_ref[0, head * LANES:(head + 1) * LANES, :] = jnp.where(keep, src, 0.0).astype(jnp.bfloat16).T

    ka = proj[:, A_KA:A_KA + LANES]
    ka_ref[...] = _rope(ka * _half_rms_scale(ka) * gk_ref[...], ca, sa, 16).astype(jnp.bfloat16)
    va_ref[0] = proj[:, A_VA:A_VA + LANES].astype(jnp.bfloat16).T

    ql = proj[:, A_QLAT:A_KVLAT]
    ql = ql * lax.rsqrt(jnp.mean(ql * ql, axis=-1, keepdims=True) + RMS_EPS) * gql_ref[...]
    qm = _bdot(ql, wqb_ref[...])
    kvl = proj[:, A_KVLAT:A_KROPE]
    kvl = kvl * lax.rsqrt(jnp.mean(kvl * kvl, axis=-1, keepdims=True) + RMS_EPS) * gkvl_ref[...]
    kn = _bdot(kvl, wkvk_ref[...])
    vb = _bdot(kvl, wkvv_ref[...])
    for c in range(vb.shape[1] // LANES):
        sl = slice(c * LANES, (c + 1) * LANES)
        vb_ref[0, sl, :] = vb[:, sl].astype(jnp.bfloat16).T
    k_pe = _rope(proj[:, A_KROPE:A_QM], cb, sb, 8)
    qb_scale = (MLA_NOPE_DIM + MLA_ROPE_DIM) ** -0.5 * LOG2_E
    for hd in range(MLA_HEADS):
        sl = slice(hd * LANES, (hd + 1) * LANES)
        qb_ref[0, sl, :] = (_rope(qm[:, sl], cb, sb, 8) * qb_scale).astype(jnp.bfloat16).T
        kb_ref[:, sl] = (kn[:, sl] + k_pe).astype(jnp.bfloat16)

    qc = proj[:, A_QM:A_END] * (MEM_HEAD_DIM ** -0.5)
    for hd in range(MEM_HEADS):
        sl = slice(hd * MEM_HEAD_DIM, (hd + 1) * MEM_HEAD_DIM)
        p, l = _softmax_rows(_bdot(qc[:, sl], mkt_ref[0, sl, :]))
        oc_ref[:, sl] = (_bdot(p, mv_ref[0, :, sl]) / l).astype(jnp.bfloat16)


def _in_proj(x2, ln_g, ln_b, wa, gq, gk, gql, gkvl, wqb, wkvk, wkvv, ca, sa, cb, sb, mkt, mv, B, S):
    N, D = x2.shape
    T = ROW_TILE
    tiles_per_seq = S // T
    row = lambda i: (i, 0)
    const = lambda i: (0, 0)
    pos = lambda i: (i % tiles_per_seq, 0)
    batch3 = lambda i: (i // tiles_per_seq, 0, 0)
    kt_map = lambda i: (i // tiles_per_seq, 0, i % tiles_per_seq)
    full = lambda a: pl.BlockSpec(a.shape, const)
    return pl.pallas_call(
        _in_proj_kernel,
        grid=(N // T,),
        in_specs=[pl.BlockSpec((T, D), row), full(ln_g), full(ln_b), full(wa), full(gq), full(gk),
                  full(gql), full(gkvl), full(wqb), full(wkvk), full(wkvv),
                  pl.BlockSpec((T, LANES), pos), pl.BlockSpec((T, LANES), pos),
                  pl.BlockSpec((T, LANES), pos), pl.BlockSpec((T, LANES), pos),
                  pl.BlockSpec((1,) + mkt.shape[1:], batch3), pl.BlockSpec((1,) + mv.shape[1:], batch3)],
        out_specs=[pl.BlockSpec((T, D), row),
                   pl.BlockSpec((1, GQA_HEADS * LANES, T), kt_map),
                   pl.BlockSpec((T, LANES), row),
                   pl.BlockSpec((1, LANES, T), kt_map),
                   pl.BlockSpec((1, MLA_HEADS * LANES, T), kt_map),
                   pl.BlockSpec((T, MLA_HEADS * LANES), row),
                   pl.BlockSpec((1, MLA_HEADS * MLA_V_DIM, T), kt_map),
                   pl.BlockSpec((T, MEM_HEADS * MEM_HEAD_DIM), row)],
        out_shape=[jax.ShapeDtypeStruct((N, D), jnp.float32),
                   jax.ShapeDtypeStruct((B, GQA_HEADS * LANES, S), jnp.bfloat16),
                   jax.ShapeDtypeStruct((N, LANES), jnp.bfloat16),
                   jax.ShapeDtypeStruct((B, LANES, S), jnp.bfloat16),
                   jax.ShapeDtypeStruct((B, MLA_HEADS * LANES, S), jnp.bfloat16),
                   jax.ShapeDtypeStruct((N, MLA_HEADS * LANES), jnp.bfloat16),
                   jax.ShapeDtypeStruct((B, MLA_HEADS * MLA_V_DIM, S), jnp.bfloat16),
                   jax.ShapeDtypeStruct((N, MEM_HEADS * MEM_HEAD_DIM), jnp.bfloat16)],
        compiler_params=_cparams("arbitrary"),
        name="in_proj",
    )(x2, ln_g, ln_b, wa, gq, gk, gql, gkvl, wqb, wkvk, wkvv, ca, sa, cb, sb, mkt, mv)


def _attention_kernel(qt_ref, k_ref, vt_ref, o_ref, s_even, s_odd, *, shared_kv):
    t = pl.program_id(0)

    @pl.when(t == 0)
    def _():
        s_odd[...] = jnp.zeros_like(s_odd)

    def step(s_new, s_old):
        S, TQ = s_new.shape[1:]
        m = [jnp.full((1, TQ), NEG_INF, jnp.float32)] * 2
        ones = jnp.ones((BF16_SUBLANES, ATTN_K_CHUNK), jnp.bfloat16)
        acc = [jnp.zeros((HALF + BF16_SUBLANES, TQ), jnp.float32)] * 2
        for c in range(S // ATTN_K_CHUNK):
            rows = slice(c * ATTN_K_CHUNK, (c + 1) * ATTN_K_CHUNK)
            for hd in range(2):
                qt = qt_ref[0, hd * LANES:(hd + 1) * LANES, :]
                k = k_ref[0, rows, :] if shared_kv else k_ref[0, rows, hd * LANES:(hd + 1) * LANES]
                s_new[hd, rows, :] = jnp.dot(k, qt, preferred_element_type=jnp.float32)
                vt = vt_ref[0, :, rows] if shared_kv else vt_ref[0, hd * HALF:(hd + 1) * HALF, rows]
                vt = jnp.concatenate([vt, ones], axis=0)
                sc = s_old[hd, rows, :]
                m_new = jnp.maximum(m[hd], jnp.max(sc, axis=0, keepdims=True))
                p = jnp.exp2(sc - m_new).astype(jnp.bfloat16)
                corr = jnp.exp2(m[hd] - m_new)
                acc[hd] = acc[hd] * corr + jnp.dot(vt, p, preferred_element_type=jnp.float32)
                m[hd] = m_new
        outs = [acc[hd][:HALF] / acc[hd][HALF:HALF + 1] for hd in range(2)]
        o_ref[0] = jnp.concatenate(outs, axis=0).T.astype(jnp.bfloat16)

    @pl.when(t % 2 == 0)
    def _():
        step(s_even, s_odd)

    @pl.when(t % 2 == 1)
    def _():
        step(s_odd, s_even)


def _attention(qt, k, vt, *, shared_kv):
    B, qh, S = qt.shape
    pairs = qh // (2 * LANES)
    TQ = ATTN_Q_TILE
    nq = S // TQ
    items = B * pairs * nq
    pairs_per_group = pairs // GQA_KV_HEADS

    def item(t):
        return t // (pairs * nq), (t // nq) % pairs, t % nq

    score_item = lambda t: item(jnp.minimum(t, items - 1))
    finish_item = lambda t: item(jnp.maximum(t - 1, 0))

    def qt_map(t):
        b, j, i = score_item(t)
        return b, j, i

    def k_map(t):
        b, j, _ = score_item(t)
        return (b, 0, 0) if shared_kv else (b, 0, j)

    def vt_map(t):
        b, j, _ = finish_item(t)
        return (b, j // pairs_per_group, 0) if shared_kv else (b, j, 0)

    def o_map(t):
        b, j, i = finish_item(t)
        return b, i, j

    k_spec = pl.BlockSpec((1, S, LANES if shared_kv else 2 * LANES), k_map)
    vt_spec = pl.BlockSpec((1, HALF if shared_kv else LANES, S), vt_map)
    return pl.pallas_call(
        functools.partial(_attention_kernel, shared_kv=shared_kv),
        grid=(items + 1,),
        in_specs=[pl.BlockSpec((1, 2 * LANES, TQ), qt_map), k_spec, vt_spec],
        out_specs=pl.BlockSpec((1, TQ, LANES), o_map),
        out_shape=jax.ShapeDtypeStruct((B, S, pairs * LANES), jnp.bfloat16),
        scratch_shapes=[pltpu.VMEM((2, S, TQ), jnp.float32), pltpu.VMEM((2, S, TQ), jnp.float32)],
        compiler_params=_cparams("arbitrary"),
        name="attention_gqa" if shared_kv else "attention_mla",
    )(qt, k, vt)


def _split2(x):
    hi = x.astype(jnp.bfloat16)
    return hi, (x - hi.astype(jnp.float32)).astype(jnp.bfloat16)


def _mix_out_kernel(h_ref, oa_ref, ob_ref, oc_ref, wg_ref, bg_ref, wa_ref, wb_ref, wc_ref, wo_ref,
                    g_ref, b_ref, wr_ref, br_ref, h1_ref, idx_ref, gate_ref):
    h = h_ref[...]
    hb = h.astype(jnp.bfloat16)
    D = D_MODEL
    dot = lambda a, b: jnp.dot(a, b, preferred_element_type=jnp.float32)
    merged = []
    for n in range(D // MIX_COLS):
        acc = None
        for i, (o_ref, w_ref) in enumerate(((oa_ref, wa_ref), (ob_ref, wb_ref), (oc_ref, wc_ref))):
            gate_cols = slice(i * D + n * MIX_COLS, i * D + (n + 1) * MIX_COLS)
            gate = jax.nn.sigmoid(dot(hb, wg_ref[:, gate_cols]) + bg_ref[:, gate_cols])
            term = gate * dot(o_ref[...], w_ref[:, n * MIX_COLS:(n + 1) * MIX_COLS])
            acc = term if acc is None else acc + term
        merged.append(acc.astype(jnp.bfloat16))
    mix = dot(jnp.concatenate(merged, axis=1), wo_ref[...])
    h1 = _layer_norm(DEEPNORM_ALPHA * h + mix, g_ref[...], b_ref[...])
    h1_ref[...] = h1

    h_hi, h_lo = _split2(h1)
    w_hi, w_lo = wr_ref[0], wr_ref[1]
    dot = lambda a, b: jnp.dot(a, b, preferred_element_type=jnp.float32)
    logits = dot(h_hi, w_hi) + (dot(h_hi, w_lo) + dot(h_lo, w_hi)) + br_ref[...]
    lane = _lane_iota(logits.shape)
    logits = jnp.where(lane < N_EXPERTS, logits, NEG_INF)

    idx_out = jnp.zeros(logits.shape, jnp.int32)
    val_out = jnp.zeros(logits.shape, jnp.float32)
    top = None
    for k in range(TOP_K):
        m = jnp.max(logits, axis=-1, keepdims=True)
        idx = jnp.min(jnp.where(logits == m, lane, LANES), axis=-1, keepdims=True)
        logits = jnp.where(lane == idx, NEG_INF, logits)
        top = m if top is None else top
        idx_out = jnp.where(lane == k, idx, idx_out)
        val_out = jnp.where(lane == k, jnp.exp(m - top), val_out)
    idx_ref[...] = idx_out
    gate_ref[...] = val_out / jnp.sum(val_out, axis=-1, keepdims=True)


def _mix_out(h, oa, ob, oc, wg, bg, wa, wb, wc, wo, g, b, wr3, br):
    N, D = h.shape
    T = MIX_TILE
    row = lambda i: (i, 0)
    full = lambda a: pl.BlockSpec(a.shape, lambda i: (0,) * a.ndim)
    return pl.pallas_call(
        _mix_out_kernel,
        grid=(N // T,),
        in_specs=[pl.BlockSpec((T, D), row), pl.BlockSpec((T, oa.shape[1]), row),
                  pl.BlockSpec((T, ob.shape[1]), row), pl.BlockSpec((T, oc.shape[1]), row),
                  full(wg), full(bg), full(wa), full(wb), full(wc), full(wo), full(g), full(b),
                  full(wr3), full(br)],
        out_specs=[pl.BlockSpec((T, D), row), pl.BlockSpec((T, LANES), row), pl.BlockSpec((T, LANES), row)],
        out_shape=[jax.ShapeDtypeStruct((N, D), jnp.float32),
                   jax.ShapeDtypeStruct((N, LANES), jnp.int32),
                   jax.ShapeDtypeStruct((N, LANES), jnp.float32)],
        compiler_params=_cparams("arbitrary"),
        name="mix_out",
    )(h, oa, ob, oc, wg, bg, wa, wb, wc, wo, g, b, wr3, br)


def _route_kernel(idx_ref, pos_ref, carry_out_ref, cnt_out_ref, total_ref, carry_ref):
    @pl.when(pl.program_id(0) == 0)
    def _():
        carry_ref[...] = jnp.zeros_like(carry_ref)

    idx = idx_ref[...]
    T = idx.shape[0]
    lane = _lane_iota(idx.shape)
    sel = [lane == idx[:, k:k + 1] for k in range(TOP_K)]
    onehot = sum(s.astype(jnp.float32) for s in sel)
    r = lax.broadcasted_iota(jnp.int32, (T, T), 0)
    c = lax.broadcasted_iota(jnp.int32, (T, T), 1)
    before = (c < r).astype(jnp.bfloat16)
    prefix = jnp.dot(before, onehot.astype(jnp.bfloat16), preferred_element_type=jnp.float32)
    cnt = jnp.broadcast_to(jnp.sum(onehot, axis=0, keepdims=True), carry_ref.shape)
    cnt = jnp.floor((cnt + (SUBLANES - 1)) * (1.0 / SUBLANES)) * SUBLANES
    er = lax.broadcasted_iota(jnp.int32, (LANES, LANES), 0)
    ec = lax.broadcasted_iota(jnp.int32, (LANES, LANES), 1)
    local_start = jnp.dot(cnt.astype(jnp.bfloat16), (er < ec).astype(jnp.bfloat16),
                          preferred_element_type=jnp.float32)
    target = prefix + local_start[0:1, :]
    out = jnp.zeros(idx.shape, jnp.int32)
    for k in range(TOP_K):
        pk = jnp.sum(jnp.where(sel[k], target, 0.0), axis=-1, keepdims=True)
        out = jnp.where(lane == k, pk.astype(jnp.int32), out)
    pos_ref[...] = out
    carry_out_ref[...] = carry_ref[...].astype(jnp.int32)
    cnt_out_ref[...] = cnt.astype(jnp.int32)
    carry_ref[...] = carry_ref[...] + cnt
    total_ref[...] = carry_ref[...].astype(jnp.int32)


def _route(idx):
    N = idx.shape[0]
    T = ROW_TILE
    n_tiles = N // T
    tile8 = pl.BlockSpec((8, LANES), lambda i: (i, 0))
    return pl.pallas_call(
        _route_kernel,
        grid=(n_tiles,),
        in_specs=[pl.BlockSpec((T, LANES), lambda i: (i, 0))],
        out_specs=[pl.BlockSpec((T, LANES), lambda i: (i, 0)), tile8, tile8,
                   pl.BlockSpec((8, LANES), lambda i: (0, 0))],
        out_shape=[jax.ShapeDtypeStruct((N, LANES), jnp.int32),
                   jax.ShapeDtypeStruct((n_tiles * 8, LANES), jnp.int32),
                   jax.ShapeDtypeStruct((n_tiles * 8, LANES), jnp.int32),
                   jax.ShapeDtypeStruct((8, LANES), jnp.int32)],
        scratch_shapes=[pltpu.VMEM((8, LANES), jnp.float32)],
        compiler_params=_cparams("arbitrary"),
        name="route",
    )(idx)


LOCAL_ROWS = TOP_K * ROW_TILE + ROW_TILE
STRIP_BUFFERS = 3
assert LOCAL_ROWS >= TOP_K * ROW_TILE + N_EXPERTS * (SUBLANES - 1)


HI16 = 0xFFFF0000


def _pack_rows(x):
    half = x.shape[1] // 2
    lo = lax.bitcast_convert_type(x[:, :half], jnp.uint32) >> 16
    hi = lax.bitcast_convert_type(x[:, half:], jnp.uint32) & jnp.uint32(HI16)
    return lo | hi


def _unpack_rows(w):
    lo = lax.bitcast_convert_type(w << 16, jnp.float32)
    hi = lax.bitcast_convert_type(w & jnp.uint32(HI16), jnp.float32)
    return jnp.concatenate([lo, hi], axis=1).astype(jnp.bfloat16)


def _round_bf16(x):
    return x.astype(jnp.bfloat16).astype(jnp.float32)


def _for_each_strip(cnt_ref, base_ref, tile, fn):
    def body(e, off):
        c = cnt_ref[tile * N_EXPERTS + e]
        d = base_ref[tile * N_EXPERTS + e]

        @pl.when(c > 0)
        def _():
            fn(pl.multiple_of(off, SUBLANES), pl.multiple_of(d, SUBLANES), pl.multiple_of(c, SUBLANES))
        return off + c

    lax.fori_loop(0, N_EXPERTS, body, 0)


def _dispatch_kernel(cnt_ref, base_ref, tail_ref, pos_ref, h_ref, xs_ref, buf, zeros, sems, zsem):
    i = pl.program_id(0)
    last = pl.num_programs(0) - 1
    T = h_ref.shape[0]
    cur = i % STRIP_BUFFERS

    @pl.when(i == 0)
    def _():
        zeros[...] = jnp.zeros_like(zeros)

        def tail_copy(e):
            row = pl.multiple_of(jnp.maximum(tail_ref[e], 0), SUBLANES)
            return pltpu.make_async_copy(zeros, xs_ref.at[pl.ds(row, EXPERT_BLOCK), :], zsem)

        def spare_copy(blk):
            row = pl.multiple_of(blk * EXPERT_BLOCK, EXPERT_BLOCK)
            return pltpu.make_async_copy(zeros, xs_ref.at[pl.ds(row, EXPERT_BLOCK), :], zsem)

        first_spare = tail_ref[N_EXPERTS] // EXPERT_BLOCK
        for act in ("start", "wait"):
            def body(e, carry, act=act):
                @pl.when(tail_ref[e] >= 0)
                def _():
                    getattr(tail_copy(e), act)()
                return carry
            lax.fori_loop(0, N_EXPERTS, body, 0)

            def spare(blk, carry, act=act):
                getattr(spare_copy(blk), act)()
                return carry
            lax.fori_loop(first_spare, xs_ref.shape[0] // EXPERT_BLOCK, spare, 0)

    pos_t = pos_ref[...].astype(jnp.float32).T.astype(jnp.int32)
    rows = lax.broadcasted_iota(jnp.int32, (LOCAL_ROWS, T), 0)
    hit = rows == pos_t[0:1, :]
    for k in range(1, TOP_K):
        hit = jnp.logical_or(hit, rows == pos_t[k:k + 1, :])
    buf[cur] = _pack_rows(jnp.dot(hit.astype(jnp.bfloat16), h_ref[...].astype(jnp.bfloat16),
                                  preferred_element_type=jnp.float32))

    def strip(tile):
        slot = tile % STRIP_BUFFERS

        def make(local_row, slot_row, rows):
            return pltpu.make_async_copy(buf.at[slot, pl.ds(local_row, rows), :],
                                         xs_ref.at[pl.ds(slot_row, rows), :], sems.at[slot])
        return make

    def drain(tile):
        _for_each_strip(cnt_ref, base_ref, tile, lambda *a: strip(tile)(*a).wait())

    @pl.when(i >= STRIP_BUFFERS - 1)
    def _():
        drain(i - (STRIP_BUFFERS - 1))

    _for_each_strip(cnt_ref, base_ref, i, lambda *a: strip(i)(*a).start())

    @pl.when(i == last)
    def _():
        for back in range(STRIP_BUFFERS - 2, -1, -1):
            @pl.when(i - back >= 0)
            def _(back=back):
                drain(i - back)


def _dispatch(cnt_flat, base_flat, tail, pos, h1, n_slots):
    N, D = h1.shape
    T = ROW_TILE
    return pl.pallas_call(
        _dispatch_kernel,
        grid_spec=pltpu.PrefetchScalarGridSpec(
            num_scalar_prefetch=3,
            grid=(N // T,),
            in_specs=[pl.BlockSpec((T, LANES), lambda i, c, b, t: (i, 0)),
                      pl.BlockSpec((T, D), lambda i, c, b, t: (i, 0))],
            out_specs=pl.BlockSpec(memory_space=pl.ANY),
            scratch_shapes=[pltpu.VMEM((STRIP_BUFFERS, LOCAL_ROWS, D // 2), jnp.uint32),
                            pltpu.VMEM((EXPERT_BLOCK, D // 2), jnp.uint32),
                            pltpu.SemaphoreType.DMA((STRIP_BUFFERS,)),
                            pltpu.SemaphoreType.DMA(())]),
        out_shape=jax.ShapeDtypeStruct((n_slots, D // 2), jnp.uint32),
        compiler_params=_cparams("arbitrary"),
        name="dispatch",
    )(cnt_flat, base_flat, tail, pos, h1)


def _experts_kernel(blk_e_ref, n_used_ref, x_ref, wi_ref, bi_ref, wo_ref, bo_ref, y_ref, wi_bf, wo_bf):
    i = pl.program_id(0)
    used = i < n_used_ref[0]
    new_expert = jnp.logical_or(i == 0, blk_e_ref[i] != blk_e_ref[jnp.maximum(i - 1, 0)])

    @pl.when(jnp.logical_and(used, new_expert))
    def _():
        wi_bf[...] = wi_ref[0].astype(jnp.bfloat16)
        wo_bf[...] = wo_ref[0].astype(jnp.bfloat16)

    @pl.when(used)
    def _():
        hb = jnp.dot(_unpack_rows(x_ref[...]), wi_bf[...], preferred_element_type=jnp.float32) + bi_ref[0]
        De = D_MODEL
        x_glu = jnp.minimum(hb[:, :De], SWIGLU_LIMIT)
        x_lin = jnp.clip(hb[:, De:], -SWIGLU_LIMIT, SWIGLU_LIMIT)
        act = x_glu * jax.nn.sigmoid(SWIGLU_ALPHA * x_glu) * (x_lin + 1.0)
        y = jnp.dot(act.astype(jnp.bfloat16), wo_bf[...], preferred_element_type=jnp.float32) + bo_ref[0]
        y_ref[...] = _pack_rows(_round_bf16(y))

    @pl.when(jnp.logical_not(used))
    def _():
        y_ref[...] = jnp.zeros_like(y_ref)


def _experts(blk_e, n_used, xs, w_in, b_in, w_out, b_out):
    n_slots, half = xs.shape
    R = EXPERT_BLOCK
    E, D, F = w_in.shape
    return pl.pallas_call(
        _experts_kernel,
        grid_spec=pltpu.PrefetchScalarGridSpec(
            num_scalar_prefetch=2,
            grid=(n_slots // R,),
            in_specs=[pl.BlockSpec((R, half), lambda i, e, n: (jnp.minimum(i, n[0] - 1), 0)),
                      pl.BlockSpec((1, D, F), lambda i, e, n: (e[i], 0, 0)),
                      pl.BlockSpec((1, 1, F), lambda i, e, n: (e[i], 0, 0)),
                      pl.BlockSpec((1, F // 2, D), lambda i, e, n: (e[i], 0, 0)),
                      pl.BlockSpec((1, 1, D), lambda i, e, n: (e[i], 0, 0))],
            out_specs=pl.BlockSpec((R, half), lambda i, e, n: (i, 0)),
            scratch_shapes=[pltpu.VMEM((D, F), jnp.bfloat16), pltpu.VMEM((F // 2, D), jnp.bfloat16)]),
        out_shape=jax.ShapeDtypeStruct((n_slots, half), jnp.uint32),
        compiler_params=pltpu.CompilerParams(dimension_semantics=("arbitrary",),
                                             vmem_limit_bytes=EXPERTS_VMEM_LIMIT),
        name="experts",
    )(blk_e, n_used, xs, w_in, b_in, w_out, b_out)


def _combine_kernel(cnt_ref, base_ref, pos_ref, gate_ref, h_ref, g_ref, b_ref, ys_ref, o_ref, buf, sems):
    i = pl.program_id(0)
    last = pl.num_programs(0) - 1
    T = h_ref.shape[0]
    cur = i % STRIP_BUFFERS

    def strip(tile):
        slot = tile % STRIP_BUFFERS

        def make(local_row, slot_row, rows):
            return pltpu.make_async_copy(ys_ref.at[pl.ds(slot_row, rows), :],
                                         buf.at[slot, pl.ds(local_row, rows), :], sems.at[slot])
        return make

    def fetch(tile):
        _for_each_strip(cnt_ref, base_ref, tile, lambda *a: strip(tile)(*a).start())

    @pl.when(i == 0)
    def _():
        buf[...] = jnp.zeros_like(buf)
        for ahead in range(STRIP_BUFFERS - 1):
            @pl.when(ahead <= last)
            def _(ahead=ahead):
                fetch(ahead)

    @pl.when(i + (STRIP_BUFFERS - 1) <= last)
    def _():
        fetch(i + (STRIP_BUFFERS - 1))

    _for_each_strip(cnt_ref, base_ref, i, lambda *a: strip(i)(*a).wait())

    pos, gate = pos_ref[...], gate_ref[...]
    col = lax.broadcasted_iota(jnp.int32, (T, LOCAL_ROWS), 1)
    u = jnp.zeros(col.shape, jnp.float32)
    for k in range(TOP_K):
        u = jnp.where(col == pos[:, k:k + 1], gate[:, k:k + 1], u)
    u_hi, u_lo = _split2(u)
    y = _unpack_rows(buf[cur])
    dot = lambda a, b: jnp.dot(a, b, preferred_element_type=jnp.float32)
    ffn = dot(u_hi, y) + dot(u_lo, y)
    o_ref[...] = _layer_norm(DEEPNORM_ALPHA * h_ref[...] + ffn, g_ref[...], b_ref[...])


def _combine(cnt_flat, base_flat, pos, gate, h1, g, b, ys):
    N, D = h1.shape
    T = ROW_TILE
    row = lambda i, c, s: (i, 0)
    const = lambda i, c, s: (0, 0)
    return pl.pallas_call(
        _combine_kernel,
        grid_spec=pltpu.PrefetchScalarGridSpec(
            num_scalar_prefetch=2,
            grid=(N // T,),
            in_specs=[pl.BlockSpec((T, LANES), row), pl.BlockSpec((T, LANES), row), pl.BlockSpec((T, D), row),
                      pl.BlockSpec(g.shape, const), pl.BlockSpec(b.shape, const),
                      pl.BlockSpec(memory_space=pl.ANY)],
            out_specs=pl.BlockSpec((T, D), row),
            scratch_shapes=[pltpu.VMEM((STRIP_BUFFERS, LOCAL_ROWS, D // 2), jnp.uint32),
                            pltpu.SemaphoreType.DMA((STRIP_BUFFERS,))]),
        out_shape=jax.ShapeDtypeStruct((N, D), jnp.float32),
        compiler_params=_cparams("arbitrary"),
        name="combine",
    )(cnt_flat, base_flat, pos, gate, h1, g, b, ys)


def _rope_tables(S):
    t = np.arange(S)
    row, col = (t // GRID_W).astype(np.float64), (t % GRID_W).astype(np.float64)

    def block(half):
        inv = ROPE_THETA ** (-np.arange(half, dtype=np.float64) / half)
        ar, ac = row[:, None] * inv[None, :], col[:, None] * inv[None, :]
        cos = np.concatenate([np.cos(ar), np.cos(ar), np.cos(ac), np.cos(ac)], axis=1)
        sin = np.concatenate([-np.sin(ar), np.sin(ar), -np.sin(ac), np.sin(ac)], axis=1)
        return cos, sin

    ca64, sa64 = block(GQA_HEAD_DIM // 4)
    ca, sa = np.tile(ca64, (1, 2)), np.tile(sa64, (1, 2))
    cb32, sb32 = block(MLA_ROPE_DIM // 4)
    ones, zeros = np.ones((S, MLA_NOPE_DIM)), np.zeros((S, MLA_NOPE_DIM))
    cb = np.concatenate([ones, cb32, ones[:, :LANES - MLA_NOPE_DIM - MLA_ROPE_DIM]], axis=1)
    sb = np.concatenate([zeros, sb32, zeros[:, :LANES - MLA_NOPE_DIM - MLA_ROPE_DIM]], axis=1)
    return tuple(jnp.asarray(a, jnp.float32) for a in (ca, sa, cb, sb))


def _layer(h_in_is_x, x2, mem, B, S, ln_g, ln_b, w_in_proj, b_gate, gqa_q_norm, gqa_k_norm, mla_q_norm,
           mla_kv_norm, w_mla_qb, w_mla_kvb, w_mem_kv, w_br_gqa, w_br_mla, w_br_mem, w_out,
           ln1_g, ln1_b, w_router, b_router, w_exp_in, b_exp_in, w_exp_out, b_exp_out, ln2_g, ln2_b):
    del h_in_is_x
    bf = jnp.bfloat16
    N, D = x2.shape
    row2 = lambda a: a.reshape(1, -1)

    W = w_in_proj
    zc = lambda n: jnp.zeros((D, n), W.dtype)
    wa = jnp.concatenate([W[:, :OFF_KROPE], zc(HALF), W[:, OFF_KROPE:OFF_QM],
                          zc(LANES - HALF - MLA_ROPE_DIM), W[:, OFF_QM:OFF_GATE]], axis=1).astype(bf)
    wg = W[:, OFF_GATE:].astype(bf)
    qd = MLA_NOPE_DIM + MLA_ROPE_DIM
    wqb = jnp.pad(w_mla_qb.reshape(MLA_Q_LORA, MLA_HEADS, qd),
                  ((0, 0), (0, 0), (0, LANES - qd))).reshape(MLA_Q_LORA, MLA_HEADS * LANES).astype(bf)
    kvb = w_mla_kvb.reshape(MLA_KV_LORA, MLA_HEADS, MLA_NOPE_DIM + MLA_V_DIM)
    wkvk = jnp.pad(kvb[:, :, :MLA_NOPE_DIM], ((0, 0), (0, 0), (0, LANES - MLA_NOPE_DIM))
                   ).reshape(MLA_KV_LORA, MLA_HEADS * LANES).astype(bf)
    wkvv = kvb[:, :, MLA_NOPE_DIM:].reshape(MLA_KV_LORA, MLA_HEADS * MLA_V_DIM).astype(bf)
    gq = row2(jnp.tile(gqa_q_norm, 2))
    gk = row2(jnp.tile(gqa_k_norm, 2))
    ca, sa, cb, sb = _rope_tables(S)

    mkt, mv = _mem_kv(mem, w_mem_kv.astype(bf))
    h, qat, ka, vat, qbt, kb, vbt, oc = _in_proj(
        x2, row2(ln_g), row2(ln_b), wa, gq, gk, row2(mla_q_norm), row2(mla_kv_norm),
        wqb, wkvk, wkvv, ca, sa, cb, sb, mkt, mv, B, S)

    oa = _attention(qat, ka.reshape(B, S, -1), vat, shared_kv=True)
    ob = _attention(qbt, kb.reshape(B, S, -1), vbt, shared_kv=False)

    wr = jnp.pad(w_router, ((0, 0), (0, LANES - N_EXPERTS)))
    wr3 = jnp.stack(_split2(wr))
    br = row2(jnp.pad(b_router, (0, LANES - N_EXPERTS)))
    h1, idx, gate = _mix_out(h, oa.reshape(N, -1), ob.reshape(N, -1), oc, wg, row2(b_gate),
                             w_br_gqa.astype(bf), w_br_mla.astype(bf), w_br_mem.astype(bf),
                             w_out.astype(bf), row2(ln1_g), row2(ln1_b), wr3, br)

    pos, carry_t, cnt_t, total = _route(idx)
    counts = total[0, :N_EXPERTS]
    R = EXPERT_BLOCK
    padded = (counts + R - 1) // R * R
    pad_end = jnp.cumsum(padded)
    pad_start = (pad_end - padded).astype(jnp.int32)
    n_tiles = N // ROW_TILE
    max_rows = N * TOP_K + n_tiles * N_EXPERTS * (SUBLANES - 1) + N_EXPERTS * (R - 1)
    n_blocks = -(-max_rows // R)
    blk_start = jnp.arange(n_blocks, dtype=jnp.int32) * R
    blk_e = jnp.sum((pad_end[None, :] <= blk_start[:, None]).astype(jnp.int32), axis=1)
    n_used = (pad_end[-1:] // R).astype(jnp.int32)
    last_e = jnp.max(jnp.where(blk_start < pad_end[-1], blk_e, 0))
    blk_e = jnp.minimum(blk_e, last_e).astype(jnp.int32)
    per_tile = lambda a: a.reshape(n_tiles, 8, LANES)[:, 0, :N_EXPERTS]
    cnt_flat = per_tile(cnt_t).reshape(-1)
    base_flat = (per_tile(carry_t) + pad_start[None, :]).reshape(-1)
    tail = jnp.concatenate([jnp.where(padded > 0, pad_end - R, -1), pad_end[-1:]]).astype(jnp.int32)

    xs = _dispatch(cnt_flat, base_flat, tail, pos, h1, n_blocks * R)
    ys = _experts(blk_e, n_used, xs, w_exp_in, b_exp_in[:, None, :], w_exp_out, b_exp_out[:, None, :])
    return _combine(cnt_flat, base_flat, pos, gate, h1, row2(ln2_g), row2(ln2_b), ys)


def kernel(x, mem, ln_in_g, ln_in_b, w_in_proj, b_gate, gqa_q_norm, gqa_k_norm, mla_q_norm, mla_kv_norm, w_mla_qb, w_mla_kvb, w_mem_kv, w_br_gqa, w_br_mla, w_br_mem, w_out, ln1_g, ln1_b, w_router, b_router, w_exp_in, b_exp_in, w_exp_out, b_exp_out, ln2_g, ln2_b):
    B, S, D = x.shape
    depth = w_in_proj.shape[0]
    assert depth == 1, "the input LayerNorm is fused into the first (only) layer's projection kernel"
    out = _layer(True, x.reshape(B * S, D), mem, B, S, ln_in_g, ln_in_b, w_in_proj[0], b_gate[0],
                 gqa_q_norm[0], gqa_k_norm[0], mla_q_norm[0], mla_kv_norm[0], w_mla_qb[0], w_mla_kvb[0],
                 w_mem_kv[0], w_br_gqa[0], w_br_mla[0], w_br_mem[0], w_out[0], ln1_g[0], ln1_b[0],
                 w_router[0], b_router[0], w_exp_in[0], b_exp_in[0], w_exp_out[0], b_exp_out[0],
                 ln2_g[0], ln2_b[0])
    return out.reshape(B, S, D)
```

```python
import functools

import jax
import jax.numpy as jnp
import numpy as np
from jax import lax
from jax.experimental import pallas as pl
from jax.experimental.pallas import tpu as pltpu

D_MODEL = 1024
MEM_LEN = 256
GRID_W = 64
ROPE_THETA = 10000.0
RMS_EPS = 1e-6
LN_EPS = 1e-5

GQA_HEADS = 8
GQA_KV_HEADS = 2
GQA_HEAD_DIM = 64
MLA_HEADS = 8
MLA_NOPE_DIM = 64
MLA_ROPE_DIM = 32
MLA_V_DIM = 64
MLA_Q_LORA = 384
MLA_KV_LORA = 256
MEM_HEADS = 4
MEM_HEAD_DIM = 128

N_EXPERTS = 32
TOP_K = 4
SWIGLU_LIMIT = 7.0
SWIGLU_ALPHA = 1.702
DEEPNORM_ALPHA = 2.0 ** 0.25

LANES = 128
SUBLANES = 8
BF16_SUBLANES = 16
HALF = LANES // 2
NEG_INF = float("-inf")
LOG2_E = 1.4426950408889634

OFF_QA, OFF_KA, OFF_VA, OFF_QLAT, OFF_KVLAT, OFF_KROPE, OFF_QM, OFF_GATE = (
    0, 512, 640, 768, 1152, 1408, 1440, 1952)
IN_PROJ_W = 5024
A_QA, A_KA, A_VA, A_QLAT, A_KVLAT, A_KROPE, A_QM, A_END = 0, 512, 640, 768, 1152, 1408, 1536, 2048
ROW_TILE = 256
IN_TILE = 512
IN_CHUNK = 256
MIX_TILE = 512
MIX_COLS = 256
ATTN_Q_TILE = 256
ATTN_K_CHUNK = 512
EXPERT_BLOCK = 256
VMEM_LIMIT = 48 * 1024 * 1024
EXPERTS_VMEM_LIMIT = 56 * 1024 * 1024


def _cparams(*sem):
    return pltpu.CompilerParams(dimension_semantics=sem, vmem_limit_bytes=VMEM_LIMIT)


def _lane_iota(shape):
    return lax.broadcasted_iota(jnp.int32, shape, len(shape) - 1)


def _layer_norm(x, g, b):
    mu = jnp.mean(x, axis=-1, keepdims=True)
    xc = x - mu
    var = jnp.mean(xc * xc, axis=-1, keepdims=True)
    return xc * lax.rsqrt(var + LN_EPS) * g + b


def _bdot(a, b):
    return jnp.dot(a.astype(jnp.bfloat16), b.astype(jnp.bfloat16), preferred_element_type=jnp.float32)


def _rope(x, cos, sin_signed, pair):
    lane = _lane_iota(x.shape)
    fwd = pltpu.roll(x, LANES - pair, 1)
    bwd = pltpu.roll(x, pair, 1)
    partner = jnp.where((lane % (2 * pair)) < pair, fwd, bwd)
    return x * cos + partner * sin_signed


def _half_rms_scale(x):
    lane = _lane_iota(x.shape)
    sq = x * x
    lo = jnp.sum(jnp.where(lane < HALF, sq, 0.0), axis=-1, keepdims=True)
    hi = jnp.sum(jnp.where(lane >= HALF, sq, 0.0), axis=-1, keepdims=True)
    inv = 1.0 / GQA_HEAD_DIM
    return jnp.where(lane < HALF, lax.rsqrt(lo * inv + RMS_EPS), lax.rsqrt(hi * inv + RMS_EPS))


def _softmax_rows(s):
    m = jnp.max(s, axis=-1, keepdims=True)
    p = jnp.exp(s - m)
    return p, jnp.sum(p, axis=-1, keepdims=True)


def _mem_kv_kernel(mem_ref, w_ref, kt_ref, v_ref):
    kv = _bdot(mem_ref[0], w_ref[...])
    width = MEM_HEADS * MEM_HEAD_DIM
    kt_ref[0] = kv[:, :width].T.astype(jnp.bfloat16)
    v_ref[0] = kv[:, width:].astype(jnp.bfloat16)


def _mem_kv(mem, w_mem_kv):
    B, M, D = mem.shape
    width = MEM_HEADS * MEM_HEAD_DIM
    return pl.pallas_call(
        _mem_kv_kernel,
        grid=(B,),
        in_specs=[pl.BlockSpec((1, M, D), lambda b: (b, 0, 0)),
                  pl.BlockSpec((D, 2 * width), lambda b: (0, 0))],
        out_specs=[pl.BlockSpec((1, width, M), lambda b: (b, 0, 0)),
                   pl.BlockSpec((1, M, width), lambda b: (b, 0, 0))],
        out_shape=[jax.ShapeDtypeStruct((B, width, M), jnp.bfloat16),
                   jax.ShapeDtypeStruct((B, M, width), jnp.bfloat16)],
        compiler_params=_cparams("arbitrary"),
        name="mem_kv",
    )(mem, w_mem_kv)


def _in_proj_kernel(x_ref, lng_ref, lnb_ref, wa_ref, gq_ref, gk_ref, gql_ref, gkvl_ref,
                    wqb_ref, wkvk_ref, wkvv_ref, ca_ref, sa_ref, cb_ref, sb_ref, mkt_ref, mv_ref,
                    h_ref, qa_ref, ka_ref, va_ref, qb_ref, kb_ref, vb_ref, oc_ref):
    lane = _lane_iota((IN_CHUNK, LANES))
    lo_mask = lane < HALF
    q_scale = GQA_HEAD_DIM ** -0.5 * LOG2_E
    qb_scale = (MLA_NOPE_DIM + MLA_ROPE_DIM) ** -0.5 * LOG2_E
    per_group = GQA_HEADS // GQA_KV_HEADS
    gq, gk = gq_ref[...], gk_ref[...]

    def project(r):
        h = _layer_norm(x_ref[r, :], lng_ref[...], lnb_ref[...])
        h_ref[r, :] = h
        return _bdot(h, wa_ref[...])

    def finish(r, proj):
        ca, sa, cb, sb = ca_ref[r, :], sa_ref[r, :], cb_ref[r, :], sb_ref[r, :]

        for c in range(GQA_HEADS // 2):
            slab = proj[:, A_QA + c * LANES:A_QA + (c + 1) * LANES]
            slab = _rope(slab * _half_rms_scale(slab) * gq, ca, sa, 16) * q_scale
            swapped = pltpu.roll(slab, HALF, 1)
            for half in range(2):
                head = 2 * c + half
                group_lo = (head // per_group) == 0
                src = slab if (half == 0) == group_lo else swapped
                keep = lo_mask if group_lo else jnp.logical_not(lo_mask)
                qa_ref[0, head * LANES:(head + 1) * LANES, r] = jnp.where(keep, src, 0.0).astype(jnp.bfloat16).T

        ka = proj[:, A_KA:A_KA + LANES]
        ka_ref[r, :] = _rope(ka * _half_rms_scale(ka) * gk, ca, sa, 16).astype(jnp.bfloat16)
        va_ref[0, :, r] = proj[:, A_VA:A_VA + LANES].astype(jnp.bfloat16).T

        ql = proj[:, A_QLAT:A_KVLAT]
        ql = ql * lax.rsqrt(jnp.mean(ql * ql, axis=-1, keepdims=True) + RMS_EPS) * gql_ref[...]
        qm = _bdot(ql, wqb_ref[...])
        kvl = proj[:, A_KVLAT:A_KROPE]
        kvl = kvl * lax.rsqrt(jnp.mean(kvl * kvl, axis=-1, keepdims=True) + RMS_EPS) * gkvl_ref[...]
        kn = _bdot(kvl, wkvk_ref[...])
        vb = _bdot(kvl, wkvv_ref[...])
        for c in range(vb.shape[1] // LANES):
            sl = slice(c * LANES, (c + 1) * LANES)
            vb_ref[0, sl, r] = vb[:, sl].astype(jnp.bfloat16).T
        k_pe = _rope(proj[:, A_KROPE:A_QM], cb, sb, 8)
        for hd in range(MLA_HEADS):
            sl = slice(hd * LANES, (hd + 1) * LANES)
            partner = qm[:, (MLA_HEADS + hd) * LANES:(MLA_HEADS + hd + 1) * LANES]
            qb_ref[0, sl, r] = ((qm[:, sl] * cb + partner * sb) * qb_scale).astype(jnp.bfloat16).T
            kb_ref[r, sl] = (kn[:, sl] + k_pe).astype(jnp.bfloat16)

        qc = proj[:, A_QM:A_END] * (MEM_HEAD_DIM ** -0.5)
        for hd in range(MEM_HEADS):
            sl = slice(hd * MEM_HEAD_DIM, (hd + 1) * MEM_HEAD_DIM)
            p, l = _softmax_rows(_bdot(qc[:, sl], mkt_ref[0, sl, :]))
            oc_ref[r, sl] = (_bdot(p, mv_ref[0, :, sl]) / l).astype(jnp.bfloat16)

    chunks = [slice(s * IN_CHUNK, (s + 1) * IN_CHUNK) for s in range(x_ref.shape[0] // IN_CHUNK)]
    pending = None
    for r in chunks:
        proj = project(r)
        if pending is not None:
            finish(*pending)
        pending = (r, proj)
    finish(*pending)


def _in_proj(x2, ln_g, ln_b, wa, gq, gk, gql, gkvl, wqb, wkvk, wkvv, ca, sa, cb, sb, mkt, mv, B, S):
    N, D = x2.shape
    T = IN_TILE
    tiles_per_seq = S // T
    row = lambda i: (i, 0)
    const = lambda i: (0, 0)
    pos = lambda i: (i % tiles_per_seq, 0)
    batch3 = lambda i: (i // tiles_per_seq, 0, 0)
    kt_map = lambda i: (i // tiles_per_seq, 0, i % tiles_per_seq)
    full = lambda a: pl.BlockSpec(a.shape, const)
    return pl.pallas_call(
        _in_proj_kernel,
        grid=(N // T,),
        in_specs=[pl.BlockSpec((T, D), row), full(ln_g), full(ln_b), full(wa), full(gq), full(gk),
                  full(gql), full(gkvl), full(wqb), full(wkvk), full(wkvv),
                  pl.BlockSpec((T, LANES), pos), pl.BlockSpec((T, LANES), pos),
                  pl.BlockSpec((T, LANES), pos), pl.BlockSpec((T, LANES), pos),
                  pl.BlockSpec((1,) + mkt.shape[1:], batch3), pl.BlockSpec((1,) + mv.shape[1:], batch3)],
        out_specs=[pl.BlockSpec((T, D), row),
                   pl.BlockSpec((1, GQA_HEADS * LANES, T), kt_map),
                   pl.BlockSpec((T, LANES), row),
                   pl.BlockSpec((1, LANES, T), kt_map),
                   pl.BlockSpec((1, MLA_HEADS * LANES, T), kt_map),
                   pl.BlockSpec((T, MLA_HEADS * LANES), row),
                   pl.BlockSpec((1, MLA_HEADS * MLA_V_DIM, T), kt_map),
                   pl.BlockSpec((T, MEM_HEADS * MEM_HEAD_DIM), row)],
        out_shape=[jax.ShapeDtypeStruct((N, D), jnp.float32),
                   jax.ShapeDtypeStruct((B, GQA_HEADS * LANES, S), jnp.bfloat16),
                   jax.ShapeDtypeStruct((N, LANES), jnp.bfloat16),
                   jax.ShapeDtypeStruct((B, LANES, S), jnp.bfloat16),
                   jax.ShapeDtypeStruct((B, MLA_HEADS * LANES, S), jnp.bfloat16),
                   jax.ShapeDtypeStruct((N, MLA_HEADS * LANES), jnp.bfloat16),
                   jax.ShapeDtypeStruct((B, MLA_HEADS * MLA_V_DIM, S), jnp.bfloat16),
                   jax.ShapeDtypeStruct((N, MEM_HEADS * MEM_HEAD_DIM), jnp.bfloat16)],
        compiler_params=_cparams("arbitrary"),
        name="in_proj",
    )(x2, ln_g, ln_b, wa, gq, gk, gql, gkvl, wqb, wkvk, wkvv, ca, sa, cb, sb, mkt, mv)


def _attention_kernel(qt_ref, k_ref, vt_ref, o_ref, s_even, s_odd, *, shared_kv):
    t = pl.program_id(0)

    @pl.when(t == 0)
    def _():
        s_odd[...] = jnp.zeros_like(s_odd)

    def step(s_new, s_old):
        S, TQ = s_new.shape[1:]
        m = [jnp.full((1, TQ), NEG_INF, jnp.float32)] * 2
        ones = jnp.ones((BF16_SUBLANES, ATTN_K_CHUNK), jnp.bfloat16)
        acc = [jnp.zeros((HALF + BF16_SUBLANES, TQ), jnp.float32)] * 2
        for c in range(S // ATTN_K_CHUNK):
            rows = slice(c * ATTN_K_CHUNK, (c + 1) * ATTN_K_CHUNK)
            for hd in range(2):
                qt = qt_ref[0, hd * LANES:(hd + 1) * LANES, :]
                k = k_ref[0, rows, :] if shared_kv else k_ref[0, rows, hd * LANES:(hd + 1) * LANES]
                s_new[hd, rows, :] = jnp.dot(k, qt, preferred_element_type=jnp.float32)
                vt = vt_ref[0, :, rows] if shared_kv else vt_ref[0, hd * HALF:(hd + 1) * HALF, rows]
                vt = jnp.concatenate([vt, ones], axis=0)
                sc = s_old[hd, rows, :]
                m_new = jnp.maximum(m[hd], jnp.max(sc, axis=0, keepdims=True))
                p = jnp.exp2(sc - m_new).astype(jnp.bfloat16)
                corr = jnp.exp2(m[hd] - m_new)
                acc[hd] = acc[hd] * corr + jnp.dot(vt, p, preferred_element_type=jnp.float32)
                m[hd] = m_new
        outs = [acc[hd][:HALF] / acc[hd][HALF:HALF + 1] for hd in range(2)]
        o_ref[0] = jnp.concatenate(outs, axis=0).T.astype(jnp.bfloat16)

    @pl.when(t % 2 == 0)
    def _():
        step(s_even, s_odd)

    @pl.when(t % 2 == 1)
    def _():
        step(s_odd, s_even)


def _attention(qt, k, vt, *, shared_kv):
    B, qh, S = qt.shape
    pairs = qh // (2 * LANES)
    TQ = ATTN_Q_TILE
    nq = S // TQ
    items = B * pairs * nq
    pairs_per_group = pairs // GQA_KV_HEADS

    def item(t):
        return t // (pairs * nq), (t // nq) % pairs, t % nq

    score_item = lambda t: item(jnp.minimum(t, items - 1))
    finish_item = lambda t: item(jnp.maximum(t - 1, 0))

    def qt_map(t):
        b, j, i = score_item(t)
        return b, j, i

    def k_map(t):
        b, j, _ = score_item(t)
        return (b, 0, 0) if shared_kv else (b, 0, j)

    def vt_map(t):
        b, j, _ = finish_item(t)
        return (b, j // pairs_per_group, 0) if shared_kv else (b, j, 0)

    def o_map(t):
        b, j, i = finish_item(t)
        return b, i, j

    k_spec = pl.BlockSpec((1, S, LANES if shared_kv else 2 * LANES), k_map)
    vt_spec = pl.BlockSpec((1, HALF if shared_kv else LANES, S), vt_map)
    return pl.pallas_call(
        functools.partial(_attention_kernel, shared_kv=shared_kv),
        grid=(items + 1,),
        in_specs=[pl.BlockSpec((1, 2 * LANES, TQ), qt_map), k_spec, vt_spec],
        out_specs=pl.BlockSpec((1, TQ, LANES), o_map),
        out_shape=jax.ShapeDtypeStruct((B, S, pairs * LANES), jnp.bfloat16),
        scratch_shapes=[pltpu.VMEM((2, S, TQ), jnp.float32), pltpu.VMEM((2, S, TQ), jnp.float32)],
        compiler_params=_cparams("arbitrary"),
        name="attention_gqa" if shared_kv else "attention_mla",
    )(qt, k, vt)


def _split2(x):
    hi = x.astype(jnp.bfloat16)
    return hi, (x - hi.astype(jnp.float32)).astype(jnp.bfloat16)


def _mix_out_kernel(h_ref, oa_ref, ob_ref, oc_ref, wg_ref, bg_ref, wa_ref, wb_ref, wc_ref, wo_ref,
                    g_ref, b_ref, wr_ref, br_ref, h1_ref, idx_ref, gate_ref):
    h = h_ref[...]
    hb = h.astype(jnp.bfloat16)
    D = D_MODEL
    dot = lambda a, b: jnp.dot(a, b, preferred_element_type=jnp.float32)
    merged = []
    for n in range(D // MIX_COLS):
        acc = None
        for i, (o_ref, w_ref) in enumerate(((oa_ref, wa_ref), (ob_ref, wb_ref), (oc_ref, wc_ref))):
            gate_cols = slice(i * D + n * MIX_COLS, i * D + (n + 1) * MIX_COLS)
            gate = jax.nn.sigmoid(dot(hb, wg_ref[:, gate_cols]) + bg_ref[:, gate_cols])
            term = gate * dot(o_ref[...], w_ref[:, n * MIX_COLS:(n + 1) * MIX_COLS])
            acc = term if acc is None else acc + term
        merged.append(acc.astype(jnp.bfloat16))
    mix = dot(jnp.concatenate(merged, axis=1), wo_ref[...])
    h1 = _layer_norm(DEEPNORM_ALPHA * h + mix, g_ref[...], b_ref[...])
    h1_ref[...] = h1

    h_hi, h_lo = _split2(h1)
    w_hi, w_lo = wr_ref[0], wr_ref[1]
    dot = lambda a, b: jnp.dot(a, b, preferred_element_type=jnp.float32)
    logits = dot(h_hi, w_hi) + (dot(h_hi, w_lo) + dot(h_lo, w_hi)) + br_ref[...]
    lane = _lane_iota(logits.shape)
    logits = jnp.where(lane < N_EXPERTS, logits, NEG_INF)

    idx_out = jnp.zeros(logits.shape, jnp.int32)
    val_out = jnp.zeros(logits.shape, jnp.float32)
    top = None
    for k in range(TOP_K):
        m = jnp.max(logits, axis=-1, keepdims=True)
        idx = jnp.min(jnp.where(logits == m, lane, LANES), axis=-1, keepdims=True)
        logits = jnp.where(lane == idx, NEG_INF, logits)
        top = m if top is None else top
        idx_out = jnp.where(lane == k, idx, idx_out)
        val_out = jnp.where(lane == k, jnp.exp(m - top), val_out)
    idx_ref[...] = idx_out
    gate_ref[...] = val_out / jnp.sum(val_out, axis=-1, keepdims=True)


def _mix_out(h, oa, ob, oc, wg, bg, wa, wb, wc, wo, g, b, wr3, br):
    N, D = h.shape
    T = MIX_TILE
    row = lambda i: (i, 0)
    full = lambda a: pl.BlockSpec(a.shape, lambda i: (0,) * a.ndim)
    return pl.pallas_call(
        _mix_out_kernel,
        grid=(N // T,),
        in_specs=[pl.BlockSpec((T, D), row), pl.BlockSpec((T, oa.shape[1]), row),
                  pl.BlockSpec((T, ob.shape[1]), row), pl.BlockSpec((T, oc.shape[1]), row),
                  full(wg), full(bg), full(wa), full(wb), full(wc), full(wo), full(g), full(b),
                  full(wr3), full(br)],
        out_specs=[pl.BlockSpec((T, D), row), pl.BlockSpec((T, LANES), row), pl.BlockSpec((T, LANES), row)],
        out_shape=[jax.ShapeDtypeStruct((N, D), jnp.float32),
                   jax.ShapeDtypeStruct((N, LANES), jnp.int32),
                   jax.ShapeDtypeStruct((N, LANES), jnp.float32)],
        compiler_params=_cparams("arbitrary"),
        name="mix_out",
    )(h, oa, ob, oc, wg, bg, wa, wb, wc, wo, g, b, wr3, br)


def _route_kernel(idx_ref, pos_ref, carry_out_ref, cnt_out_ref, total_ref, carry_ref):
    @pl.when(pl.program_id(0) == 0)
    def _():
        carry_ref[...] = jnp.zeros_like(carry_ref)

    idx = idx_ref[...]
    T = idx.shape[0]
    lane = _lane_iota(idx.shape)
    sel = [lane == idx[:, k:k + 1] for k in range(TOP_K)]
    onehot = sum(s.astype(jnp.float32) for s in sel)
    r = lax.broadcasted_iota(jnp.int32, (T, T), 0)
    c = lax.broadcasted_iota(jnp.int32, (T, T), 1)
    before = (c < r).astype(jnp.bfloat16)
    prefix = jnp.dot(before, onehot.astype(jnp.bfloat16), preferred_element_type=jnp.float32)
    cnt = jnp.broadcast_to(jnp.sum(onehot, axis=0, keepdims=True), carry_ref.shape)
    cnt = jnp.floor((cnt + (SUBLANES - 1)) * (1.0 / SUBLANES)) * SUBLANES
    er = lax.broadcasted_iota(jnp.int32, (LANES, LANES), 0)
    ec = lax.broadcasted_iota(jnp.int32, (LANES, LANES), 1)
    local_start = jnp.dot(cnt.astype(jnp.bfloat16), (er < ec).astype(jnp.bfloat16),
                          preferred_element_type=jnp.float32)
    target = prefix + local_start[0:1, :]
    out = jnp.zeros(idx.shape, jnp.int32)
    for k in range(TOP_K):
        pk = jnp.sum(jnp.where(sel[k], target, 0.0), axis=-1, keepdims=True)
        out = jnp.where(lane == k, pk.astype(jnp.int32), out)
    pos_ref[...] = out
    carry_out_ref[...] = carry_ref[...].astype(jnp.int32)
    cnt_out_ref[...] = cnt.astype(jnp.int32)
    carry_ref[...] = carry_ref[...] + cnt
    total_ref[...] = carry_ref[...].astype(jnp.int32)


def _route(idx):
    N = idx.shape[0]
    T = ROW_TILE
    n_tiles = N // T
    tile8 = pl.BlockSpec((8, LANES), lambda i: (i, 0))
    return pl.pallas_call(
        _route_kernel,
        grid=(n_tiles,),
        in_specs=[pl.BlockSpec((T, LANES), lambda i: (i, 0))],
        out_specs=[pl.BlockSpec((T, LANES), lambda i: (i, 0)), tile8, tile8,
                   pl.BlockSpec((8, LANES), lambda i: (0, 0))],
        out_shape=[jax.ShapeDtypeStruct((N, LANES), jnp.int32),
                   jax.ShapeDtypeStruct((n_tiles * 8, LANES), jnp.int32),
                   jax.ShapeDtypeStruct((n_tiles * 8, LANES), jnp.int32),
                   jax.ShapeDtypeStruct((8, LANES), jnp.int32)],
        scratch_shapes=[pltpu.VMEM((8, LANES), jnp.float32)],
        compiler_params=_cparams("arbitrary"),
        name="route",
    )(idx)


LOCAL_ROWS = TOP_K * ROW_TILE + ROW_TILE
STRIP_BUFFERS = 3
assert LOCAL_ROWS >= TOP_K * ROW_TILE + N_EXPERTS * (SUBLANES - 1)


HI16 = 0xFFFF0000


def _pack_rows(x):
    half = x.shape[1] // 2
    lo = lax.bitcast_convert_type(x[:, :half], jnp.uint32) >> 16
    hi = lax.bitcast_convert_type(x[:, half:], jnp.uint32) & jnp.uint32(HI16)
    return lo | hi


def _unpack_rows(w):
    lo = lax.bitcast_convert_type(w << 16, jnp.float32)
    hi = lax.bitcast_convert_type(w & jnp.uint32(HI16), jnp.float32)
    return jnp.concatenate([lo, hi], axis=1).astype(jnp.bfloat16)


def _round_bf16(x):
    return x.astype(jnp.bfloat16).astype(jnp.float32)


def _for_each_strip(cnt_ref, base_ref, tile, fn):
    def body(e, off):
        c = cnt_ref[tile * N_EXPERTS + e]
        d = base_ref[tile * N_EXPERTS + e]

        @pl.when(c > 0)
        def _():
            fn(pl.multiple_of(off, SUBLANES), pl.multiple_of(d, SUBLANES), pl.multiple_of(c, SUBLANES))
        return off + c

    lax.fori_loop(0, N_EXPERTS, body, 0)


def _dispatch_kernel(cnt_ref, base_ref, tail_ref, pos_ref, h_ref, xs_ref, buf, zeros, sems, zsem):
    i = pl.program_id(0)
    last = pl.num_programs(0) - 1
    T = h_ref.shape[0]
    cur = i % STRIP_BUFFERS

    @pl.when(i == 0)
    def _():
        zeros[...] = jnp.zeros_like(zeros)

        def tail_copy(e):
            row = pl.multiple_of(jnp.maximum(tail_ref[e], 0), SUBLANES)
            return pltpu.make_async_copy(zeros, xs_ref.at[pl.ds(row, EXPERT_BLOCK), :], zsem)

        def spare_copy(blk):
            row = pl.multiple_of(blk * EXPERT_BLOCK, EXPERT_BLOCK)
            return pltpu.make_async_copy(zeros, xs_ref.at[pl.ds(row, EXPERT_BLOCK), :], zsem)

        first_spare = tail_ref[N_EXPERTS] // EXPERT_BLOCK
        for act in ("start", "wait"):
            def body(e, carry, act=act):
                @pl.when(tail_ref[e] >= 0)
                def _():
                    getattr(tail_copy(e), act)()
                return carry
            lax.fori_loop(0, N_EXPERTS, body, 0)

            def spare(blk, carry, act=act):
                getattr(spare_copy(blk), act)()
                return carry
            lax.fori_loop(first_spare, xs_ref.shape[0] // EXPERT_BLOCK, spare, 0)

    pos_t = pos_ref[...].astype(jnp.float32).T.astype(jnp.int32)
    rows = lax.broadcasted_iota(jnp.int32, (LOCAL_ROWS, T), 0)
    hit = rows == pos_t[0:1, :]
    for k in range(1, TOP_K):
        hit = jnp.logical_or(hit, rows == pos_t[k:k + 1, :])
    buf[cur] = _pack_rows(jnp.dot(hit.astype(jnp.bfloat16), h_ref[...].astype(jnp.bfloat16),
                                  preferred_element_type=jnp.float32))

    def strip(tile):
        slot = tile % STRIP_BUFFERS

        def make(local_row, slot_row, rows):
            return pltpu.make_async_copy(buf.at[slot, pl.ds(local_row, rows), :],
                                         xs_ref.at[pl.ds(slot_row, rows), :], sems.at[slot])
        return make

    def drain(tile):
        _for_each_strip(cnt_ref, base_ref, tile, lambda *a: strip(tile)(*a).wait())

    @pl.when(i >= STRIP_BUFFERS - 1)
    def _():
        drain(i - (STRIP_BUFFERS - 1))

    _for_each_strip(cnt_ref, base_ref, i, lambda *a: strip(i)(*a).start())

    @pl.when(i == last)
    def _():
        for back in range(STRIP_BUFFERS - 2, -1, -1):
            @pl.when(i - back >= 0)
            def _(back=back):
                drain(i - back)


def _dispatch(cnt_flat, base_flat, tail, pos, h1, n_slots):
    N, D = h1.shape
    T = ROW_TILE
    return pl.pallas_call(
        _dispatch_kernel,
        grid_spec=pltpu.PrefetchScalarGridSpec(
            num_scalar_prefetch=3,
            grid=(N // T,),
            in_specs=[pl.BlockSpec((T, LANES), lambda i, c, b, t: (i, 0)),
                      pl.BlockSpec((T, D), lambda i, c, b, t: (i, 0))],
            out_specs=pl.BlockSpec(memory_space=pl.ANY),
            scratch_shapes=[pltpu.VMEM((STRIP_BUFFERS, LOCAL_ROWS, D // 2), jnp.uint32),
                            pltpu.VMEM((EXPERT_BLOCK, D // 2), jnp.uint32),
                            pltpu.SemaphoreType.DMA((STRIP_BUFFERS,)),
                            pltpu.SemaphoreType.DMA(())]),
        out_shape=jax.ShapeDtypeStruct((n_slots, D // 2), jnp.uint32),
        compiler_params=_cparams("arbitrary"),
        name="dispatch",
    )(cnt_flat, base_flat, tail, pos, h1)


def _experts_kernel(blk_e_ref, n_used_ref, x_ref, wi_ref, bi_ref, wo_ref, bo_ref, y_ref, wi_bf, wo_bf):
    i = pl.program_id(0)
    used = i < n_used_ref[0]
    new_expert = jnp.logical_or(i == 0, blk_e_ref[i] != blk_e_ref[jnp.maximum(i - 1, 0)])

    @pl.when(jnp.logical_and(used, new_expert))
    def _():
        wi_bf[...] = wi_ref[0].astype(jnp.bfloat16)
        wo_bf[...] = wo_ref[0].astype(jnp.bfloat16)

    @pl.when(used)
    def _():
        hb = jnp.dot(_unpack_rows(x_ref[...]), wi_bf[...], preferred_element_type=jnp.float32) + bi_ref[0]
        De = D_MODEL
        x_glu = jnp.minimum(hb[:, :De], SWIGLU_LIMIT)
        x_lin = jnp.clip(hb[:, De:], -SWIGLU_LIMIT, SWIGLU_LIMIT)
        act = x_glu * jax.nn.sigmoid(SWIGLU_ALPHA * x_glu) * (x_lin + 1.0)
        y = jnp.dot(act.astype(jnp.bfloat16), wo_bf[...], preferred_element_type=jnp.float32) + bo_ref[0]
        y_ref[...] = _pack_rows(_round_bf16(y))

    @pl.when(jnp.logical_not(used))
    def _():
        y_ref[...] = jnp.zeros_like(y_ref)


def _experts(blk_e, n_used, xs, w_in, b_in, w_out, b_out):
    n_slots, half = xs.shape
    R = EXPERT_BLOCK
    E, D, F = w_in.shape
    return pl.pallas_call(
        _experts_kernel,
        grid_spec=pltpu.PrefetchScalarGridSpec(
            num_scalar_prefetch=2,
            grid=(n_slots // R,),
            in_specs=[pl.BlockSpec((R, half), lambda i, e, n: (jnp.minimum(i, n[0] - 1), 0)),
                      pl.BlockSpec((1, D, F), lambda i, e, n: (e[i], 0, 0)),
                      pl.BlockSpec((1, 1, F), lambda i, e, n: (e[i], 0, 0)),
                      pl.BlockSpec((1, F // 2, D), lambda i, e, n: (e[i], 0, 0)),
                      pl.BlockSpec((1, 1, D), lambda i, e, n: (e[i], 0, 0))],
            out_specs=pl.BlockSpec((R, half), lambda i, e, n: (i, 0)),
            scratch_shapes=[pltpu.VMEM((D, F), jnp.bfloat16), pltpu.VMEM((F // 2, D), jnp.bfloat16)]),
        out_shape=jax.ShapeDtypeStruct((n_slots, half), jnp.uint32),
        compiler_params=pltpu.CompilerParams(dimension_semantics=("arbitrary",),
                                             vmem_limit_bytes=EXPERTS_VMEM_LIMIT),
        name="experts",
    )(blk_e, n_used, xs, w_in, b_in, w_out, b_out)


def _combine_kernel(cnt_ref, base_ref, pos_ref, gate_ref, h_ref, g_ref, b_ref, ys_ref, o_ref, buf, sems):
    i = pl.program_id(0)
    last = pl.num_programs(0) - 1
    T = h_ref.shape[0]
    cur = i % STRIP_BUFFERS

    def strip(tile):
        slot = tile % STRIP_BUFFERS

        def make(local_row, slot_row, rows):
            return pltpu.make_async_copy(ys_ref.at[pl.ds(slot_row, rows), :],
                                         buf.at[slot, pl.ds(local_row, rows), :], sems.at[slot])
        return make

    def fetch(tile):
        _for_each_strip(cnt_ref, base_ref, tile, lambda *a: strip(tile)(*a).start())

    @pl.when(i == 0)
    def _():
        buf[...] = jnp.zeros_like(buf)
        for ahead in range(STRIP_BUFFERS - 1):
            @pl.when(ahead <= last)
            def _(ahead=ahead):
                fetch(ahead)

    @pl.when(i + (STRIP_BUFFERS - 1) <= last)
    def _():
        fetch(i + (STRIP_BUFFERS - 1))

    _for_each_strip(cnt_ref, base_ref, i, lambda *a: strip(i)(*a).wait())

    pos, gate = pos_ref[...], gate_ref[...]
    col = lax.broadcasted_iota(jnp.int32, (T, LOCAL_ROWS), 1)
    u = jnp.zeros(col.shape, jnp.float32)
    for k in range(TOP_K):
        u = jnp.where(col == pos[:, k:k + 1], gate[:, k:k + 1], u)
    u_hi, u_lo = _split2(u)
    y = _unpack_rows(buf[cur])
    dot = lambda a, b: jnp.dot(a, b, preferred_element_type=jnp.float32)
    ffn = dot(u_hi, y) + dot(u_lo, y)
    o_ref[...] = _layer_norm(DEEPNORM_ALPHA * h_ref[...] + ffn, g_ref[...], b_ref[...])


def _combine(cnt_flat, base_flat, pos, gate, h1, g, b, ys):
    N, D = h1.shape
    T = ROW_TILE
    row = lambda i, c, s: (i, 0)
    const = lambda i, c, s: (0, 0)
    return pl.pallas_call(
        _combine_kernel,
        grid_spec=pltpu.PrefetchScalarGridSpec(
            num_scalar_prefetch=2,
            grid=(N // T,),
            in_specs=[pl.BlockSpec((T, LANES), row), pl.BlockSpec((T, LANES), row), pl.BlockSpec((T, D), row),
                      pl.BlockSpec(g.shape, const), pl.BlockSpec(b.shape, const),
                      pl.BlockSpec(memory_space=pl.ANY)],
            out_specs=pl.BlockSpec((T, D), row),
            scratch_shapes=[pltpu.VMEM((STRIP_BUFFERS, LOCAL_ROWS, D // 2), jnp.uint32),
                            pltpu.SemaphoreType.DMA((STRIP_BUFFERS,))]),
        out_shape=jax.ShapeDtypeStruct((N, D), jnp.float32),
        compiler_params=_cparams("arbitrary"),
        name="combine",
    )(cnt_flat, base_flat, pos, gate, h1, g, b, ys)


def _rope_tables(S):
    t = np.arange(S)
    row, col = (t // GRID_W).astype(np.float64), (t % GRID_W).astype(np.float64)

    def block(half):
        inv = ROPE_THETA ** (-np.arange(half, dtype=np.float64) / half)
        ar, ac = row[:, None] * inv[None, :], col[:, None] * inv[None, :]
        cos = np.concatenate([np.cos(ar), np.cos(ar), np.cos(ac), np.cos(ac)], axis=1)
        sin = np.concatenate([-np.sin(ar), np.sin(ar), -np.sin(ac), np.sin(ac)], axis=1)
        return cos, sin

    ca64, sa64 = block(GQA_HEAD_DIM // 4)
    ca, sa = np.tile(ca64, (1, 2)), np.tile(sa64, (1, 2))
    cb32, sb32 = block(MLA_ROPE_DIM // 4)
    ones, zeros = np.ones((S, MLA_NOPE_DIM)), np.zeros((S, MLA_NOPE_DIM))
    cb = np.concatenate([ones, cb32, ones[:, :LANES - MLA_NOPE_DIM - MLA_ROPE_DIM]], axis=1)
    sb = np.concatenate([zeros, sb32, zeros[:, :LANES - MLA_NOPE_DIM - MLA_ROPE_DIM]], axis=1)
    return tuple(jnp.asarray(a, jnp.float32) for a in (ca, sa, cb, sb))


def _layer(h_in_is_x, x2, mem, B, S, ln_g, ln_b, w_in_proj, b_gate, gqa_q_norm, gqa_k_norm, mla_q_norm,
           mla_kv_norm, w_mla_qb, w_mla_kvb, w_mem_kv, w_br_gqa, w_br_mla, w_br_mem, w_out,
           ln1_g, ln1_b, w_router, b_router, w_exp_in, b_exp_in, w_exp_out, b_exp_out, ln2_g, ln2_b):
    del h_in_is_x
    bf = jnp.bfloat16
    N, D = x2.shape
    row2 = lambda a: a.reshape(1, -1)

    W = w_in_proj
    zc = lambda n: jnp.zeros((D, n), W.dtype)
    wa = jnp.concatenate([W[:, :OFF_KROPE], zc(HALF), W[:, OFF_KROPE:OFF_QM],
                          zc(LANES - HALF - MLA_ROPE_DIM), W[:, OFF_QM:OFF_GATE]], axis=1).astype(bf)
    wg = W[:, OFF_GATE:].astype(bf)
    qd = MLA_NOPE_DIM + MLA_ROPE_DIM
    wqb = jnp.pad(w_mla_qb.reshape(MLA_Q_LORA, MLA_HEADS, qd), ((0, 0), (0, 0), (0, LANES - qd)))
    partner_lane = np.arange(LANES) ^ (MLA_ROPE_DIM // 4)
    wqb = jnp.concatenate([wqb, wqb[:, :, partner_lane]], axis=1
                          ).reshape(MLA_Q_LORA, 2 * MLA_HEADS * LANES).astype(bf)
    kvb = w_mla_kvb.reshape(MLA_KV_LORA, MLA_HEADS, MLA_NOPE_DIM + MLA_V_DIM)
    wkvk = jnp.pad(kvb[:, :, :MLA_NOPE_DIM], ((0, 0), (0, 0), (0, LANES - MLA_NOPE_DIM))
                   ).reshape(MLA_KV_LORA, MLA_HEADS * LANES).astype(bf)
    wkvv = kvb[:, :, MLA_NOPE_DIM:].reshape(MLA_KV_LORA, MLA_HEADS * MLA_V_DIM).astype(bf)
    gq = row2(jnp.tile(gqa_q_norm, 2))
    gk = row2(jnp.tile(gqa_k_norm, 2))
    ca, sa, cb, sb = _rope_tables(S)

    mkt, mv = _mem_kv(mem, w_mem_kv.astype(bf))
    h, qat, ka, vat, qbt, kb, vbt, oc = _in_proj(
        x2, row2(ln_g), row2(ln_b), wa, gq, gk, row2(mla_q_norm), row2(mla_kv_norm),
        wqb, wkvk, wkvv, ca, sa, cb, sb, mkt, mv, B, S)

    oa = _attention(qat, ka.reshape(B, S, -1), vat, shared_kv=True)
    ob = _attention(qbt, kb.reshape(B, S, -1), vbt, shared_kv=False)

    wr = jnp.pad(w_router, ((0, 0), (0, LANES - N_EXPERTS)))
    wr3 = jnp.stack(_split2(wr))
    br = row2(jnp.pad(b_router, (0, LANES - N_EXPERTS)))
    h1, idx, gate = _mix_out(h, oa.reshape(N, -1), ob.reshape(N, -1), oc, wg, row2(b_gate),
                             w_br_gqa.astype(bf), w_br_mla.astype(bf), w_br_mem.astype(bf),
                             w_out.astype(bf), row2(ln1_g), row2(ln1_b), wr3, br)

    pos, carry_t, cnt_t, total = _route(idx)
    counts = total[0, :N_EXPERTS]
    R = EXPERT_BLOCK
    padded = (counts + R - 1) // R * R
    pad_end = jnp.cumsum(padded)
    pad_start = (pad_end - padded).astype(jnp.int32)
    n_tiles = N // ROW_TILE
    max_rows = N * TOP_K + n_tiles * N_EXPERTS * (SUBLANES - 1) + N_EXPERTS * (R - 1)
    n_blocks = -(-max_rows // R)
    blk_start = jnp.arange(n_blocks, dtype=jnp.int32) * R
    blk_e = jnp.sum((pad_end[None, :] <= blk_start[:, None]).astype(jnp.int32), axis=1)
    n_used = (pad_end[-1:] // R).astype(jnp.int32)
    last_e = jnp.max(jnp.where(blk_start < pad_end[-1], blk_e, 0))
    blk_e = jnp.minimum(blk_e, last_e).astype(jnp.int32)
    per_tile = lambda a: a.reshape(n_tiles, 8, LANES)[:, 0, :N_EXPERTS]
    cnt_flat = per_tile(cnt_t).reshape(-1)
    base_flat = (per_tile(carry_t) + pad_start[None, :]).reshape(-1)
    tail = jnp.concatenate([jnp.where(padded > 0, pad_end - R, -1), pad_end[-1:]]).astype(jnp.int32)

    xs = _dispatch(cnt_flat, base_flat, tail, pos, h1, n_blocks * R)
    ys = _experts(blk_e, n_used, xs, w_exp_in, b_exp_in[:, None, :], w_exp_out, b_exp_out[:, None, :])
    return _combine(cnt_flat, base_flat, pos, gate, h1, row2(ln2_g), row2(ln2_b), ys)


def kernel(x, mem, ln_in_g, ln_in_b, w_in_proj, b_gate, gqa_q_norm, gqa_k_norm, mla_q_norm, mla_kv_norm, w_mla_qb, w_mla_kvb, w_mem_kv, w_br_gqa, w_br_mla, w_br_mem, w_out, ln1_g, ln1_b, w_router, b_router, w_exp_in, b_exp_in, w_exp_out, b_exp_out, ln2_g, ln2_b):
    B, S, D = x.shape
    depth = w_in_proj.shape[0]
    assert depth == 1, "the input LayerNorm is fused into the first (only) layer's projection kernel"
    out = _layer(True, x.reshape(B * S, D), mem, B, S, ln_in_g, ln_in_b, w_in_proj[0], b_gate[0],
                 gqa_q_norm[0], gqa_k_norm[0], mla_q_norm[0], mla_kv_norm[0], w_mla_qb[0], w_mla_kvb[0],
                 w_mem_kv[0], w_br_gqa[0], w_br_mla[0], w_br_mem[0], w_out[0], ln1_g[0], ln1_b[0],
                 w_router[0], b_router[0], w_exp_in[0], b_exp_in[0], w_exp_out[0], b_exp_out[0],
                 ln2_g[0], ln2_b[0])
    return out.reshape(B, S, D)
```

```python
import functools

import jax
import jax.numpy as jnp
import numpy as np
from jax import lax
from jax.experimental import pallas as pl
from jax.experimental.pallas import tpu as pltpu

D_MODEL = 1024
MEM_LEN = 256
GRID_W = 64
ROPE_THETA = 10000.0
RMS_EPS = 1e-6
LN_EPS = 1e-5

GQA_HEADS = 8
GQA_KV_HEADS = 2
GQA_HEAD_DIM = 64
MLA_HEADS = 8
MLA_NOPE_DIM = 64
MLA_ROPE_DIM = 32
MLA_V_DIM = 64
MLA_Q_LORA = 384
MLA_KV_LORA = 256
MEM_HEADS = 4
MEM_HEAD_DIM = 128

N_EXPERTS = 32
TOP_K = 4
SWIGLU_LIMIT = 7.0
SWIGLU_ALPHA = 1.702
DEEPNORM_ALPHA = 2.0 ** 0.25

LANES = 128
SUBLANES = 8
BF16_SUBLANES = 16
HALF = LANES // 2
NEG_INF = float("-inf")
LOG2_E = 1.4426950408889634

OFF_QA, OFF_KA, OFF_VA, OFF_QLAT, OFF_KVLAT, OFF_KROPE, OFF_QM, OFF_GATE = (
    0, 512, 640, 768, 1152, 1408, 1440, 1952)
IN_PROJ_W = 5024
A_QA, A_KA, A_VA, A_QLAT, A_KVLAT, A_KROPE, A_QM, A_END = 0, 512, 640, 768, 1152, 1408, 1536, 2048
ROW_TILE = 256
IN_TILE = 512
IN_CHUNK = 256
MIX_TILE = 1024
MIX_CHUNK = 256
MIX_COLS = 256
ATTN_Q_TILE = 256
ATTN_K_CHUNK = 512
ROUTE_STEP_TILES = 4
EXPERT_BLOCK = 256
VMEM_LIMIT = 48 * 1024 * 1024
EXPERTS_VMEM_LIMIT = 56 * 1024 * 1024


def _cparams(*sem):
    return pltpu.CompilerParams(dimension_semantics=sem, vmem_limit_bytes=VMEM_LIMIT)


def _lane_iota(shape):
    return lax.broadcasted_iota(jnp.int32, shape, len(shape) - 1)


def _layer_norm(x, g, b):
    mu = jnp.mean(x, axis=-1, keepdims=True)
    xc = x - mu
    var = jnp.mean(xc * xc, axis=-1, keepdims=True)
    return xc * lax.rsqrt(var + LN_EPS) * g + b


def _bdot(a, b):
    return jnp.dot(a.astype(jnp.bfloat16), b.astype(jnp.bfloat16), preferred_element_type=jnp.float32)


def _rope(x, cos, sin_signed, pair):
    lane = _lane_iota(x.shape)
    fwd = pltpu.roll(x, LANES - pair, 1)
    bwd = pltpu.roll(x, pair, 1)
    partner = jnp.where((lane % (2 * pair)) < pair, fwd, bwd)
    return x * cos + partner * sin_signed


def _half_rms_scale(x):
    lane = _lane_iota(x.shape)
    sq = x * x
    lo = jnp.sum(jnp.where(lane < HALF, sq, 0.0), axis=-1, keepdims=True)
    hi = jnp.sum(jnp.where(lane >= HALF, sq, 0.0), axis=-1, keepdims=True)
    inv = 1.0 / GQA_HEAD_DIM
    return jnp.where(lane < HALF, lax.rsqrt(lo * inv + RMS_EPS), lax.rsqrt(hi * inv + RMS_EPS))


def _softmax_rows(s):
    m = jnp.max(s, axis=-1, keepdims=True)
    p = jnp.exp(s - m)
    return p, jnp.sum(p, axis=-1, keepdims=True)


def _mem_kv_kernel(mem_ref, w_ref, kt_ref, v_ref):
    kv = _bdot(mem_ref[0], w_ref[...])
    width = MEM_HEADS * MEM_HEAD_DIM
    kt_ref[0] = kv[:, :width].T.astype(jnp.bfloat16)
    v_ref[0] = kv[:, width:].astype(jnp.bfloat16)


def _mem_kv(mem, w_mem_kv):
    B, M, D = mem.shape
    width = MEM_HEADS * MEM_HEAD_DIM
    return pl.pallas_call(
        _mem_kv_kernel,
        grid=(B,),
        in_specs=[pl.BlockSpec((1, M, D), lambda b: (b, 0, 0)),
                  pl.BlockSpec((D, 2 * width), lambda b: (0, 0))],
        out_specs=[pl.BlockSpec((1, width, M), lambda b: (b, 0, 0)),
                   pl.BlockSpec((1, M, width), lambda b: (b, 0, 0))],
        out_shape=[jax.ShapeDtypeStruct((B, width, M), jnp.bfloat16),
                   jax.ShapeDtypeStruct((B, M, width), jnp.bfloat16)],
        compiler_params=_cparams("arbitrary"),
        name="mem_kv",
    )(mem, w_mem_kv)


def _in_proj_kernel(x_ref, lng_ref, lnb_ref, wa_ref, gq_ref, gk_ref, gql_ref, gkvl_ref,
                    wqb_ref, wkvk_ref, wkvv_ref, ca_ref, sa_ref, cb_ref, sb_ref, mkt_ref, mv_ref,
                    h_ref, qa_ref, ka_ref, va_ref, qb_ref, kb_ref, vb_ref, oc_ref):
    lane = _lane_iota((IN_CHUNK, LANES))
    lo_mask = lane < HALF
    q_scale = GQA_HEAD_DIM ** -0.5 * LOG2_E
    qb_scale = (MLA_NOPE_DIM + MLA_ROPE_DIM) ** -0.5 * LOG2_E
    per_group = GQA_HEADS // GQA_KV_HEADS
    gq, gk = gq_ref[...], gk_ref[...]

    def project(r):
        h = _layer_norm(x_ref[r, :], lng_ref[...], lnb_ref[...])
        h_ref[r, :] = h
        return _bdot(h, wa_ref[...])

    def finish(r, proj):
        ca, sa, cb, sb = ca_ref[r, :], sa_ref[r, :], cb_ref[r, :], sb_ref[r, :]

        for c in range(GQA_HEADS // 2):
            slab = proj[:, A_QA + c * LANES:A_QA + (c + 1) * LANES]
            slab = _rope(slab * _half_rms_scale(slab) * gq, ca, sa, 16) * q_scale
            swapped = pltpu.roll(slab, HALF, 1)
            for half in range(2):
                head = 2 * c + half
                group_lo = (head // per_group) == 0
                src = slab if (half == 0) == group_lo else swapped
                keep = lo_mask if group_lo else jnp.logical_not(lo_mask)
                qa_ref[0, head * LANES:(head + 1) * LANES, r] = jnp.where(keep, src, 0.0).astype(jnp.bfloat16).T

        ka = proj[:, A_KA:A_KA + LANES]
        ka_ref[r, :] = _rope(ka * _half_rms_scale(ka) * gk, ca, sa, 16).astype(jnp.bfloat16)
        va_ref[0, :, r] = proj[:, A_VA:A_VA + LANES].astype(jnp.bfloat16).T

        ql = proj[:, A_QLAT:A_KVLAT]
        ql = ql * lax.rsqrt(jnp.mean(ql * ql, axis=-1, keepdims=True) + RMS_EPS) * gql_ref[...]
        qm = _bdot(ql, wqb_ref[...])
        kvl = proj[:, A_KVLAT:A_KROPE]
        kvl = kvl * lax.rsqrt(jnp.mean(kvl * kvl, axis=-1, keepdims=True) + RMS_EPS) * gkvl_ref[...]
        kn = _bdot(kvl, wkvk_ref[...])
        vb = _bdot(kvl, wkvv_ref[...])
        for c in range(vb.shape[1] // LANES):
            sl = slice(c * LANES, (c + 1) * LANES)
            vb_ref[0, sl, r] = vb[:, sl].astype(jnp.bfloat16).T
        k_pe = _rope(proj[:, A_KROPE:A_QM], cb, sb, 8)
        for hd in range(MLA_HEADS):
            sl = slice(hd * LANES, (hd + 1) * LANES)
            partner = qm[:, (MLA_HEADS + hd) * LANES:(MLA_HEADS + hd + 1) * LANES]
            qb_ref[0, sl, r] = ((qm[:, sl] * cb + partner * sb) * qb_scale).astype(jnp.bfloat16).T
            kb_ref[r, sl] = (kn[:, sl] + k_pe).astype(jnp.bfloat16)

        qc = proj[:, A_QM:A_END] * (MEM_HEAD_DIM ** -0.5)
        for hd in range(MEM_HEADS):
            sl = slice(hd * MEM_HEAD_DIM, (hd + 1) * MEM_HEAD_DIM)
            p, l = _softmax_rows(_bdot(qc[:, sl], mkt_ref[0, sl, :]))
            oc_ref[r, sl] = (_bdot(p, mv_ref[0, :, sl]) / l).astype(jnp.bfloat16)

    chunks = [slice(s * IN_CHUNK, (s + 1) * IN_CHUNK) for s in range(x_ref.shape[0] // IN_CHUNK)]
    pending = None
    for r in chunks:
        proj = project(r)
        if pending is not None:
            finish(*pending)
        pending = (r, proj)
    finish(*pending)


def _in_proj(x2, ln_g, ln_b, wa, gq, gk, gql, gkvl, wqb, wkvk, wkvv, ca, sa, cb, sb, mkt, mv, B, S):
    N, D = x2.shape
    T = IN_TILE
    tiles_per_seq = S // T
    row = lambda i: (i, 0)
    const = lambda i: (0, 0)
    pos = lambda i: (i % tiles_per_seq, 0)
    batch3 = lambda i: (i // tiles_per_seq, 0, 0)
    kt_map = lambda i: (i // tiles_per_seq, 0, i % tiles_per_seq)
    full = lambda a: pl.BlockSpec(a.shape, const)
    return pl.pallas_call(
        _in_proj_kernel,
        grid=(N // T,),
        in_specs=[pl.BlockSpec((T, D), row), full(ln_g), full(ln_b), full(wa), full(gq), full(gk),
                  full(gql), full(gkvl), full(wqb), full(wkvk), full(wkvv),
                  pl.BlockSpec((T, LANES), pos), pl.BlockSpec((T, LANES), pos),
                  pl.BlockSpec((T, LANES), pos), pl.BlockSpec((T, LANES), pos),
                  pl.BlockSpec((1,) + mkt.shape[1:], batch3), pl.BlockSpec((1,) + mv.shape[1:], batch3)],
        out_specs=[pl.BlockSpec((T, D), row),
                   pl.BlockSpec((1, GQA_HEADS * LANES, T), kt_map),
                   pl.BlockSpec((T, LANES), row),
                   pl.BlockSpec((1, LANES, T), kt_map),
                   pl.BlockSpec((1, MLA_HEADS * LANES, T), kt_map),
                   pl.BlockSpec((T, MLA_HEADS * LANES), row),
                   pl.BlockSpec((1, MLA_HEADS * MLA_V_DIM, T), kt_map),
                   pl.BlockSpec((T, MEM_HEADS * MEM_HEAD_DIM), row)],
        out_shape=[jax.ShapeDtypeStruct((N, D), jnp.float32),
                   jax.ShapeDtypeStruct((B, GQA_HEADS * LANES, S), jnp.bfloat16),
                   jax.ShapeDtypeStruct((N, LANES), jnp.bfloat16),
                   jax.ShapeDtypeStruct((B, LANES, S), jnp.bfloat16),
                   jax.ShapeDtypeStruct((B, MLA_HEADS * LANES, S), jnp.bfloat16),
                   jax.ShapeDtypeStruct((N, MLA_HEADS * LANES), jnp.bfloat16),
                   jax.ShapeDtypeStruct((B, MLA_HEADS * MLA_V_DIM, S), jnp.bfloat16),
                   jax.ShapeDtypeStruct((N, MEM_HEADS * MEM_HEAD_DIM), jnp.bfloat16)],
        compiler_params=_cparams("arbitrary"),
        name="in_proj",
    )(x2, ln_g, ln_b, wa, gq, gk, gql, gkvl, wqb, wkvk, wkvv, ca, sa, cb, sb, mkt, mv)


def _attention_kernel(qt_ref, k_ref, vt_ref, o_ref, s_even, s_odd, *, shared_kv):
    t = pl.program_id(0)

    @pl.when(t == 0)
    def _():
        s_odd[...] = jnp.zeros_like(s_odd)

    def step(s_new, s_old):
        S, TQ = s_new.shape[1:]
        m = [jnp.full((1, TQ), NEG_INF, jnp.float32)] * 2
        ones = jnp.ones((BF16_SUBLANES, ATTN_K_CHUNK), jnp.bfloat16)
        acc = [jnp.zeros((HALF + BF16_SUBLANES, TQ), jnp.float32)] * 2
        for c in range(S // ATTN_K_CHUNK):
            rows = slice(c * ATTN_K_CHUNK, (c + 1) * ATTN_K_CHUNK)
            for hd in range(2):
                qt = qt_ref[0, hd * LANES:(hd + 1) * LANES, :]
                k = k_ref[0, rows, :] if shared_kv else k_ref[0, rows, hd * LANES:(hd + 1) * LANES]
                s_new[hd, rows, :] = jnp.dot(k, qt, preferred_element_type=jnp.float32)
                vt = vt_ref[0, :, rows] if shared_kv else vt_ref[0, hd * HALF:(hd + 1) * HALF, rows]
                vt = jnp.concatenate([vt, ones], axis=0)
                sc = s_old[hd, rows, :]
                m_new = jnp.maximum(m[hd], jnp.max(sc, axis=0, keepdims=True))
                p = jnp.exp2(sc - m_new).astype(jnp.bfloat16)
                corr = jnp.exp2(m[hd] - m_new)
                acc[hd] = acc[hd] * corr + jnp.dot(vt, p, preferred_element_type=jnp.float32)
                m[hd] = m_new
        outs = [acc[hd][:HALF] / acc[hd][HALF:HALF + 1] for hd in range(2)]
        o_ref[0] = jnp.concatenate(outs, axis=0).T.astype(jnp.bfloat16)

    @pl.when(t % 2 == 0)
    def _():
        step(s_even, s_odd)

    @pl.when(t % 2 == 1)
    def _():
        step(s_odd, s_even)


def _attention(qt, k, vt, *, shared_kv):
    B, qh, S = qt.shape
    pairs = qh // (2 * LANES)
    TQ = ATTN_Q_TILE
    nq = S // TQ
    items = B * pairs * nq
    pairs_per_group = pairs // GQA_KV_HEADS

    def item(t):
        return t // (pairs * nq), (t // nq) % pairs, t % nq

    score_item = lambda t: item(jnp.minimum(t, items - 1))
    finish_item = lambda t: item(jnp.maximum(t - 1, 0))

    def qt_map(t):
        b, j, i = score_item(t)
        return b, j, i

    def k_map(t):
        b, j, _ = score_item(t)
        return (b, 0, 0) if shared_kv else (b, 0, j)

    def vt_map(t):
        b, j, _ = finish_item(t)
        return (b, j // pairs_per_group, 0) if shared_kv else (b, j, 0)

    def o_map(t):
        b, j, i = finish_item(t)
        return b, i, j

    k_spec = pl.BlockSpec((1, S, LANES if shared_kv else 2 * LANES), k_map)
    vt_spec = pl.BlockSpec((1, HALF if shared_kv else LANES, S), vt_map)
    return pl.pallas_call(
        functools.partial(_attention_kernel, shared_kv=shared_kv),
        grid=(items + 1,),
        in_specs=[pl.BlockSpec((1, 2 * LANES, TQ), qt_map), k_spec, vt_spec],
        out_specs=pl.BlockSpec((1, TQ, LANES), o_map),
        out_shape=jax.ShapeDtypeStruct((B, S, pairs * LANES), jnp.bfloat16),
        scratch_shapes=[pltpu.VMEM((2, S, TQ), jnp.float32), pltpu.VMEM((2, S, TQ), jnp.float32)],
        compiler_params=_cparams("arbitrary"),
        name="attention_gqa" if shared_kv else "attention_mla",
    )(qt, k, vt)


def _split2(x):
    hi = x.astype(jnp.bfloat16)
    return hi, (x - hi.astype(jnp.float32)).astype(jnp.bfloat16)


def _mix_out_kernel(h_ref, oa_ref, ob_ref, oc_ref, wg_ref, bg_ref, wa_ref, wb_ref, wc_ref, wo_ref,
                    g_ref, b_ref, wr_ref, br_ref, h1_ref, idx_ref, gate_ref):
    D = D_MODEL
    dot = lambda a, b: jnp.dot(a, b, preferred_element_type=jnp.float32)

    def mix(r):
        hb = h_ref[r, :].astype(jnp.bfloat16)
        merged = []
        for n in range(D // MIX_COLS):
            acc = None
            for i, (o_ref, w_ref) in enumerate(((oa_ref, wa_ref), (ob_ref, wb_ref), (oc_ref, wc_ref))):
                gate_cols = slice(i * D + n * MIX_COLS, i * D + (n + 1) * MIX_COLS)
                gate = jax.nn.sigmoid(dot(hb, wg_ref[:, gate_cols]) + bg_ref[:, gate_cols])
                term = gate * dot(o_ref[r, :], w_ref[:, n * MIX_COLS:(n + 1) * MIX_COLS])
                acc = term if acc is None else acc + term
            merged.append(acc.astype(jnp.bfloat16))
        return dot(jnp.concatenate(merged, axis=1), wo_ref[...])

    def route(r, mixed):
        h1 = _layer_norm(DEEPNORM_ALPHA * h_ref[r, :] + mixed, g_ref[...], b_ref[...])
        h1_ref[r, :] = h1
        h_hi, h_lo = _split2(h1)
        w_hi, w_lo = wr_ref[0], wr_ref[1]
        logits = dot(h_hi, w_hi) + (dot(h_hi, w_lo) + dot(h_lo, w_hi)) + br_ref[...]
        lane = _lane_iota(logits.shape)
        logits = jnp.where(lane < N_EXPERTS, logits, NEG_INF)
        idx_out = jnp.zeros(logits.shape, jnp.int32)
        val_out = jnp.zeros(logits.shape, jnp.float32)
        top = None
        for k in range(TOP_K):
            m = jnp.max(logits, axis=-1, keepdims=True)
            idx = jnp.min(jnp.where(logits == m, lane, LANES), axis=-1, keepdims=True)
            logits = jnp.where(lane == idx, NEG_INF, logits)
            top = m if top is None else top
            idx_out = jnp.where(lane == k, idx, idx_out)
            val_out = jnp.where(lane == k, jnp.exp(m - top), val_out)
        idx_ref[r, :] = idx_out
        gate_ref[r, :] = val_out / jnp.sum(val_out, axis=-1, keepdims=True)

    pending = None
    for s in range(h_ref.shape[0] // MIX_CHUNK):
        r = slice(s * MIX_CHUNK, (s + 1) * MIX_CHUNK)
        mixed = mix(r)
        if pending is not None:
            route(*pending)
        pending = (r, mixed)
    route(*pending)


def _mix_out(h, oa, ob, oc, wg, bg, wa, wb, wc, wo, g, b, wr3, br):
    N, D = h.shape
    T = MIX_TILE
    row = lambda i: (i, 0)
    full = lambda a: pl.BlockSpec(a.shape, lambda i: (0,) * a.ndim)
    return pl.pallas_call(
        _mix_out_kernel,
        grid=(N // T,),
        in_specs=[pl.BlockSpec((T, D), row), pl.BlockSpec((T, oa.shape[1]), row),
                  pl.BlockSpec((T, ob.shape[1]), row), pl.BlockSpec((T, oc.shape[1]), row),
                  full(wg), full(bg), full(wa), full(wb), full(wc), full(wo), full(g), full(b),
                  full(wr3), full(br)],
        out_specs=[pl.BlockSpec((T, D), row), pl.BlockSpec((T, LANES), row), pl.BlockSpec((T, LANES), row)],
        out_shape=[jax.ShapeDtypeStruct((N, D), jnp.float32),
                   jax.ShapeDtypeStruct((N, LANES), jnp.int32),
                   jax.ShapeDtypeStruct((N, LANES), jnp.float32)],
        compiler_params=_cparams("arbitrary"),
        name="mix_out",
    )(h, oa, ob, oc, wg, bg, wa, wb, wc, wo, g, b, wr3, br)


def _expert_rows(idx, rows_shape):
    T = idx.shape[0]
    lane = _lane_iota(idx.shape)
    sel = [lane == idx[:, k:k + 1] for k in range(TOP_K)]
    onehot = sum(s.astype(jnp.float32) for s in sel)
    r = lax.broadcasted_iota(jnp.int32, (T, T), 0)
    c = lax.broadcasted_iota(jnp.int32, (T, T), 1)
    before = (c < r).astype(jnp.bfloat16)
    prefix = jnp.dot(before, onehot.astype(jnp.bfloat16), preferred_element_type=jnp.float32)
    cnt = jnp.broadcast_to(jnp.sum(onehot, axis=0, keepdims=True), rows_shape)
    cnt = jnp.floor((cnt + (SUBLANES - 1)) * (1.0 / SUBLANES)) * SUBLANES
    er = lax.broadcasted_iota(jnp.int32, (LANES, LANES), 0)
    ec = lax.broadcasted_iota(jnp.int32, (LANES, LANES), 1)
    local_start = jnp.dot(cnt.astype(jnp.bfloat16), (er < ec).astype(jnp.bfloat16),
                          preferred_element_type=jnp.float32)
    target = prefix + local_start[0:1, :]
    out = jnp.zeros(idx.shape, jnp.int32)
    for k in range(TOP_K):
        pk = jnp.sum(jnp.where(sel[k], target, 0.0), axis=-1, keepdims=True)
        out = jnp.where(lane == k, pk.astype(jnp.int32), out)
    return out, cnt


def _route_kernel(idx_ref, pos_ref, carry_out_ref, cnt_out_ref, total_ref, carry_ref):
    @pl.when(pl.program_id(0) == 0)
    def _():
        carry_ref[...] = jnp.zeros_like(carry_ref)

    for s in range(idx_ref.shape[0] // ROW_TILE):
        rows = slice(s * ROW_TILE, (s + 1) * ROW_TILE)
        stats = slice(s * SUBLANES, (s + 1) * SUBLANES)
        pos, cnt = _expert_rows(idx_ref[rows, :], carry_ref.shape)
        pos_ref[rows, :] = pos
        carry_out_ref[stats, :] = carry_ref[...].astype(jnp.int32)
        cnt_out_ref[stats, :] = cnt.astype(jnp.int32)
        carry_ref[...] = carry_ref[...] + cnt
    total_ref[...] = carry_ref[...].astype(jnp.int32)


def _route(idx):
    N = idx.shape[0]
    T = ROUTE_STEP_TILES * ROW_TILE
    n_tiles = N // ROW_TILE
    tile8 = pl.BlockSpec((ROUTE_STEP_TILES * SUBLANES, LANES), lambda i: (i, 0))
    return pl.pallas_call(
        _route_kernel,
        grid=(N // T,),
        in_specs=[pl.BlockSpec((T, LANES), lambda i: (i, 0))],
        out_specs=[pl.BlockSpec((T, LANES), lambda i: (i, 0)), tile8, tile8,
                   pl.BlockSpec((SUBLANES, LANES), lambda i: (0, 0))],
        out_shape=[jax.ShapeDtypeStruct((N, LANES), jnp.int32),
                   jax.ShapeDtypeStruct((n_tiles * SUBLANES, LANES), jnp.int32),
                   jax.ShapeDtypeStruct((n_tiles * SUBLANES, LANES), jnp.int32),
                   jax.ShapeDtypeStruct((SUBLANES, LANES), jnp.int32)],
        scratch_shapes=[pltpu.VMEM((SUBLANES, LANES), jnp.float32)],
        compiler_params=_cparams("arbitrary"),
        name="route",
    )(idx)


LOCAL_ROWS = TOP_K * ROW_TILE + ROW_TILE
STRIP_BUFFERS = 3
assert LOCAL_ROWS >= TOP_K * ROW_TILE + N_EXPERTS * (SUBLANES - 1)


HI16 = 0xFFFF0000


def _pack_rows(x):
    half = x.shape[1] // 2
    lo = lax.bitcast_convert_type(x[:, :half], jnp.uint32) >> 16
    hi = lax.bitcast_convert_type(x[:, half:], jnp.uint32) & jnp.uint32(HI16)
    return lo | hi


def _unpack_rows(w):
    lo = lax.bitcast_convert_type(w << 16, jnp.float32)
    hi = lax.bitcast_convert_type(w & jnp.uint32(HI16), jnp.float32)
    return jnp.concatenate([lo, hi], axis=1).astype(jnp.bfloat16)


def _round_bf16(x):
    return x.astype(jnp.bfloat16).astype(jnp.float32)


def _for_each_strip(cnt_ref, base_ref, tile, fn):
    def body(e, off):
        c = cnt_ref[tile * N_EXPERTS + e]
        d = base_ref[tile * N_EXPERTS + e]

        @pl.when(c > 0)
        def _():
            fn(pl.multiple_of(off, SUBLANES), pl.multiple_of(d, SUBLANES), pl.multiple_of(c, SUBLANES))
        return off + c

    lax.fori_loop(0, N_EXPERTS, body, 0)


def _dispatch_kernel(cnt_ref, base_ref, tail_ref, pos_ref, h_ref, xs_ref, buf, zeros, sems, zsem):
    i = pl.program_id(0)
    last = pl.num_programs(0) - 1
    T = h_ref.shape[0]
    cur = i % STRIP_BUFFERS

    @pl.when(i == 0)
    def _():
        zeros[...] = jnp.zeros_like(zeros)

        def tail_copy(e):
            row = pl.multiple_of(jnp.maximum(tail_ref[e], 0), SUBLANES)
            return pltpu.make_async_copy(zeros, xs_ref.at[pl.ds(row, EXPERT_BLOCK), :], zsem)

        def spare_copy(blk):
            row = pl.multiple_of(blk * EXPERT_BLOCK, EXPERT_BLOCK)
            return pltpu.make_async_copy(zeros, xs_ref.at[pl.ds(row, EXPERT_BLOCK), :], zsem)

        first_spare = tail_ref[N_EXPERTS] // EXPERT_BLOCK
        for act in ("start", "wait"):
            def body(e, carry, act=act):
                @pl.when(tail_ref[e] >= 0)
                def _():
                    getattr(tail_copy(e), act)()
                return carry
            lax.fori_loop(0, N_EXPERTS, body, 0)

            def spare(blk, carry, act=act):
                getattr(spare_copy(blk), act)()
                return carry
            lax.fori_loop(first_spare, xs_ref.shape[0] // EXPERT_BLOCK, spare, 0)

    pos_t = pos_ref[...].astype(jnp.float32).T.astype(jnp.int32)
    rows = lax.broadcasted_iota(jnp.int32, (LOCAL_ROWS, T), 0)
    hit = rows == pos_t[0:1, :]
    for k in range(1, TOP_K):
        hit = jnp.logical_or(hit, rows == pos_t[k:k + 1, :])
    buf[cur] = _pack_rows(jnp.dot(hit.astype(jnp.bfloat16), h_ref[...].astype(jnp.bfloat16),
                                  preferred_element_type=jnp.float32))

    def strip(tile):
        slot = tile % STRIP_BUFFERS

        def make(local_row, slot_row, rows):
            return pltpu.make_async_copy(buf.at[slot, pl.ds(local_row, rows), :],
                                         xs_ref.at[pl.ds(slot_row, rows), :], sems.at[slot])
        return make

    def drain(tile):
        _for_each_strip(cnt_ref, base_ref, tile, lambda *a: strip(tile)(*a).wait())

    @pl.when(i >= STRIP_BUFFERS - 1)
    def _():
        drain(i - (STRIP_BUFFERS - 1))

    _for_each_strip(cnt_ref, base_ref, i, lambda *a: strip(i)(*a).start())

    @pl.when(i == last)
    def _():
        for back in range(STRIP_BUFFERS - 2, -1, -1):
            @pl.when(i - back >= 0)
            def _(back=back):
                drain(i - back)


def _dispatch(cnt_flat, base_flat, tail, pos, h1, n_slots):
    N, D = h1.shape
    T = ROW_TILE
    return pl.pallas_call(
        _dispatch_kernel,
        grid_spec=pltpu.PrefetchScalarGridSpec(
            num_scalar_prefetch=3,
            grid=(N // T,),
            in_specs=[pl.BlockSpec((T, LANES), lambda i, c, b, t: (i, 0)),
                      pl.BlockSpec((T, D), lambda i, c, b, t: (i, 0))],
            out_specs=pl.BlockSpec(memory_space=pl.ANY),
            scratch_shapes=[pltpu.VMEM((STRIP_BUFFERS, LOCAL_ROWS, D // 2), jnp.uint32),
                            pltpu.VMEM((EXPERT_BLOCK, D // 2), jnp.uint32),
                            pltpu.SemaphoreType.DMA((STRIP_BUFFERS,)),
                            pltpu.SemaphoreType.DMA(())]),
        out_shape=jax.ShapeDtypeStruct((n_slots, D // 2), jnp.uint32),
        compiler_params=_cparams("arbitrary"),
        name="dispatch",
    )(cnt_flat, base_flat, tail, pos, h1)


def _experts_kernel(blk_e_ref, n_used_ref, x_ref, wi_ref, bi_ref, wo_ref, bo_ref, y_ref, wi_bf, wo_bf):
    i = pl.program_id(0)
    used = i < n_used_ref[0]
    new_expert = jnp.logical_or(i == 0, blk_e_ref[i] != blk_e_ref[jnp.maximum(i - 1, 0)])

    @pl.when(jnp.logical_and(used, new_expert))
    def _():
        wi_bf[...] = wi_ref[0].astype(jnp.bfloat16)
        wo_bf[...] = wo_ref[0].astype(jnp.bfloat16)

    @pl.when(used)
    def _():
        hb = jnp.dot(_unpack_rows(x_ref[...]), wi_bf[...], preferred_element_type=jnp.float32) + bi_ref[0]
        De = D_MODEL
        x_glu = jnp.minimum(hb[:, :De], SWIGLU_LIMIT)
        x_lin = jnp.clip(hb[:, De:], -SWIGLU_LIMIT, SWIGLU_LIMIT)
        act = x_glu * jax.nn.sigmoid(SWIGLU_ALPHA * x_glu) * (x_lin + 1.0)
        y = jnp.dot(act.astype(jnp.bfloat16), wo_bf[...], preferred_element_type=jnp.float32) + bo_ref[0]
        y_ref[...] = _pack_rows(_round_bf16(y))

    @pl.when(jnp.logical_not(used))
    def _():
        y_ref[...] = jnp.zeros_like(y_ref)


def _experts(blk_e, n_used, xs, w_in, b_in, w_out, b_out):
    n_slots, half = xs.shape
    R = EXPERT_BLOCK
    E, D, F = w_in.shape
    return pl.pallas_call(
        _experts_kernel,
        grid_spec=pltpu.PrefetchScalarGridSpec(
            num_scalar_prefetch=2,
            grid=(n_slots // R,),
            in_specs=[pl.BlockSpec((R, half), lambda i, e, n: (jnp.minimum(i, n[0] - 1), 0)),
                      pl.BlockSpec((1, D, F), lambda i, e, n: (e[i], 0, 0)),
                      pl.BlockSpec((1, 1, F), lambda i, e, n: (e[i], 0, 0)),
                      pl.BlockSpec((1, F // 2, D), lambda i, e, n: (e[i], 0, 0)),
                      pl.BlockSpec((1, 1, D), lambda i, e, n: (e[i], 0, 0))],
            out_specs=pl.BlockSpec((R, half), lambda i, e, n: (i, 0)),
            scratch_shapes=[pltpu.VMEM((D, F), jnp.bfloat16), pltpu.VMEM((F // 2, D), jnp.bfloat16)]),
        out_shape=jax.ShapeDtypeStruct((n_slots, half), jnp.uint32),
        compiler_params=pltpu.CompilerParams(dimension_semantics=("arbitrary",),
                                             vmem_limit_bytes=EXPERTS_VMEM_LIMIT),
        name="experts",
    )(blk_e, n_used, xs, w_in, b_in, w_out, b_out)


def _combine_kernel(cnt_ref, base_ref, pos_ref, gate_ref, h_ref, g_ref, b_ref, ys_ref, o_ref, buf, sems):
    i = pl.program_id(0)
    last = pl.num_programs(0) - 1
    T = h_ref.shape[0]
    cur = i % STRIP_BUFFERS

    def strip(tile):
        slot = tile % STRIP_BUFFERS

        def make(local_row, slot_row, rows):
            return pltpu.make_async_copy(ys_ref.at[pl.ds(slot_row, rows), :],
                                         buf.at[slot, pl.ds(local_row, rows), :], sems.at[slot])
        return make

    def fetch(tile):
        _for_each_strip(cnt_ref, base_ref, tile, lambda *a: strip(tile)(*a).start())

    @pl.when(i == 0)
    def _():
        buf[...] = jnp.zeros_like(buf)
        for ahead in range(STRIP_BUFFERS - 1):
            @pl.when(ahead <= last)
            def _(ahead=ahead):
                fetch(ahead)

    @pl.when(i + (STRIP_BUFFERS - 1) <= last)
    def _():
        fetch(i + (STRIP_BUFFERS - 1))

    _for_each_strip(cnt_ref, base_ref, i, lambda *a: strip(i)(*a).wait())

    pos, gate = pos_ref[...], gate_ref[...]
    col = lax.broadcasted_iota(jnp.int32, (T, LOCAL_ROWS), 1)
    u = jnp.zeros(col.shape, jnp.float32)
    for k in range(TOP_K):
        u = jnp.where(col == pos[:, k:k + 1], gate[:, k:k + 1], u)
    u_hi, u_lo = _split2(u)
    y = _unpack_rows(buf[cur])
    dot = lambda a, b: jnp.dot(a, b, preferred_element_type=jnp.float32)
    ffn = dot(u_hi, y) + dot(u_lo, y)
    o_ref[...] = _layer_norm(DEEPNORM_ALPHA * h_ref[...] + ffn, g_ref[...], b_ref[...])


def _combine(cnt_flat, base_flat, pos, gate, h1, g, b, ys):
    N, D = h1.shape
    T = ROW_TILE
    row = lambda i, c, s: (i, 0)
    const = lambda i, c, s: (0, 0)
    return pl.pallas_call(
        _combine_kernel,
        grid_spec=pltpu.PrefetchScalarGridSpec(
            num_scalar_prefetch=2,
            grid=(N // T,),
            in_specs=[pl.BlockSpec((T, LANES), row), pl.BlockSpec((T, LANES), row), pl.BlockSpec((T, D), row),
                      pl.BlockSpec(g.shape, const), pl.BlockSpec(b.shape, const),
                      pl.BlockSpec(memory_space=pl.ANY)],
            out_specs=pl.BlockSpec((T, D), row),
            scratch_shapes=[pltpu.VMEM((STRIP_BUFFERS, LOCAL_ROWS, D // 2), jnp.uint32),
                            pltpu.SemaphoreType.DMA((STRIP_BUFFERS,))]),
        out_shape=jax.ShapeDtypeStruct((N, D), jnp.float32),
        compiler_params=_cparams("arbitrary"),
        name="combine",
    )(cnt_flat, base_flat, pos, gate, h1, g, b, ys)


def _rope_tables(S):
    t = np.arange(S)
    row, col = (t // GRID_W).astype(np.float64), (t % GRID_W).astype(np.float64)

    def block(half):
        inv = ROPE_THETA ** (-np.arange(half, dtype=np.float64) / half)
        ar, ac = row[:, None] * inv[None, :], col[:, None] * inv[None, :]
        cos = np.concatenate([np.cos(ar), np.cos(ar), np.cos(ac), np.cos(ac)], axis=1)
        sin = np.concatenate([-np.sin(ar), np.sin(ar), -np.sin(ac), np.sin(ac)], axis=1)
        return cos, sin

    ca64, sa64 = block(GQA_HEAD_DIM // 4)
    ca, sa = np.tile(ca64, (1, 2)), np.tile(sa64, (1, 2))
    cb32, sb32 = block(MLA_ROPE_DIM // 4)
    ones, zeros = np.ones((S, MLA_NOPE_DIM)), np.zeros((S, MLA_NOPE_DIM))
    cb = np.concatenate([ones, cb32, ones[:, :LANES - MLA_NOPE_DIM - MLA_ROPE_DIM]], axis=1)
    sb = np.concatenate([zeros, sb32, zeros[:, :LANES - MLA_NOPE_DIM - MLA_ROPE_DIM]], axis=1)
    return tuple(jnp.asarray(a, jnp.float32) for a in (ca, sa, cb, sb))


def _layer(h_in_is_x, x2, mem, B, S, ln_g, ln_b, w_in_proj, b_gate, gqa_q_norm, gqa_k_norm, mla_q_norm,
           mla_kv_norm, w_mla_qb, w_mla_kvb, w_mem_kv, w_br_gqa, w_br_mla, w_br_mem, w_out,
           ln1_g, ln1_b, w_router, b_router, w_exp_in, b_exp_in, w_exp_out, b_exp_out, ln2_g, ln2_b):
    del h_in_is_x
    bf = jnp.bfloat16
    N, D = x2.shape
    row2 = lambda a: a.reshape(1, -1)

    W = w_in_proj
    zc = lambda n: jnp.zeros((D, n), W.dtype)
    wa = jnp.concatenate([W[:, :OFF_KROPE], zc(HALF), W[:, OFF_KROPE:OFF_QM],
                          zc(LANES - HALF - MLA_ROPE_DIM), W[:, OFF_QM:OFF_GATE]], axis=1).astype(bf)
    wg = W[:, OFF_GATE:].astype(bf)
    qd = MLA_NOPE_DIM + MLA_ROPE_DIM
    wqb = jnp.pad(w_mla_qb.reshape(MLA_Q_LORA, MLA_HEADS, qd), ((0, 0), (0, 0), (0, LANES - qd)))
    partner_lane = np.arange(LANES) ^ (MLA_ROPE_DIM // 4)
    wqb = jnp.concatenate([wqb, wqb[:, :, partner_lane]], axis=1
                          ).reshape(MLA_Q_LORA, 2 * MLA_HEADS * LANES).astype(bf)
    kvb = w_mla_kvb.reshape(MLA_KV_LORA, MLA_HEADS, MLA_NOPE_DIM + MLA_V_DIM)
    wkvk = jnp.pad(kvb[:, :, :MLA_NOPE_DIM], ((0, 0), (0, 0), (0, LANES - MLA_NOPE_DIM))
                   ).reshape(MLA_KV_LORA, MLA_HEADS * LANES).astype(bf)
    wkvv = kvb[:, :, MLA_NOPE_DIM:].reshape(MLA_KV_LORA, MLA_HEADS * MLA_V_DIM).astype(bf)
    gq = row2(jnp.tile(gqa_q_norm, 2))
    gk = row2(jnp.tile(gqa_k_norm, 2))
    ca, sa, cb, sb = _rope_tables(S)

    mkt, mv = _mem_kv(mem, w_mem_kv.astype(bf))
    h, qat, ka, vat, qbt, kb, vbt, oc = _in_proj(
        x2, row2(ln_g), row2(ln_b), wa, gq, gk, row2(mla_q_norm), row2(mla_kv_norm),
        wqb, wkvk, wkvv, ca, sa, cb, sb, mkt, mv, B, S)

    oa = _attention(qat, ka.reshape(B, S, -1), vat, shared_kv=True)
    ob = _attention(qbt, kb.reshape(B, S, -1), vbt, shared_kv=False)

    wr = jnp.pad(w_router, ((0, 0), (0, LANES - N_EXPERTS)))
    wr3 = jnp.stack(_split2(wr))
    br = row2(jnp.pad(b_router, (0, LANES - N_EXPERTS)))
    h1, idx, gate = _mix_out(h, oa.reshape(N, -1), ob.reshape(N, -1), oc, wg, row2(b_gate),
                             w_br_gqa.astype(bf), w_br_mla.astype(bf), w_br_mem.astype(bf),
                             w_out.astype(bf), row2(ln1_g), row2(ln1_b), wr3, br)

    pos, carry_t, cnt_t, total = _route(idx)
    counts = total[0, :N_EXPERTS]
    R = EXPERT_BLOCK
    padded = (counts + R - 1) // R * R
    pad_end = jnp.cumsum(padded)
    pad_start = (pad_end - padded).astype(jnp.int32)
    n_tiles = N // ROW_TILE
    max_rows = N * TOP_K + n_tiles * N_EXPERTS * (SUBLANES - 1) + N_EXPERTS * (R - 1)
    n_blocks = -(-max_rows // R)
    blk_start = jnp.arange(n_blocks, dtype=jnp.int32) * R
    blk_e = jnp.sum((pad_end[None, :] <= blk_start[:, None]).astype(jnp.int32), axis=1)
    n_used = (pad_end[-1:] // R).astype(jnp.int32)
    last_e = jnp.max(jnp.where(blk_start < pad_end[-1], blk_e, 0))
    blk_e = jnp.minimum(blk_e, last_e).astype(jnp.int32)
    per_tile = lambda a: a.reshape(n_tiles, 8, LANES)[:, 0, :N_EXPERTS]
    cnt_flat = per_tile(cnt_t).reshape(-1)
    base_flat = (per_tile(carry_t) + pad_start[None, :]).reshape(-1)
    tail = jnp.concatenate([jnp.where(padded > 0, pad_end - R, -1), pad_end[-1:]]).astype(jnp.int32)

    xs = _dispatch(cnt_flat, base_flat, tail, pos, h1, n_blocks * R)
    ys = _experts(blk_e, n_used, xs, w_exp_in, b_exp_in[:, None, :], w_exp_out, b_exp_out[:, None, :])
    return _combine(cnt_flat, base_flat, pos, gate, h1, row2(ln2_g), row2(ln2_b), ys)


def kernel(x, mem, ln_in_g, ln_in_b, w_in_proj, b_gate, gqa_q_norm, gqa_k_norm, mla_q_norm, mla_kv_norm, w_mla_qb, w_mla_kvb, w_mem_kv, w_br_gqa, w_br_mla, w_br_mem, w_out, ln1_g, ln1_b, w_router, b_router, w_exp_in, b_exp_in, w_exp_out, b_exp_out, ln2_g, ln2_b):
    B, S, D = x.shape
    depth = w_in_proj.shape[0]
    assert depth == 1, "the input LayerNorm is fused into the first (only) layer's projection kernel"
    out = _layer(True, x.reshape(B * S, D), mem, B, S, ln_in_g, ln_in_b, w_in_proj[0], b_gate[0],
                 gqa_q_norm[0], gqa_k_norm[0], mla_q_norm[0], mla_kv_norm[0], w_mla_qb[0], w_mla_kvb[0],
                 w_mem_kv[0], w_br_gqa[0], w_br_mla[0], w_br_mem[0], w_out[0], ln1_g[0], ln1_b[0],
                 w_router[0], b_router[0], w_exp_in[0], b_exp_in[0], w_exp_out[0], b_exp_out[0],
                 ln2_g[0], ln2_b[0])
    return out.reshape(B, S, D)
```

```python
import functools

import jax
import jax.numpy as jnp
import numpy as np
from jax import lax
from jax.experimental import pallas as pl
from jax.experimental.pallas import tpu as pltpu

D_MODEL = 1024
MEM_LEN = 256
GRID_W = 64
ROPE_THETA = 10000.0
RMS_EPS = 1e-6
LN_EPS = 1e-5

GQA_HEADS = 8
GQA_KV_HEADS = 2
GQA_HEAD_DIM = 64
MLA_HEADS = 8
MLA_NOPE_DIM = 64
MLA_ROPE_DIM = 32
MLA_V_DIM = 64
MLA_Q_LORA = 384
MLA_KV_LORA = 256
MEM_HEADS = 4
MEM_HEAD_DIM = 128

N_EXPERTS = 32
TOP_K = 4
SWIGLU_LIMIT = 7.0
SWIGLU_ALPHA = 1.702
DEEPNORM_ALPHA = 2.0 ** 0.25

LANES = 128
SUBLANES = 8
BF16_SUBLANES = 16
HALF = LANES // 2
NEG_INF = float("-inf")
LOG2_E = 1.4426950408889634

OFF_QA, OFF_KA, OFF_VA, OFF_QLAT, OFF_KVLAT, OFF_KROPE, OFF_QM, OFF_GATE = (
    0, 512, 640, 768, 1152, 1408, 1440, 1952)
IN_PROJ_W = 5024
A_QA, A_KA, A_VA, A_QLAT, A_KVLAT, A_KROPE, A_QM, A_END = 0, 512, 640, 768, 1152, 1408, 1536, 2048
ROW_TILE = 256
IN_TILE = 512
IN_CHUNK = 256
MIX_TILE = 1024
MIX_CHUNK = 256
MIX_COLS = 256
ATTN_Q_TILE = 256
ATTN_K_CHUNK = 512
ROUTE_STEP_TILES = 4
EXPERT_BLOCK = 256
VMEM_LIMIT = 48 * 1024 * 1024
EXPERTS_VMEM_LIMIT = 56 * 1024 * 1024


def _cparams(*sem):
    return pltpu.CompilerParams(dimension_semantics=sem, vmem_limit_bytes=VMEM_LIMIT)


def _lane_iota(shape):
    return lax.broadcasted_iota(jnp.int32, shape, len(shape) - 1)


def _layer_norm(x, g, b):
    mu = jnp.mean(x, axis=-1, keepdims=True)
    xc = x - mu
    var = jnp.mean(xc * xc, axis=-1, keepdims=True)
    return xc * lax.rsqrt(var + LN_EPS) * g + b


def _bdot(a, b):
    return jnp.dot(a.astype(jnp.bfloat16), b.astype(jnp.bfloat16), preferred_element_type=jnp.float32)


def _rope(x, cos, sin_signed, pair):
    lane = _lane_iota(x.shape)
    fwd = pltpu.roll(x, LANES - pair, 1)
    bwd = pltpu.roll(x, pair, 1)
    partner = jnp.where((lane % (2 * pair)) < pair, fwd, bwd)
    return x * cos + partner * sin_signed


def _half_rms_scale(x):
    lane = _lane_iota(x.shape)
    sq = x * x
    lo = jnp.sum(jnp.where(lane < HALF, sq, 0.0), axis=-1, keepdims=True)
    hi = jnp.sum(jnp.where(lane >= HALF, sq, 0.0), axis=-1, keepdims=True)
    inv = 1.0 / GQA_HEAD_DIM
    return jnp.where(lane < HALF, lax.rsqrt(lo * inv + RMS_EPS), lax.rsqrt(hi * inv + RMS_EPS))


def _softmax_rows(s):
    m = jnp.max(s, axis=-1, keepdims=True)
    p = jnp.exp(s - m)
    return p, jnp.sum(p, axis=-1, keepdims=True)


def _mem_kv_kernel(mem_ref, w_ref, kt_ref, v_ref):
    kv = _bdot(mem_ref[0], w_ref[...])
    width = MEM_HEADS * MEM_HEAD_DIM
    kt_ref[0] = kv[:, :width].T.astype(jnp.bfloat16)
    v_ref[0] = kv[:, width:].astype(jnp.bfloat16)


def _mem_kv(mem, w_mem_kv):
    B, M, D = mem.shape
    width = MEM_HEADS * MEM_HEAD_DIM
    return pl.pallas_call(
        _mem_kv_kernel,
        grid=(B,),
        in_specs=[pl.BlockSpec((1, M, D), lambda b: (b, 0, 0)),
                  pl.BlockSpec((D, 2 * width), lambda b: (0, 0))],
        out_specs=[pl.BlockSpec((1, width, M), lambda b: (b, 0, 0)),
                   pl.BlockSpec((1, M, width), lambda b: (b, 0, 0))],
        out_shape=[jax.ShapeDtypeStruct((B, width, M), jnp.bfloat16),
                   jax.ShapeDtypeStruct((B, M, width), jnp.bfloat16)],
        compiler_params=_cparams("arbitrary"),
        name="mem_kv",
    )(mem, w_mem_kv)


def _in_proj_kernel(x_ref, lng_ref, lnb_ref, wa_ref, gq_ref, gk_ref, gql_ref, gkvl_ref,
                    wqb_ref, wkvk_ref, wkvv_ref, ca_ref, sa_ref, cb_ref, sb_ref, mkt_ref, mv_ref,
                    h_ref, qa_ref, ka_ref, va_ref, qb_ref, kb_ref, vb_ref, oc_ref):
    lane = _lane_iota((IN_CHUNK, LANES))
    lo_mask = lane < HALF
    q_scale = GQA_HEAD_DIM ** -0.5 * LOG2_E
    qb_scale = (MLA_NOPE_DIM + MLA_ROPE_DIM) ** -0.5 * LOG2_E
    per_group = GQA_HEADS // GQA_KV_HEADS
    gq, gk = gq_ref[...], gk_ref[...]

    def project(r):
        h = _layer_norm(x_ref[r, :], lng_ref[...], lnb_ref[...])
        h_ref[r, :] = h
        return _bdot(h, wa_ref[...])

    def finish(r, proj):
        ca, sa, cb, sb = ca_ref[r, :], sa_ref[r, :], cb_ref[r, :], sb_ref[r, :]

        for c in range(GQA_HEADS // 2):
            slab = proj[:, A_QA + c * LANES:A_QA + (c + 1) * LANES]
            slab = _rope(slab * _half_rms_scale(slab) * gq, ca, sa, 16) * q_scale
            swapped = pltpu.roll(slab, HALF, 1)
            for half in range(2):
                head = 2 * c + half
                group_lo = (head // per_group) == 0
                src = slab if (half == 0) == group_lo else swapped
                keep = lo_mask if group_lo else jnp.logical_not(lo_mask)
                qa_ref[0, head * LANES:(head + 1) * LANES, r] = jnp.where(keep, src, 0.0).astype(jnp.bfloat16).T

        ka = proj[:, A_KA:A_KA + LANES]
        ka_ref[r, :] = _rope(ka * _half_rms_scale(ka) * gk, ca, sa, 16).astype(jnp.bfloat16)
        va_ref[0, :, r] = proj[:, A_VA:A_VA + LANES].astype(jnp.bfloat16).T

        ql = proj[:, A_QLAT:A_KVLAT]
        ql = ql * lax.rsqrt(jnp.mean(ql * ql, axis=-1, keepdims=True) + RMS_EPS) * gql_ref[...]
        qm = _bdot(ql, wqb_ref[...])
        kvl = proj[:, A_KVLAT:A_KROPE]
        kvl = kvl * lax.rsqrt(jnp.mean(kvl * kvl, axis=-1, keepdims=True) + RMS_EPS) * gkvl_ref[...]
        kn = _bdot(kvl, wkvk_ref[...])
        vb = _bdot(kvl, wkvv_ref[...])
        for c in range(vb.shape[1] // LANES):
            sl = slice(c * LANES, (c + 1) * LANES)
            vb_ref[0, sl, r] = vb[:, sl].astype(jnp.bfloat16).T
        k_pe = _rope(proj[:, A_KROPE:A_QM], cb, sb, 8)
        for hd in range(MLA_HEADS):
            sl = slice(hd * LANES, (hd + 1) * LANES)
            partner = qm[:, (MLA_HEADS + hd) * LANES:(MLA_HEADS + hd + 1) * LANES]
            qb_ref[0, sl, r] = ((qm[:, sl] * cb + partner * sb) * qb_scale).astype(jnp.bfloat16).T
            kb_ref[r, sl] = (kn[:, sl] + k_pe).astype(jnp.bfloat16)

        qc = proj[:, A_QM:A_END] * (MEM_HEAD_DIM ** -0.5)
        for hd in range(MEM_HEADS):
            sl = slice(hd * MEM_HEAD_DIM, (hd + 1) * MEM_HEAD_DIM)
            p, l = _softmax_rows(_bdot(qc[:, sl], mkt_ref[0, sl, :]))
            oc_ref[r, sl] = (_bdot(p, mv_ref[0, :, sl]) / l).astype(jnp.bfloat16)

    chunks = [slice(s * IN_CHUNK, (s + 1) * IN_CHUNK) for s in range(x_ref.shape[0] // IN_CHUNK)]
    pending = None
    for r in chunks:
        proj = project(r)
        if pending is not None:
            finish(*pending)
        pending = (r, proj)
    finish(*pending)


def _in_proj(x2, ln_g, ln_b, wa, gq, gk, gql, gkvl, wqb, wkvk, wkvv, ca, sa, cb, sb, mkt, mv, B, S):
    N, D = x2.shape
    T = IN_TILE
    tiles_per_seq = S // T
    row = lambda i: (i, 0)
    const = lambda i: (0, 0)
    pos = lambda i: (i % tiles_per_seq, 0)
    batch3 = lambda i: (i // tiles_per_seq, 0, 0)
    kt_map = lambda i: (i // tiles_per_seq, 0, i % tiles_per_seq)
    full = lambda a: pl.BlockSpec(a.shape, const)
    return pl.pallas_call(
        _in_proj_kernel,
        grid=(N // T,),
        in_specs=[pl.BlockSpec((T, D), row), full(ln_g), full(ln_b), full(wa), full(gq), full(gk),
                  full(gql), full(gkvl), full(wqb), full(wkvk), full(wkvv),
                  pl.BlockSpec((T, LANES), pos), pl.BlockSpec((T, LANES), pos),
                  pl.BlockSpec((T, LANES), pos), pl.BlockSpec((T, LANES), pos),
                  pl.BlockSpec((1,) + mkt.shape[1:], batch3), pl.BlockSpec((1,) + mv.shape[1:], batch3)],
        out_specs=[pl.BlockSpec((T, D), row),
                   pl.BlockSpec((1, GQA_HEADS * LANES, T), kt_map),
                   pl.BlockSpec((T, LANES), row),
                   pl.BlockSpec((1, LANES, T), kt_map),
                   pl.BlockSpec((1, MLA_HEADS * LANES, T), kt_map),
                   pl.BlockSpec((T, MLA_HEADS * LANES), row),
                   pl.BlockSpec((1, MLA_HEADS * MLA_V_DIM, T), kt_map),
                   pl.BlockSpec((T, MEM_HEADS * MEM_HEAD_DIM), row)],
        out_shape=[jax.ShapeDtypeStruct((N, D), jnp.float32),
                   jax.ShapeDtypeStruct((B, GQA_HEADS * LANES, S), jnp.bfloat16),
                   jax.ShapeDtypeStruct((N, LANES), jnp.bfloat16),
                   jax.ShapeDtypeStruct((B, LANES, S), jnp.bfloat16),
                   jax.ShapeDtypeStruct((B, MLA_HEADS * LANES, S), jnp.bfloat16),
                   jax.ShapeDtypeStruct((N, MLA_HEADS * LANES), jnp.bfloat16),
                   jax.ShapeDtypeStruct((B, MLA_HEADS * MLA_V_DIM, S), jnp.bfloat16),
                   jax.ShapeDtypeStruct((N, MEM_HEADS * MEM_HEAD_DIM), jnp.bfloat16)],
        compiler_params=_cparams("arbitrary"),
        name="in_proj",
    )(x2, ln_g, ln_b, wa, gq, gk, gql, gkvl, wqb, wkvk, wkvv, ca, sa, cb, sb, mkt, mv)


def _attention_kernel(qt_ref, k_ref, vt_ref, o_ref, s_even, s_odd, *, shared_kv):
    t = pl.program_id(0)

    @pl.when(t == 0)
    def _():
        s_odd[...] = jnp.zeros_like(s_odd)

    def step(s_new, s_old):
        S, TQ = s_new.shape[1:]
        m = [jnp.full((1, TQ), NEG_INF, jnp.float32)] * 2
        ones = jnp.ones((BF16_SUBLANES, ATTN_K_CHUNK), jnp.bfloat16)
        acc = [jnp.zeros((HALF + BF16_SUBLANES, TQ), jnp.float32)] * 2
        for c in range(S // ATTN_K_CHUNK):
            rows = slice(c * ATTN_K_CHUNK, (c + 1) * ATTN_K_CHUNK)
            for hd in range(2):
                qt = qt_ref[0, hd * LANES:(hd + 1) * LANES, :]
                k = k_ref[0, rows, :] if shared_kv else k_ref[0, rows, hd * LANES:(hd + 1) * LANES]
                s_new[hd, rows, :] = jnp.dot(k, qt, preferred_element_type=jnp.float32)
                vt = vt_ref[0, :, rows] if shared_kv else vt_ref[0, hd * HALF:(hd + 1) * HALF, rows]
                vt = jnp.concatenate([vt, ones], axis=0)
                sc = s_old[hd, rows, :]
                m_new = jnp.maximum(m[hd], jnp.max(sc, axis=0, keepdims=True))
                p = jnp.exp2(sc - m_new).astype(jnp.bfloat16)
                corr = jnp.exp2(m[hd] - m_new)
                acc[hd] = acc[hd] * corr + jnp.dot(vt, p, preferred_element_type=jnp.float32)
                m[hd] = m_new
        outs = [acc[hd][:HALF] / acc[hd][HALF:HALF + 1] for hd in range(2)]
        o_ref[0] = jnp.concatenate(outs, axis=0).T.astype(jnp.bfloat16)

    @pl.when(t % 2 == 0)
    def _():
        step(s_even, s_odd)

    @pl.when(t % 2 == 1)
    def _():
        step(s_odd, s_even)


def _attention(qt, k, vt, *, shared_kv):
    B, qh, S = qt.shape
    pairs = qh // (2 * LANES)
    TQ = ATTN_Q_TILE
    nq = S // TQ
    items = B * pairs * nq
    pairs_per_group = pairs // GQA_KV_HEADS

    def item(t):
        return t // (pairs * nq), (t // nq) % pairs, t % nq

    score_item = lambda t: item(jnp.minimum(t, items - 1))
    finish_item = lambda t: item(jnp.maximum(t - 1, 0))

    def qt_map(t):
        b, j, i = score_item(t)
        return b, j, i

    def k_map(t):
        b, j, _ = score_item(t)
        return (b, 0, 0) if shared_kv else (b, 0, j)

    def vt_map(t):
        b, j, _ = finish_item(t)
        return (b, j // pairs_per_group, 0) if shared_kv else (b, j, 0)

    def o_map(t):
        b, j, i = finish_item(t)
        return b, i, j

    k_spec = pl.BlockSpec((1, S, LANES if shared_kv else 2 * LANES), k_map)
    vt_spec = pl.BlockSpec((1, HALF if shared_kv else LANES, S), vt_map)
    return pl.pallas_call(
        functools.partial(_attention_kernel, shared_kv=shared_kv),
        grid=(items + 1,),
        in_specs=[pl.BlockSpec((1, 2 * LANES, TQ), qt_map), k_spec, vt_spec],
        out_specs=pl.BlockSpec((1, TQ, LANES), o_map),
        out_shape=jax.ShapeDtypeStruct((B, S, pairs * LANES), jnp.bfloat16),
        scratch_shapes=[pltpu.VMEM((2, S, TQ), jnp.float32), pltpu.VMEM((2, S, TQ), jnp.float32)],
        compiler_params=_cparams("arbitrary"),
        name="attention_gqa" if shared_kv else "attention_mla",
    )(qt, k, vt)


def _split2(x):
    hi = x.astype(jnp.bfloat16)
    return hi, (x - hi.astype(jnp.float32)).astype(jnp.bfloat16)


def _mix_out_kernel(h_ref, oa_ref, ob_ref, oc_ref, wg_ref, bg_ref, wa_ref, wb_ref, wc_ref, wo_ref,
                    g_ref, b_ref, wr_ref, br_ref, h1_ref, idx_ref, gate_ref):
    D = D_MODEL
    dot = lambda a, b: jnp.dot(a, b, preferred_element_type=jnp.float32)

    def mix(r):
        hb = h_ref[r, :].astype(jnp.bfloat16)
        merged = []
        for n in range(D // MIX_COLS):
            acc = None
            for i, (o_ref, w_ref) in enumerate(((oa_ref, wa_ref), (ob_ref, wb_ref), (oc_ref, wc_ref))):
                gate_cols = slice(i * D + n * MIX_COLS, i * D + (n + 1) * MIX_COLS)
                gate = jax.nn.sigmoid(dot(hb, wg_ref[:, gate_cols]) + bg_ref[:, gate_cols])
                term = gate * dot(o_ref[r, :], w_ref[:, n * MIX_COLS:(n + 1) * MIX_COLS])
                acc = term if acc is None else acc + term
            merged.append(acc.astype(jnp.bfloat16))
        return dot(jnp.concatenate(merged, axis=1), wo_ref[...])

    def route(r, mixed):
        h1 = _layer_norm(DEEPNORM_ALPHA * h_ref[r, :] + mixed, g_ref[...], b_ref[...])
        h1_ref[r, :] = h1
        h_hi, h_lo = _split2(h1)
        w_hi, w_lo = wr_ref[0], wr_ref[1]
        logits = dot(h_hi, w_hi) + (dot(h_hi, w_lo) + dot(h_lo, w_hi)) + br_ref[...]
        lane = _lane_iota(logits.shape)
        logits = jnp.where(lane < N_EXPERTS, logits, NEG_INF)
        idx_out = jnp.zeros(logits.shape, jnp.int32)
        val_out = jnp.zeros(logits.shape, jnp.float32)
        top = None
        for k in range(TOP_K):
            m = jnp.max(logits, axis=-1, keepdims=True)
            idx = jnp.min(jnp.where(logits == m, lane, LANES), axis=-1, keepdims=True)
            logits = jnp.where(lane == idx, NEG_INF, logits)
            top = m if top is None else top
            idx_out = jnp.where(lane == k, idx, idx_out)
            val_out = jnp.where(lane == k, jnp.exp(m - top), val_out)
        idx_ref[r, :] = idx_out
        gate_ref[r, :] = val_out / jnp.sum(val_out, axis=-1, keepdims=True)

    pending = None
    for s in range(h_ref.shape[0] // MIX_CHUNK):
        r = slice(s * MIX_CHUNK, (s + 1) * MIX_CHUNK)
        mixed = mix(r)
        if pending is not None:
            route(*pending)
        pending = (r, mixed)
    route(*pending)


def _mix_out(h, oa, ob, oc, wg, bg, wa, wb, wc, wo, g, b, wr3, br):
    N, D = h.shape
    T = MIX_TILE
    row = lambda i: (i, 0)
    full = lambda a: pl.BlockSpec(a.shape, lambda i: (0,) * a.ndim)
    return pl.pallas_call(
        _mix_out_kernel,
        grid=(N // T,),
        in_specs=[pl.BlockSpec((T, D), row), pl.BlockSpec((T, oa.shape[1]), row),
                  pl.BlockSpec((T, ob.shape[1]), row), pl.BlockSpec((T, oc.shape[1]), row),
                  full(wg), full(bg), full(wa), full(wb), full(wc), full(wo), full(g), full(b),
                  full(wr3), full(br)],
        out_specs=[pl.BlockSpec((T, D), row), pl.BlockSpec((T, LANES), row), pl.BlockSpec((T, LANES), row)],
        out_shape=[jax.ShapeDtypeStruct((N, D), jnp.float32),
                   jax.ShapeDtypeStruct((N, LANES), jnp.int32),
                   jax.ShapeDtypeStruct((N, LANES), jnp.float32)],
        compiler_params=_cparams("arbitrary"),
        name="mix_out",
    )(h, oa, ob, oc, wg, bg, wa, wb, wc, wo, g, b, wr3, br)


def _expert_rows(idx, rows_shape):
    T = idx.shape[0]
    lane = _lane_iota(idx.shape)
    sel = [lane == idx[:, k:k + 1] for k in range(TOP_K)]
    onehot = sum(s.astype(jnp.float32) for s in sel)
    r = lax.broadcasted_iota(jnp.int32, (T, T), 0)
    c = lax.broadcasted_iota(jnp.int32, (T, T), 1)
    before = (c < r).astype(jnp.bfloat16)
    prefix = jnp.dot(before, onehot.astype(jnp.bfloat16), preferred_element_type=jnp.float32)
    cnt = jnp.broadcast_to(jnp.sum(onehot, axis=0, keepdims=True), rows_shape)
    cnt = jnp.floor((cnt + (SUBLANES - 1)) * (1.0 / SUBLANES)) * SUBLANES
    er = lax.broadcasted_iota(jnp.int32, (LANES, LANES), 0)
    ec = lax.broadcasted_iota(jnp.int32, (LANES, LANES), 1)
    local_start = jnp.dot(cnt.astype(jnp.bfloat16), (er < ec).astype(jnp.bfloat16),
                          preferred_element_type=jnp.float32)
    target = prefix + local_start[0:1, :]
    out = jnp.zeros(idx.shape, jnp.int32)
    for k in range(TOP_K):
        pk = jnp.sum(jnp.where(sel[k], target, 0.0), axis=-1, keepdims=True)
        out = jnp.where(lane == k, pk.astype(jnp.int32), out)
    return out, cnt


def _route_kernel(idx_ref, pos_ref, carry_out_ref, cnt_out_ref, total_ref, carry_ref):
    @pl.when(pl.program_id(0) == 0)
    def _():
        carry_ref[...] = jnp.zeros_like(carry_ref)

    for s in range(idx_ref.shape[0] // ROW_TILE):
        rows = slice(s * ROW_TILE, (s + 1) * ROW_TILE)
        stats = slice(s * SUBLANES, (s + 1) * SUBLANES)
        pos, cnt = _expert_rows(idx_ref[rows, :], carry_ref.shape)
        pos_ref[rows, :] = pos
        carry_out_ref[stats, :] = carry_ref[...].astype(jnp.int32)
        cnt_out_ref[stats, :] = cnt.astype(jnp.int32)
        carry_ref[...] = carry_ref[...] + cnt
    total_ref[...] = carry_ref[...].astype(jnp.int32)


def _route(idx):
    N = idx.shape[0]
    T = ROUTE_STEP_TILES * ROW_TILE
    n_tiles = N // ROW_TILE
    tile8 = pl.BlockSpec((ROUTE_STEP_TILES * SUBLANES, LANES), lambda i: (i, 0))
    return pl.pallas_call(
        _route_kernel,
        grid=(N // T,),
        in_specs=[pl.BlockSpec((T, LANES), lambda i: (i, 0))],
        out_specs=[pl.BlockSpec((T, LANES), lambda i: (i, 0)), tile8, tile8,
                   pl.BlockSpec((SUBLANES, LANES), lambda i: (0, 0))],
        out_shape=[jax.ShapeDtypeStruct((N, LANES), jnp.int32),
                   jax.ShapeDtypeStruct((n_tiles * SUBLANES, LANES), jnp.int32),
                   jax.ShapeDtypeStruct((n_tiles * SUBLANES, LANES), jnp.int32),
                   jax.ShapeDtypeStruct((SUBLANES, LANES), jnp.int32)],
        scratch_shapes=[pltpu.VMEM((SUBLANES, LANES), jnp.float32)],
        compiler_params=_cparams("arbitrary"),
        name="route",
    )(idx)


LOCAL_ROWS = TOP_K * ROW_TILE + ROW_TILE
LOCAL_CHUNK = 256
STRIP_BUFFERS = 3
assert LOCAL_ROWS >= TOP_K * ROW_TILE + N_EXPERTS * (SUBLANES - 1)


HI16 = 0xFFFF0000


def _pack_rows(x):
    half = x.shape[1] // 2
    lo = lax.bitcast_convert_type(x[:, :half], jnp.uint32) >> 16
    hi = lax.bitcast_convert_type(x[:, half:], jnp.uint32) & jnp.uint32(HI16)
    return lo | hi


def _unpack_rows(w):
    lo = lax.bitcast_convert_type(w << 16, jnp.float32)
    hi = lax.bitcast_convert_type(w & jnp.uint32(HI16), jnp.float32)
    return jnp.concatenate([lo, hi], axis=1).astype(jnp.bfloat16)


def _round_bf16(x):
    return x.astype(jnp.bfloat16).astype(jnp.float32)


def _for_each_strip(cnt_ref, base_ref, tile, fn):
    def body(e, off):
        c = cnt_ref[tile * N_EXPERTS + e]
        d = base_ref[tile * N_EXPERTS + e]

        @pl.when(c > 0)
        def _():
            fn(pl.multiple_of(off, SUBLANES), pl.multiple_of(d, SUBLANES), pl.multiple_of(c, SUBLANES))
        return off + c

    lax.fori_loop(0, N_EXPERTS, body, 0)


def _wait_tile_strips(cnt_ref, tile, make_copy):
    rows = cnt_ref[pl.num_programs(0) * N_EXPERTS + tile]

    @pl.when(rows > 0)
    def _():
        make_copy(0, 0, pl.multiple_of(rows, SUBLANES)).wait()


def _dispatch_kernel(cnt_ref, base_ref, tail_ref, pos_ref, h_ref, xs_ref, buf, zeros, sems, zsem):
    i = pl.program_id(0)
    last = pl.num_programs(0) - 1
    T = h_ref.shape[0]
    cur = i % STRIP_BUFFERS

    @pl.when(i == 0)
    def _():
        zeros[...] = jnp.zeros_like(zeros)

        def tail_copy(e):
            row = pl.multiple_of(jnp.maximum(tail_ref[e], 0), SUBLANES)
            return pltpu.make_async_copy(zeros, xs_ref.at[pl.ds(row, EXPERT_BLOCK), :], zsem)

        def spare_copy(blk):
            row = pl.multiple_of(blk * EXPERT_BLOCK, EXPERT_BLOCK)
            return pltpu.make_async_copy(zeros, xs_ref.at[pl.ds(row, EXPERT_BLOCK), :], zsem)

        first_spare = tail_ref[N_EXPERTS] // EXPERT_BLOCK
        for act in ("start", "wait"):
            def body(e, carry, act=act):
                @pl.when(tail_ref[e] >= 0)
                def _():
                    getattr(tail_copy(e), act)()
                return carry
            lax.fori_loop(0, N_EXPERTS, body, 0)

            def spare(blk, carry, act=act):
                getattr(spare_copy(blk), act)()
                return carry
            lax.fori_loop(first_spare, xs_ref.shape[0] // EXPERT_BLOCK, spare, 0)

    pos_t = pos_ref[...].astype(jnp.float32).T.astype(jnp.int32)
    hb = h_ref[...].astype(jnp.bfloat16)
    for c in range(LOCAL_ROWS // LOCAL_CHUNK):
        rows = lax.broadcasted_iota(jnp.int32, (LOCAL_CHUNK, T), 0) + c * LOCAL_CHUNK
        hit = rows == pos_t[0:1, :]
        for k in range(1, TOP_K):
            hit = jnp.logical_or(hit, rows == pos_t[k:k + 1, :])
        buf[cur, c * LOCAL_CHUNK:(c + 1) * LOCAL_CHUNK, :] = _pack_rows(
            jnp.dot(hit.astype(jnp.bfloat16), hb, preferred_element_type=jnp.float32))

    def strip(tile):
        slot = tile % STRIP_BUFFERS

        def make(local_row, slot_row, rows):
            return pltpu.make_async_copy(buf.at[slot, pl.ds(local_row, rows), :],
                                         xs_ref.at[pl.ds(slot_row, rows), :], sems.at[slot])
        return make

    def drain(tile):
        _wait_tile_strips(cnt_ref, tile, strip(tile))

    @pl.when(i >= STRIP_BUFFERS - 1)
    def _():
        drain(i - (STRIP_BUFFERS - 1))

    _for_each_strip(cnt_ref, base_ref, i, lambda *a: strip(i)(*a).start())

    @pl.when(i == last)
    def _():
        for back in range(STRIP_BUFFERS - 2, -1, -1):
            @pl.when(i - back >= 0)
            def _(back=back):
                drain(i - back)


def _dispatch(cnt_flat, base_flat, tail, pos, h1, n_slots):
    N, D = h1.shape
    T = ROW_TILE
    return pl.pallas_call(
        _dispatch_kernel,
        grid_spec=pltpu.PrefetchScalarGridSpec(
            num_scalar_prefetch=3,
            grid=(N // T,),
            in_specs=[pl.BlockSpec((T, LANES), lambda i, c, b, t: (i, 0)),
                      pl.BlockSpec((T, D), lambda i, c, b, t: (i, 0))],
            out_specs=pl.BlockSpec(memory_space=pl.ANY),
            scratch_shapes=[pltpu.VMEM((STRIP_BUFFERS, LOCAL_ROWS, D // 2), jnp.uint32),
                            pltpu.VMEM((EXPERT_BLOCK, D // 2), jnp.uint32),
                            pltpu.SemaphoreType.DMA((STRIP_BUFFERS,)),
                            pltpu.SemaphoreType.DMA(())]),
        out_shape=jax.ShapeDtypeStruct((n_slots, D // 2), jnp.uint32),
        compiler_params=_cparams("arbitrary"),
        name="dispatch",
    )(cnt_flat, base_flat, tail, pos, h1)


def _experts_kernel(blk_e_ref, n_used_ref, x_ref, wi_ref, bi_ref, wo_ref, bo_ref, y_ref, wi_bf, wo_bf):
    i = pl.program_id(0)
    used = i < n_used_ref[0]
    new_expert = jnp.logical_or(i == 0, blk_e_ref[i] != blk_e_ref[jnp.maximum(i - 1, 0)])

    @pl.when(jnp.logical_and(used, new_expert))
    def _():
        wi_bf[...] = wi_ref[0].astype(jnp.bfloat16)
        wo_bf[...] = wo_ref[0].astype(jnp.bfloat16)

    @pl.when(used)
    def _():
        hb = jnp.dot(_unpack_rows(x_ref[...]), wi_bf[...], preferred_element_type=jnp.float32) + bi_ref[0]
        De = D_MODEL
        x_glu = jnp.minimum(hb[:, :De], SWIGLU_LIMIT)
        x_lin = jnp.clip(hb[:, De:], -SWIGLU_LIMIT, SWIGLU_LIMIT)
        act = x_glu * jax.nn.sigmoid(SWIGLU_ALPHA * x_glu) * (x_lin + 1.0)
        y = jnp.dot(act.astype(jnp.bfloat16), wo_bf[...], preferred_element_type=jnp.float32) + bo_ref[0]
        y_ref[...] = _pack_rows(_round_bf16(y))

    @pl.when(jnp.logical_not(used))
    def _():
        y_ref[...] = jnp.zeros_like(y_ref)


def _experts(blk_e, n_used, xs, w_in, b_in, w_out, b_out):
    n_slots, half = xs.shape
    R = EXPERT_BLOCK
    E, D, F = w_in.shape
    return pl.pallas_call(
        _experts_kernel,
        grid_spec=pltpu.PrefetchScalarGridSpec(
            num_scalar_prefetch=2,
            grid=(n_slots // R,),
            in_specs=[pl.BlockSpec((R, half), lambda i, e, n: (jnp.minimum(i, n[0] - 1), 0)),
                      pl.BlockSpec((1, D, F), lambda i, e, n: (e[i], 0, 0)),
                      pl.BlockSpec((1, 1, F), lambda i, e, n: (e[i], 0, 0)),
                      pl.BlockSpec((1, F // 2, D), lambda i, e, n: (e[i], 0, 0)),
                      pl.BlockSpec((1, 1, D), lambda i, e, n: (e[i], 0, 0))],
            out_specs=pl.BlockSpec((R, half), lambda i, e, n: (i, 0)),
            scratch_shapes=[pltpu.VMEM((D, F), jnp.bfloat16), pltpu.VMEM((F // 2, D), jnp.bfloat16)]),
        out_shape=jax.ShapeDtypeStruct((n_slots, half), jnp.uint32),
        compiler_params=pltpu.CompilerParams(dimension_semantics=("arbitrary",),
                                             vmem_limit_bytes=EXPERTS_VMEM_LIMIT),
        name="experts",
    )(blk_e, n_used, xs, w_in, b_in, w_out, b_out)


def _combine_kernel(cnt_ref, base_ref, pos_ref, gate_ref, h_ref, g_ref, b_ref, ys_ref, o_ref, buf, sems):
    i = pl.program_id(0)
    last = pl.num_programs(0) - 1
    T = h_ref.shape[0]
    cur = i % STRIP_BUFFERS

    def strip(tile):
        slot = tile % STRIP_BUFFERS

        def make(local_row, slot_row, rows):
            return pltpu.make_async_copy(ys_ref.at[pl.ds(slot_row, rows), :],
                                         buf.at[slot, pl.ds(local_row, rows), :], sems.at[slot])
        return make

    def fetch(tile):
        _for_each_strip(cnt_ref, base_ref, tile, lambda *a: strip(tile)(*a).start())

    @pl.when(i == 0)
    def _():
        buf[...] = jnp.zeros_like(buf)
        for ahead in range(STRIP_BUFFERS - 1):
            @pl.when(ahead <= last)
            def _(ahead=ahead):
                fetch(ahead)

    @pl.when(i + (STRIP_BUFFERS - 1) <= last)
    def _():
        fetch(i + (STRIP_BUFFERS - 1))

    _wait_tile_strips(cnt_ref, i, strip(i))

    pos, gate = pos_ref[...], gate_ref[...]
    dot = lambda a, b: jnp.dot(a, b, preferred_element_type=jnp.float32)
    ffn = None
    for c in range(LOCAL_ROWS // LOCAL_CHUNK):
        col = lax.broadcasted_iota(jnp.int32, (T, LOCAL_CHUNK), 1) + c * LOCAL_CHUNK
        u = jnp.zeros(col.shape, jnp.float32)
        for k in range(TOP_K):
            u = jnp.where(col == pos[:, k:k + 1], gate[:, k:k + 1], u)
        u_hi, u_lo = _split2(u)
        y = _unpack_rows(buf[cur, c * LOCAL_CHUNK:(c + 1) * LOCAL_CHUNK, :])
        part = dot(u_hi, y) + dot(u_lo, y)
        ffn = part if ffn is None else ffn + part
    o_ref[...] = _layer_norm(DEEPNORM_ALPHA * h_ref[...] + ffn, g_ref[...], b_ref[...])


def _combine(cnt_flat, base_flat, pos, gate, h1, g, b, ys):
    N, D = h1.shape
    T = ROW_TILE
    row = lambda i, c, s: (i, 0)
    const = lambda i, c, s: (0, 0)
    return pl.pallas_call(
        _combine_kernel,
        grid_spec=pltpu.PrefetchScalarGridSpec(
            num_scalar_prefetch=2,
            grid=(N // T,),
            in_specs=[pl.BlockSpec((T, LANES), row), pl.BlockSpec((T, LANES), row), pl.BlockSpec((T, D), row),
                      pl.BlockSpec(g.shape, const), pl.BlockSpec(b.shape, const),
                      pl.BlockSpec(memory_space=pl.ANY)],
            out_specs=pl.BlockSpec((T, D), row),
            scratch_shapes=[pltpu.VMEM((STRIP_BUFFERS, LOCAL_ROWS, D // 2), jnp.uint32),
                            pltpu.SemaphoreType.DMA((STRIP_BUFFERS,))]),
        out_shape=jax.ShapeDtypeStruct((N, D), jnp.float32),
        compiler_params=_cparams("arbitrary"),
        name="combine",
    )(cnt_flat, base_flat, pos, gate, h1, g, b, ys)


def _rope_tables(S):
    t = np.arange(S)
    row, col = (t // GRID_W).astype(np.float64), (t % GRID_W).astype(np.float64)

    def block(half):
        inv = ROPE_THETA ** (-np.arange(half, dtype=np.float64) / half)
        ar, ac = row[:, None] * inv[None, :], col[:, None] * inv[None, :]
        cos = np.concatenate([np.cos(ar), np.cos(ar), np.cos(ac), np.cos(ac)], axis=1)
        sin = np.concatenate([-np.sin(ar), np.sin(ar), -np.sin(ac), np.sin(ac)], axis=1)
        return cos, sin

    ca64, sa64 = block(GQA_HEAD_DIM // 4)
    ca, sa = np.tile(ca64, (1, 2)), np.tile(sa64, (1, 2))
    cb32, sb32 = block(MLA_ROPE_DIM // 4)
    ones, zeros = np.ones((S, MLA_NOPE_DIM)), np.zeros((S, MLA_NOPE_DIM))
    cb = np.concatenate([ones, cb32, ones[:, :LANES - MLA_NOPE_DIM - MLA_ROPE_DIM]], axis=1)
    sb = np.concatenate([zeros, sb32, zeros[:, :LANES - MLA_NOPE_DIM - MLA_ROPE_DIM]], axis=1)
    return tuple(jnp.asarray(a, jnp.float32) for a in (ca, sa, cb, sb))


def _layer(h_in_is_x, x2, mem, B, S, ln_g, ln_b, w_in_proj, b_gate, gqa_q_norm, gqa_k_norm, mla_q_norm,
           mla_kv_norm, w_mla_qb, w_mla_kvb, w_mem_kv, w_br_gqa, w_br_mla, w_br_mem, w_out,
           ln1_g, ln1_b, w_router, b_router, w_exp_in, b_exp_in, w_exp_out, b_exp_out, ln2_g, ln2_b):
    del h_in_is_x
    bf = jnp.bfloat16
    N, D = x2.shape
    row2 = lambda a: a.reshape(1, -1)

    W = w_in_proj
    zc = lambda n: jnp.zeros((D, n), W.dtype)
    wa = jnp.concatenate([W[:, :OFF_KROPE], zc(HALF), W[:, OFF_KROPE:OFF_QM],
                          zc(LANES - HALF - MLA_ROPE_DIM), W[:, OFF_QM:OFF_GATE]], axis=1).astype(bf)
    wg = W[:, OFF_GATE:].astype(bf)
    qd = MLA_NOPE_DIM + MLA_ROPE_DIM
    wqb = jnp.pad(w_mla_qb.reshape(MLA_Q_LORA, MLA_HEADS, qd), ((0, 0), (0, 0), (0, LANES - qd)))
    partner_lane = np.arange(LANES) ^ (MLA_ROPE_DIM // 4)
    wqb = jnp.concatenate([wqb, wqb[:, :, partner_lane]], axis=1
                          ).reshape(MLA_Q_LORA, 2 * MLA_HEADS * LANES).astype(bf)
    kvb = w_mla_kvb.reshape(MLA_KV_LORA, MLA_HEADS, MLA_NOPE_DIM + MLA_V_DIM)
    wkvk = jnp.pad(kvb[:, :, :MLA_NOPE_DIM], ((0, 0), (0, 0), (0, LANES - MLA_NOPE_DIM))
                   ).reshape(MLA_KV_LORA, MLA_HEADS * LANES).astype(bf)
    wkvv = kvb[:, :, MLA_NOPE_DIM:].reshape(MLA_KV_LORA, MLA_HEADS * MLA_V_DIM).astype(bf)
    gq = row2(jnp.tile(gqa_q_norm, 2))
    gk = row2(jnp.tile(gqa_k_norm, 2))
    ca, sa, cb, sb = _rope_tables(S)

    mkt, mv = _mem_kv(mem, w_mem_kv.astype(bf))
    h, qat, ka, vat, qbt, kb, vbt, oc = _in_proj(
        x2, row2(ln_g), row2(ln_b), wa, gq, gk, row2(mla_q_norm), row2(mla_kv_norm),
        wqb, wkvk, wkvv, ca, sa, cb, sb, mkt, mv, B, S)

    oa = _attention(qat, ka.reshape(B, S, -1), vat, shared_kv=True)
    ob = _attention(qbt, kb.reshape(B, S, -1), vbt, shared_kv=False)

    wr = jnp.pad(w_router, ((0, 0), (0, LANES - N_EXPERTS)))
    wr3 = jnp.stack(_split2(wr))
    br = row2(jnp.pad(b_router, (0, LANES - N_EXPERTS)))
    h1, idx, gate = _mix_out(h, oa.reshape(N, -1), ob.reshape(N, -1), oc, wg, row2(b_gate),
                             w_br_gqa.astype(bf), w_br_mla.astype(bf), w_br_mem.astype(bf),
                             w_out.astype(bf), row2(ln1_g), row2(ln1_b), wr3, br)

    pos, carry_t, cnt_t, total = _route(idx)
    counts = total[0, :N_EXPERTS]
    R = EXPERT_BLOCK
    padded = (counts + R - 1) // R * R
    pad_end = jnp.cumsum(padded)
    pad_start = (pad_end - padded).astype(jnp.int32)
    n_tiles = N // ROW_TILE
    max_rows = N * TOP_K + n_tiles * N_EXPERTS * (SUBLANES - 1) + N_EXPERTS * (R - 1)
    n_blocks = -(-max_rows // R)
    blk_start = jnp.arange(n_blocks, dtype=jnp.int32) * R
    blk_e = jnp.sum((pad_end[None, :] <= blk_start[:, None]).astype(jnp.int32), axis=1)
    n_used = (pad_end[-1:] // R).astype(jnp.int32)
    last_e = jnp.max(jnp.where(blk_start < pad_end[-1], blk_e, 0))
    blk_e = jnp.minimum(blk_e, last_e).astype(jnp.int32)
    per_tile = lambda a: a.reshape(n_tiles, 8, LANES)[:, 0, :N_EXPERTS]
    cnt_flat = jnp.concatenate([per_tile(cnt_t).reshape(-1), jnp.sum(per_tile(cnt_t), axis=1)])
    base_flat = (per_tile(carry_t) + pad_start[None, :]).reshape(-1)
    tail = jnp.concatenate([jnp.where(padded > 0, pad_end - R, -1), pad_end[-1:]]).astype(jnp.int32)

    xs = _dispatch(cnt_flat, base_flat, tail, pos, h1, n_blocks * R)
    ys = _experts(blk_e, n_used, xs, w_exp_in, b_exp_in[:, None, :], w_exp_out, b_exp_out[:, None, :])
    return _combine(cnt_flat, base_flat, pos, gate, h1, row2(ln2_g), row2(ln2_b), ys)


def kernel(x, mem, ln_in_g, ln_in_b, w_in_proj, b_gate, gqa_q_norm, gqa_k_norm, mla_q_norm, mla_kv_norm, w_mla_qb, w_mla_kvb, w_mem_kv, w_br_gqa, w_br_mla, w_br_mem, w_out, ln1_g, ln1_b, w_router, b_router, w_exp_in, b_exp_in, w_exp_out, b_exp_out, ln2_g, ln2_b):
    B, S, D = x.shape
    depth = w_in_proj.shape[0]
    assert depth == 1, "the input LayerNorm is fused into the first (only) layer's projection kernel"
    out = _layer(True, x.reshape(B * S, D), mem, B, S, ln_in_g, ln_in_b, w_in_proj[0], b_gate[0],
                 gqa_q_norm[0], gqa_k_norm[0], mla_q_norm[0], mla_kv_norm[0], w_mla_qb[0], w_mla_kvb[0],
                 w_mem_kv[0], w_br_gqa[0], w_br_mla[0], w_br_mem[0], w_out[0], ln1_g[0], ln1_b[0],
                 w_router[0], b_router[0], w_exp_in[0], b_exp_in[0], w_exp_out[0], b_exp_out[0],
                 ln2_g[0], ln2_b[0])
    return out.reshape(B, S, D)
```

```python
import functools

import jax
import jax.numpy as jnp
import numpy as np
from jax import lax
from jax.experimental import pallas as pl
from jax.experimental.pallas import tpu as pltpu

D_MODEL = 1024
MEM_LEN = 256
GRID_W = 64
ROPE_THETA = 10000.0
RMS_EPS = 1e-6
LN_EPS = 1e-5

GQA_HEADS = 8
GQA_KV_HEADS = 2
GQA_HEAD_DIM = 64
MLA_HEADS = 8
MLA_NOPE_DIM = 64
MLA_ROPE_DIM = 32
MLA_V_DIM = 64
MLA_Q_LORA = 384
MLA_KV_LORA = 256
MEM_HEADS = 4
MEM_HEAD_DIM = 128

N_EXPERTS = 32
TOP_K = 4
SWIGLU_LIMIT = 7.0
SWIGLU_ALPHA = 1.702
DEEPNORM_ALPHA = 2.0 ** 0.25

LANES = 128
SUBLANES = 8
BF16_SUBLANES = 16
HALF = LANES // 2
NEG_INF = float("-inf")
LOG2_E = 1.4426950408889634

OFF_QA, OFF_KA, OFF_VA, OFF_QLAT, OFF_KVLAT, OFF_KROPE, OFF_QM, OFF_GATE = (
    0, 512, 640, 768, 1152, 1408, 1440, 1952)
IN_PROJ_W = 5024
A_QA, A_KA, A_VA, A_QLAT, A_KVLAT, A_KROPE, A_QM, A_END = 0, 512, 640, 768, 1152, 1408, 1536, 2048
ROW_TILE = 256
IN_TILE = 512
IN_CHUNK = 256
MIX_TILE = 1024
MIX_CHUNK = 256
MIX_COLS = 256
ATTN_Q_TILE = 256
ATTN_K_CHUNK = 512
ROUTE_STEP_TILES = 4
EXPERT_BLOCK = 256
VMEM_LIMIT = 48 * 1024 * 1024
EXPERTS_VMEM_LIMIT = 56 * 1024 * 1024


def _cparams(*sem):
    return pltpu.CompilerParams(dimension_semantics=sem, vmem_limit_bytes=VMEM_LIMIT)


def _lane_iota(shape):
    return lax.broadcasted_iota(jnp.int32, shape, len(shape) - 1)


def _layer_norm(x, g, b):
    mu = jnp.mean(x, axis=-1, keepdims=True)
    xc = x - mu
    var = jnp.mean(xc * xc, axis=-1, keepdims=True)
    return xc * lax.rsqrt(var + LN_EPS) * g + b


def _bdot(a, b):
    return jnp.dot(a.astype(jnp.bfloat16), b.astype(jnp.bfloat16), preferred_element_type=jnp.float32)


def _rope(x, cos, sin_signed, pair):
    lane = _lane_iota(x.shape)
    fwd = pltpu.roll(x, LANES - pair, 1)
    bwd = pltpu.roll(x, pair, 1)
    partner = jnp.where((lane % (2 * pair)) < pair, fwd, bwd)
    return x * cos + partner * sin_signed


def _half_rms_scale(x):
    lane = _lane_iota(x.shape)
    sq = x * x
    lo = jnp.sum(jnp.where(lane < HALF, sq, 0.0), axis=-1, keepdims=True)
    hi = jnp.sum(jnp.where(lane >= HALF, sq, 0.0), axis=-1, keepdims=True)
    inv = 1.0 / GQA_HEAD_DIM
    return jnp.where(lane < HALF, lax.rsqrt(lo * inv + RMS_EPS), lax.rsqrt(hi * inv + RMS_EPS))


def _softmax_rows(s):
    m = jnp.max(s, axis=-1, keepdims=True)
    p = jnp.exp(s - m)
    return p, jnp.sum(p, axis=-1, keepdims=True)


def _mem_kv_kernel(mem_ref, w_ref, kt_ref, v_ref):
    kv = _bdot(mem_ref[0], w_ref[...])
    width = MEM_HEADS * MEM_HEAD_DIM
    kt_ref[0] = kv[:, :width].T.astype(jnp.bfloat16)
    v_ref[0] = kv[:, width:].astype(jnp.bfloat16)


def _mem_kv(mem, w_mem_kv):
    B, M, D = mem.shape
    width = MEM_HEADS * MEM_HEAD_DIM
    return pl.pallas_call(
        _mem_kv_kernel,
        grid=(B,),
        in_specs=[pl.BlockSpec((1, M, D), lambda b: (b, 0, 0)),
                  pl.BlockSpec((D, 2 * width), lambda b: (0, 0))],
        out_specs=[pl.BlockSpec((1, width, M), lambda b: (b, 0, 0)),
                   pl.BlockSpec((1, M, width), lambda b: (b, 0, 0))],
        out_shape=[jax.ShapeDtypeStruct((B, width, M), jnp.bfloat16),
                   jax.ShapeDtypeStruct((B, M, width), jnp.bfloat16)],
        compiler_params=_cparams("arbitrary"),
        name="mem_kv",
    )(mem, w_mem_kv)


def _in_proj_kernel(x_ref, lng_ref, lnb_ref, wa_ref, gq_ref, gk_ref, gql_ref, gkvl_ref,
                    wqb_ref, wkvk_ref, wkvv_ref, ca_ref, sa_ref, cb_ref, sb_ref, mkt_ref, mv_ref,
                    h_ref, qa_ref, ka_ref, va_ref, qb_ref, kb_ref, vb_ref, oc_ref):
    lane = _lane_iota((IN_CHUNK, LANES))
    lo_mask = lane < HALF
    q_scale = GQA_HEAD_DIM ** -0.5 * LOG2_E
    qb_scale = (MLA_NOPE_DIM + MLA_ROPE_DIM) ** -0.5 * LOG2_E
    per_group = GQA_HEADS // GQA_KV_HEADS
    gq, gk = gq_ref[...], gk_ref[...]

    def project(r):
        h = _layer_norm(x_ref[r, :], lng_ref[...], lnb_ref[...])
        h_ref[r, :] = h
        return _bdot(h, wa_ref[...])

    def finish(r, proj):
        ca, sa, cb, sb = ca_ref[r, :], sa_ref[r, :], cb_ref[r, :], sb_ref[r, :]

        for c in range(GQA_HEADS // 2):
            slab = proj[:, A_QA + c * LANES:A_QA + (c + 1) * LANES]
            slab = _rope(slab * _half_rms_scale(slab) * gq, ca, sa, 16) * q_scale
            swapped = pltpu.roll(slab, HALF, 1)
            for half in range(2):
                head = 2 * c + half
                group_lo = (head // per_group) == 0
                src = slab if (half == 0) == group_lo else swapped
                keep = lo_mask if group_lo else jnp.logical_not(lo_mask)
                qa_ref[0, head * LANES:(head + 1) * LANES, r] = jnp.where(keep, src, 0.0).astype(jnp.bfloat16).T

        ka = proj[:, A_KA:A_KA + LANES]
        ka_ref[r, :] = _rope(ka * _half_rms_scale(ka) * gk, ca, sa, 16).astype(jnp.bfloat16)
        va_ref[0, :, r] = proj[:, A_VA:A_VA + LANES].astype(jnp.bfloat16).T

        ql = proj[:, A_QLAT:A_KVLAT]
        ql = ql * lax.rsqrt(jnp.mean(ql * ql, axis=-1, keepdims=True) + RMS_EPS) * gql_ref[...]
        qm = _bdot(ql, wqb_ref[...])
        kvl = proj[:, A_KVLAT:A_KROPE]
        kvl = kvl * lax.rsqrt(jnp.mean(kvl * kvl, axis=-1, keepdims=True) + RMS_EPS) * gkvl_ref[...]
        kn = _bdot(kvl, wkvk_ref[...])
        vb = _bdot(kvl, wkvv_ref[...])
        for c in range(vb.shape[1] // LANES):
            sl = slice(c * LANES, (c + 1) * LANES)
            vb_ref[0, sl, r] = vb[:, sl].astype(jnp.bfloat16).T
        k_pe = _rope(proj[:, A_KROPE:A_QM], cb, sb, 8)
        for hd in range(MLA_HEADS):
            sl = slice(hd * LANES, (hd + 1) * LANES)
            partner = qm[:, (MLA_HEADS + hd) * LANES:(MLA_HEADS + hd + 1) * LANES]
            qb_ref[0, sl, r] = ((qm[:, sl] * cb + partner * sb) * qb_scale).astype(jnp.bfloat16).T
            kb_ref[r, sl] = (kn[:, sl] + k_pe).astype(jnp.bfloat16)

        qc = proj[:, A_QM:A_END] * (MEM_HEAD_DIM ** -0.5)
        for hd in range(MEM_HEADS):
            sl = slice(hd * MEM_HEAD_DIM, (hd + 1) * MEM_HEAD_DIM)
            p, l = _softmax_rows(_bdot(qc[:, sl], mkt_ref[0, sl, :]))
            oc_ref[r, sl] = (_bdot(p, mv_ref[0, :, sl]) / l).astype(jnp.bfloat16)

    chunks = [slice(s * IN_CHUNK, (s + 1) * IN_CHUNK) for s in range(x_ref.shape[0] // IN_CHUNK)]
    pending = None
    for r in chunks:
        proj = project(r)
        if pending is not None:
            finish(*pending)
        pending = (r, proj)
    finish(*pending)


def _in_proj(x2, ln_g, ln_b, wa, gq, gk, gql, gkvl, wqb, wkvk, wkvv, ca, sa, cb, sb, mkt, mv, B, S):
    N, D = x2.shape
    T = IN_TILE
    tiles_per_seq = S // T
    row = lambda i: (i, 0)
    const = lambda i: (0, 0)
    pos = lambda i: (i % tiles_per_seq, 0)
    batch3 = lambda i: (i // tiles_per_seq, 0, 0)
    kt_map = lambda i: (i // tiles_per_seq, 0, i % tiles_per_seq)
    full = lambda a: pl.BlockSpec(a.shape, const)
    return pl.pallas_call(
        _in_proj_kernel,
        grid=(N // T,),
        in_specs=[pl.BlockSpec((T, D), row), full(ln_g), full(ln_b), full(wa), full(gq), full(gk),
                  full(gql), full(gkvl), full(wqb), full(wkvk), full(wkvv),
                  pl.BlockSpec((T, LANES), pos), pl.BlockSpec((T, LANES), pos),
                  pl.BlockSpec((T, LANES), pos), pl.BlockSpec((T, LANES), pos),
                  pl.BlockSpec((1,) + mkt.shape[1:], batch3), pl.BlockSpec((1,) + mv.shape[1:], batch3)],
        out_specs=[pl.BlockSpec((T, D), row),
                   pl.BlockSpec((1, GQA_HEADS * LANES, T), kt_map),
                   pl.BlockSpec((T, LANES), row),
                   pl.BlockSpec((1, LANES, T), kt_map),
                   pl.BlockSpec((1, MLA_HEADS * LANES, T), kt_map),
                   pl.BlockSpec((T, MLA_HEADS * LANES), row),
                   pl.BlockSpec((1, MLA_HEADS * MLA_V_DIM, T), kt_map),
                   pl.BlockSpec((T, MEM_HEADS * MEM_HEAD_DIM), row)],
        out_shape=[jax.ShapeDtypeStruct((N, D), jnp.float32),
                   jax.ShapeDtypeStruct((B, GQA_HEADS * LANES, S), jnp.bfloat16),
                   jax.ShapeDtypeStruct((N, LANES), jnp.bfloat16),
                   jax.ShapeDtypeStruct((B, LANES, S), jnp.bfloat16),
                   jax.ShapeDtypeStruct((B, MLA_HEADS * LANES, S), jnp.bfloat16),
                   jax.ShapeDtypeStruct((N, MLA_HEADS * LANES), jnp.bfloat16),
                   jax.ShapeDtypeStruct((B, MLA_HEADS * MLA_V_DIM, S), jnp.bfloat16),
                   jax.ShapeDtypeStruct((N, MEM_HEADS * MEM_HEAD_DIM), jnp.bfloat16)],
        compiler_params=_cparams("arbitrary"),
        name="in_proj",
    )(x2, ln_g, ln_b, wa, gq, gk, gql, gkvl, wqb, wkvk, wkvv, ca, sa, cb, sb, mkt, mv)


def _attention_kernel(qt_ref, k_ref, vt_ref, o_ref, s_even, s_odd, *, shared_kv):
    t = pl.program_id(0)

    @pl.when(t == 0)
    def _():
        s_odd[...] = jnp.zeros_like(s_odd)

    def step(s_new, s_old):
        S, TQ = s_new.shape[1:]
        m = [jnp.full((1, TQ), NEG_INF, jnp.float32)] * 2
        ones = jnp.ones((BF16_SUBLANES, ATTN_K_CHUNK), jnp.bfloat16)
        acc = [jnp.zeros((HALF + BF16_SUBLANES, TQ), jnp.float32)] * 2
        for c in range(S // ATTN_K_CHUNK):
            rows = slice(c * ATTN_K_CHUNK, (c + 1) * ATTN_K_CHUNK)
            for hd in range(2):
                qt = qt_ref[0, hd * LANES:(hd + 1) * LANES, :]
                k = k_ref[0, rows, :] if shared_kv else k_ref[0, rows, hd * LANES:(hd + 1) * LANES]
                s_new[hd, rows, :] = jnp.dot(k, qt, preferred_element_type=jnp.float32)
                vt = vt_ref[0, :, rows] if shared_kv else vt_ref[0, hd * HALF:(hd + 1) * HALF, rows]
                vt = jnp.concatenate([vt, ones], axis=0)
                sc = s_old[hd, rows, :]
                m_new = jnp.maximum(m[hd], jnp.max(sc, axis=0, keepdims=True))
                p = jnp.exp2(sc - m_new).astype(jnp.bfloat16)
                corr = jnp.exp2(m[hd] - m_new)
                acc[hd] = acc[hd] * corr + jnp.dot(vt, p, preferred_element_type=jnp.float32)
                m[hd] = m_new
        outs = [acc[hd][:HALF] / acc[hd][HALF:HALF + 1] for hd in range(2)]
        o_ref[0] = jnp.concatenate(outs, axis=0).T.astype(jnp.bfloat16)

    @pl.when(t % 2 == 0)
    def _():
        step(s_even, s_odd)

    @pl.when(t % 2 == 1)
    def _():
        step(s_odd, s_even)


def _attention(qt, k, vt, *, shared_kv):
    B, qh, S = qt.shape
    pairs = qh // (2 * LANES)
    TQ = ATTN_Q_TILE
    nq = S // TQ
    items = B * pairs * nq
    pairs_per_group = pairs // GQA_KV_HEADS

    def item(t):
        return t // (pairs * nq), (t // nq) % pairs, t % nq

    score_item = lambda t: item(jnp.minimum(t, items - 1))
    finish_item = lambda t: item(jnp.maximum(t - 1, 0))

    def qt_map(t):
        b, j, i = score_item(t)
        return b, j, i

    def k_map(t):
        b, j, _ = score_item(t)
        return (b, 0, 0) if shared_kv else (b, 0, j)

    def vt_map(t):
        b, j, _ = finish_item(t)
        return (b, j // pairs_per_group, 0) if shared_kv else (b, j, 0)

    def o_map(t):
        b, j, i = finish_item(t)
        return b, i, j

    k_spec = pl.BlockSpec((1, S, LANES if shared_kv else 2 * LANES), k_map)
    vt_spec = pl.BlockSpec((1, HALF if shared_kv else LANES, S), vt_map)
    return pl.pallas_call(
        functools.partial(_attention_kernel, shared_kv=shared_kv),
        grid=(items + 1,),
        in_specs=[pl.BlockSpec((1, 2 * LANES, TQ), qt_map), k_spec, vt_spec],
        out_specs=pl.BlockSpec((1, TQ, LANES), o_map),
        out_shape=jax.ShapeDtypeStruct((B, S, pairs * LANES), jnp.bfloat16),
        scratch_shapes=[pltpu.VMEM((2, S, TQ), jnp.float32), pltpu.VMEM((2, S, TQ), jnp.float32)],
        compiler_params=_cparams("arbitrary"),
        name="attention_gqa" if shared_kv else "attention_mla",
    )(qt, k, vt)


def _split2(x):
    hi = x.astype(jnp.bfloat16)
    return hi, (x - hi.astype(jnp.float32)).astype(jnp.bfloat16)


def _mix_out_kernel(h_ref, oa_ref, ob_ref, oc_ref, wg_ref, bg_ref, wa_ref, wb_ref, wc_ref, wo_ref,
                    g_ref, b_ref, wr_ref, br_ref, h1_ref, idx_ref, gate_ref):
    D = D_MODEL
    dot = lambda a, b: jnp.dot(a, b, preferred_element_type=jnp.float32)

    def mix(r):
        hb = h_ref[r, :].astype(jnp.bfloat16)
        merged = []
        for n in range(D // MIX_COLS):
            acc = None
            for i, (o_ref, w_ref) in enumerate(((oa_ref, wa_ref), (ob_ref, wb_ref), (oc_ref, wc_ref))):
                gate_cols = slice(i * D + n * MIX_COLS, i * D + (n + 1) * MIX_COLS)
                gate = jax.nn.sigmoid(dot(hb, wg_ref[:, gate_cols]) + bg_ref[:, gate_cols])
                term = gate * dot(o_ref[r, :], w_ref[:, n * MIX_COLS:(n + 1) * MIX_COLS])
                acc = term if acc is None else acc + term
            merged.append(acc.astype(jnp.bfloat16))
        return dot(jnp.concatenate(merged, axis=1), wo_ref[...])

    def route(r, mixed):
        h1 = _layer_norm(DEEPNORM_ALPHA * h_ref[r, :] + mixed, g_ref[...], b_ref[...])
        h1_ref[r, :] = h1
        h_hi, h_lo = _split2(h1)
        w_hi, w_lo = wr_ref[0], wr_ref[1]
        logits = dot(h_hi, w_hi) + (dot(h_hi, w_lo) + dot(h_lo, w_hi)) + br_ref[...]
        lane = _lane_iota(logits.shape)
        logits = jnp.where(lane < N_EXPERTS, logits, NEG_INF)
        idx_out = jnp.zeros(logits.shape, jnp.int32)
        val_out = jnp.zeros(logits.shape, jnp.float32)
        top = None
        for k in range(TOP_K):
            m = jnp.max(logits, axis=-1, keepdims=True)
            idx = jnp.min(jnp.where(logits == m, lane, LANES), axis=-1, keepdims=True)
            logits = jnp.where(lane == idx, NEG_INF, logits)
            top = m if top is None else top
            idx_out = jnp.where(lane == k, idx, idx_out)
            val_out = jnp.where(lane == k, jnp.exp(m - top), val_out)
        idx_ref[r, :] = idx_out
        gate_ref[r, :] = val_out / jnp.sum(val_out, axis=-1, keepdims=True)

    pending = None
    for s in range(h_ref.shape[0] // MIX_CHUNK):
        r = slice(s * MIX_CHUNK, (s + 1) * MIX_CHUNK)
        mixed = mix(r)
        if pending is not None:
            route(*pending)
        pending = (r, mixed)
    route(*pending)


def _mix_out(h, oa, ob, oc, wg, bg, wa, wb, wc, wo, g, b, wr3, br):
    N, D = h.shape
    T = MIX_TILE
    row = lambda i: (i, 0)
    full = lambda a: pl.BlockSpec(a.shape, lambda i: (0,) * a.ndim)
    return pl.pallas_call(
        _mix_out_kernel,
        grid=(N // T,),
        in_specs=[pl.BlockSpec((T, D), row), pl.BlockSpec((T, oa.shape[1]), row),
                  pl.BlockSpec((T, ob.shape[1]), row), pl.BlockSpec((T, oc.shape[1]), row),
                  full(wg), full(bg), full(wa), full(wb), full(wc), full(wo), full(g), full(b),
                  full(wr3), full(br)],
        out_specs=[pl.BlockSpec((T, D), row), pl.BlockSpec((T, LANES), row), pl.BlockSpec((T, LANES), row)],
        out_shape=[jax.ShapeDtypeStruct((N, D), jnp.float32),
                   jax.ShapeDtypeStruct((N, LANES), jnp.int32),
                   jax.ShapeDtypeStruct((N, LANES), jnp.float32)],
        compiler_params=_cparams("arbitrary"),
        name="mix_out",
    )(h, oa, ob, oc, wg, bg, wa, wb, wc, wo, g, b, wr3, br)


def _expert_rows(idx, rows_shape):
    T = idx.shape[0]
    lane = _lane_iota(idx.shape)
    sel = [lane == idx[:, k:k + 1] for k in range(TOP_K)]
    onehot = sum(s.astype(jnp.float32) for s in sel)
    r = lax.broadcasted_iota(jnp.int32, (T, T), 0)
    c = lax.broadcasted_iota(jnp.int32, (T, T), 1)
    before = (c < r).astype(jnp.bfloat16)
    prefix = jnp.dot(before, onehot.astype(jnp.bfloat16), preferred_element_type=jnp.float32)
    cnt = jnp.broadcast_to(jnp.sum(onehot, axis=0, keepdims=True), rows_shape)
    cnt = jnp.floor((cnt + (SUBLANES - 1)) * (1.0 / SUBLANES)) * SUBLANES
    er = lax.broadcasted_iota(jnp.int32, (LANES, LANES), 0)
    ec = lax.broadcasted_iota(jnp.int32, (LANES, LANES), 1)
    local_start = jnp.dot(cnt.astype(jnp.bfloat16), (er < ec).astype(jnp.bfloat16),
                          preferred_element_type=jnp.float32)
    target = prefix + local_start[0:1, :]
    out = jnp.zeros(idx.shape, jnp.int32)
    for k in range(TOP_K):
        pk = jnp.sum(jnp.where(sel[k], target, 0.0), axis=-1, keepdims=True)
        out = jnp.where(lane == k, pk.astype(jnp.int32), out)
    return out, cnt


def _route_kernel(idx_ref, pos_ref, carry_out_ref, cnt_out_ref, total_ref, carry_ref):
    @pl.when(pl.program_id(0) == 0)
    def _():
        carry_ref[...] = jnp.zeros_like(carry_ref)

    for s in range(idx_ref.shape[0] // ROW_TILE):
        rows = slice(s * ROW_TILE, (s + 1) * ROW_TILE)
        stats = slice(s * SUBLANES, (s + 1) * SUBLANES)
        pos, cnt = _expert_rows(idx_ref[rows, :], carry_ref.shape)
        pos_ref[rows, :] = pos
        carry_out_ref[stats, :] = carry_ref[...].astype(jnp.int32)
        cnt_out_ref[stats, :] = cnt.astype(jnp.int32)
        carry_ref[...] = carry_ref[...] + cnt
    total_ref[...] = carry_ref[...].astype(jnp.int32)


def _route(idx):
    N = idx.shape[0]
    T = ROUTE_STEP_TILES * ROW_TILE
    n_tiles = N // ROW_TILE
    tile8 = pl.BlockSpec((ROUTE_STEP_TILES * SUBLANES, LANES), lambda i: (i, 0))
    return pl.pallas_call(
        _route_kernel,
        grid=(N // T,),
        in_specs=[pl.BlockSpec((T, LANES), lambda i: (i, 0))],
        out_specs=[pl.BlockSpec((T, LANES), lambda i: (i, 0)), tile8, tile8,
                   pl.BlockSpec((SUBLANES, LANES), lambda i: (0, 0))],
        out_shape=[jax.ShapeDtypeStruct((N, LANES), jnp.int32),
                   jax.ShapeDtypeStruct((n_tiles * SUBLANES, LANES), jnp.int32),
                   jax.ShapeDtypeStruct((n_tiles * SUBLANES, LANES), jnp.int32),
                   jax.ShapeDtypeStruct((SUBLANES, LANES), jnp.int32)],
        scratch_shapes=[pltpu.VMEM((SUBLANES, LANES), jnp.float32)],
        compiler_params=_cparams("arbitrary"),
        name="route",
    )(idx)


LOCAL_ROWS = TOP_K * ROW_TILE + ROW_TILE
STRIP_LOOP_UNROLL = 4
LOCAL_CHUNK = 256
STRIP_BUFFERS = 3
assert LOCAL_ROWS >= TOP_K * ROW_TILE + N_EXPERTS * (SUBLANES - 1)


HI16 = 0xFFFF0000


def _pack_rows(x):
    half = x.shape[1] // 2
    lo = lax.bitcast_convert_type(x[:, :half], jnp.uint32) >> 16
    hi = lax.bitcast_convert_type(x[:, half:], jnp.uint32) & jnp.uint32(HI16)
    return lo | hi


def _unpack_rows(w):
    lo = lax.bitcast_convert_type(w << 16, jnp.float32)
    hi = lax.bitcast_convert_type(w & jnp.uint32(HI16), jnp.float32)
    return jnp.concatenate([lo, hi], axis=1).astype(jnp.bfloat16)


def _round_bf16(x):
    return x.astype(jnp.bfloat16).astype(jnp.float32)


def _for_each_strip(cnt_ref, base_ref, tile, fn):
    def body(e, off):
        c = cnt_ref[tile * N_EXPERTS + e]
        d = base_ref[tile * N_EXPERTS + e]

        @pl.when(c > 0)
        def _():
            fn(pl.multiple_of(off, SUBLANES), pl.multiple_of(d, SUBLANES), pl.multiple_of(c, SUBLANES))
        return off + c

    lax.fori_loop(0, N_EXPERTS, body, 0, unroll=STRIP_LOOP_UNROLL)


def _wait_tile_strips(cnt_ref, tile, make_copy):
    rows = cnt_ref[pl.num_programs(0) * N_EXPERTS + tile]

    @pl.when(rows > 0)
    def _():
        make_copy(0, 0, pl.multiple_of(rows, SUBLANES)).wait()


def _dispatch_kernel(cnt_ref, base_ref, tail_ref, pos_ref, h_ref, xs_ref, buf, zeros, sems, zsem):
    i = pl.program_id(0)
    last = pl.num_programs(0) - 1
    T = h_ref.shape[0]
    cur = i % STRIP_BUFFERS

    @pl.when(i == 0)
    def _():
        zeros[...] = jnp.zeros_like(zeros)

        def tail_copy(e):
            row = pl.multiple_of(jnp.maximum(tail_ref[e], 0), SUBLANES)
            return pltpu.make_async_copy(zeros, xs_ref.at[pl.ds(row, EXPERT_BLOCK), :], zsem)

        def spare_copy(blk):
            row = pl.multiple_of(blk * EXPERT_BLOCK, EXPERT_BLOCK)
            return pltpu.make_async_copy(zeros, xs_ref.at[pl.ds(row, EXPERT_BLOCK), :], zsem)

        first_spare = tail_ref[N_EXPERTS] // EXPERT_BLOCK
        for act in ("start", "wait"):
            def body(e, carry, act=act):
                @pl.when(tail_ref[e] >= 0)
                def _():
                    getattr(tail_copy(e), act)()
                return carry
            lax.fori_loop(0, N_EXPERTS, body, 0)

            def spare(blk, carry, act=act):
                getattr(spare_copy(blk), act)()
                return carry
            lax.fori_loop(first_spare, xs_ref.shape[0] // EXPERT_BLOCK, spare, 0)

    pos_t = pos_ref[...].astype(jnp.float32).T.astype(jnp.int32)
    hb = h_ref[...].astype(jnp.bfloat16)
    for c in range(LOCAL_ROWS // LOCAL_CHUNK):
        rows = lax.broadcasted_iota(jnp.int32, (LOCAL_CHUNK, T), 0) + c * LOCAL_CHUNK
        hit = rows == pos_t[0:1, :]
        for k in range(1, TOP_K):
            hit = jnp.logical_or(hit, rows == pos_t[k:k + 1, :])
        buf[cur, c * LOCAL_CHUNK:(c + 1) * LOCAL_CHUNK, :] = _pack_rows(
            jnp.dot(hit.astype(jnp.bfloat16), hb, preferred_element_type=jnp.float32))

    def strip(tile):
        slot = tile % STRIP_BUFFERS

        def make(local_row, slot_row, rows):
            return pltpu.make_async_copy(buf.at[slot, pl.ds(local_row, rows), :],
                                         xs_ref.at[pl.ds(slot_row, rows), :], sems.at[slot])
        return make

    def drain(tile):
        _wait_tile_strips(cnt_ref, tile, strip(tile))

    @pl.when(i >= STRIP_BUFFERS - 1)
    def _():
        drain(i - (STRIP_BUFFERS - 1))

    _for_each_strip(cnt_ref, base_ref, i, lambda *a: strip(i)(*a).start())

    @pl.when(i == last)
    def _():
        for back in range(STRIP_BUFFERS - 2, -1, -1):
            @pl.when(i - back >= 0)
            def _(back=back):
                drain(i - back)


def _dispatch(cnt_flat, base_flat, tail, pos, h1, n_slots):
    N, D = h1.shape
    T = ROW_TILE
    return pl.pallas_call(
        _dispatch_kernel,
        grid_spec=pltpu.PrefetchScalarGridSpec(
            num_scalar_prefetch=3,
            grid=(N // T,),
            in_specs=[pl.BlockSpec((T, LANES), lambda i, c, b, t: (i, 0)),
                      pl.BlockSpec((T, D), lambda i, c, b, t: (i, 0))],
            out_specs=pl.BlockSpec(memory_space=pl.ANY),
            scratch_shapes=[pltpu.VMEM((STRIP_BUFFERS, LOCAL_ROWS, D // 2), jnp.uint32),
                            pltpu.VMEM((EXPERT_BLOCK, D // 2), jnp.uint32),
                            pltpu.SemaphoreType.DMA((STRIP_BUFFERS,)),
                            pltpu.SemaphoreType.DMA(())]),
        out_shape=jax.ShapeDtypeStruct((n_slots, D // 2), jnp.uint32),
        compiler_params=_cparams("arbitrary"),
        name="dispatch",
    )(cnt_flat, base_flat, tail, pos, h1)


def _experts_kernel(blk_e_ref, n_used_ref, x_ref, wi_ref, bi_ref, wo_ref, bo_ref, y_ref, wi_bf, wo_bf):
    i = pl.program_id(0)
    used = i < n_used_ref[0]
    new_expert = jnp.logical_or(i == 0, blk_e_ref[i] != blk_e_ref[jnp.maximum(i - 1, 0)])

    @pl.when(jnp.logical_and(used, new_expert))
    def _():
        wi_bf[...] = wi_ref[0].astype(jnp.bfloat16)
        wo_bf[...] = wo_ref[0].astype(jnp.bfloat16)

    @pl.when(used)
    def _():
        hb = jnp.dot(_unpack_rows(x_ref[...]), wi_bf[...], preferred_element_type=jnp.float32) + bi_ref[0]
        De = D_MODEL
        x_glu = jnp.minimum(hb[:, :De], SWIGLU_LIMIT)
        x_lin = jnp.clip(hb[:, De:], -SWIGLU_LIMIT, SWIGLU_LIMIT)
        act = x_glu * jax.nn.sigmoid(SWIGLU_ALPHA * x_glu) * (x_lin + 1.0)
        y = jnp.dot(act.astype(jnp.bfloat16), wo_bf[...], preferred_element_type=jnp.float32) + bo_ref[0]
        y_ref[...] = _pack_rows(_round_bf16(y))

    @pl.when(jnp.logical_not(used))
    def _():
        y_ref[...] = jnp.zeros_like(y_ref)


def _experts(blk_e, n_used, xs, w_in, b_in, w_out, b_out):
    n_slots, half = xs.shape
    R = EXPERT_BLOCK
    E, D, F = w_in.shape
    return pl.pallas_call(
        _experts_kernel,
        grid_spec=pltpu.PrefetchScalarGridSpec(
            num_scalar_prefetch=2,
            grid=(n_slots // R,),
            in_specs=[pl.BlockSpec((R, half), lambda i, e, n: (jnp.minimum(i, n[0] - 1), 0)),
                      pl.BlockSpec((1, D, F), lambda i, e, n: (e[i], 0, 0)),
                      pl.BlockSpec((1, 1, F), lambda i, e, n: (e[i], 0, 0)),
                      pl.BlockSpec((1, F // 2, D), lambda i, e, n: (e[i], 0, 0)),
                      pl.BlockSpec((1, 1, D), lambda i, e, n: (e[i], 0, 0))],
            out_specs=pl.BlockSpec((R, half), lambda i, e, n: (i, 0)),
            scratch_shapes=[pltpu.VMEM((D, F), jnp.bfloat16), pltpu.VMEM((F // 2, D), jnp.bfloat16)]),
        out_shape=jax.ShapeDtypeStruct((n_slots, half), jnp.uint32),
        compiler_params=pltpu.CompilerParams(dimension_semantics=("arbitrary",),
                                             vmem_limit_bytes=EXPERTS_VMEM_LIMIT),
        name="experts",
    )(blk_e, n_used, xs, w_in, b_in, w_out, b_out)


def _combine_kernel(cnt_ref, base_ref, pos_ref, gate_ref, h_ref, g_ref, b_ref, ys_ref, o_ref, buf, sems):
    i = pl.program_id(0)
    last = pl.num_programs(0) - 1
    T = h_ref.shape[0]
    cur = i % STRIP_BUFFERS

    def strip(tile):
        slot = tile % STRIP_BUFFERS

        def make(local_row, slot_row, rows):
            return pltpu.make_async_copy(ys_ref.at[pl.ds(slot_row, rows), :],
                                         buf.at[slot, pl.ds(local_row, rows), :], sems.at[slot])
        return make

    def fetch(tile):
        _for_each_strip(cnt_ref, base_ref, tile, lambda *a: strip(tile)(*a).start())

    @pl.when(i == 0)
    def _():
        buf[...] = jnp.zeros_like(buf)
        for ahead in range(STRIP_BUFFERS - 1):
            @pl.when(ahead <= last)
            def _(ahead=ahead):
                fetch(ahead)

    @pl.when(i + (STRIP_BUFFERS - 1) <= last)
    def _():
        fetch(i + (STRIP_BUFFERS - 1))

    _wait_tile_strips(cnt_ref, i, strip(i))

    pos, gate = pos_ref[...], gate_ref[...]
    dot = lambda a, b: jnp.dot(a, b, preferred_element_type=jnp.float32)
    ffn = None
    for c in range(LOCAL_ROWS // LOCAL_CHUNK):
        col = lax.broadcasted_iota(jnp.int32, (T, LOCAL_CHUNK), 1) + c * LOCAL_CHUNK
        u = jnp.zeros(col.shape, jnp.float32)
        for k in range(TOP_K):
            u = jnp.where(col == pos[:, k:k + 1], gate[:, k:k + 1], u)
        u_hi, u_lo = _split2(u)
        y = _unpack_rows(buf[cur, c * LOCAL_CHUNK:(c + 1) * LOCAL_CHUNK, :])
        part = dot(u_hi, y) + dot(u_lo, y)
        ffn = part if ffn is None else ffn + part
    o_ref[...] = _layer_norm(DEEPNORM_ALPHA * h_ref[...] + ffn, g_ref[...], b_ref[...])


def _combine(cnt_flat, base_flat, pos, gate, h1, g, b, ys):
    N, D = h1.shape
    T = ROW_TILE
    row = lambda i, c, s: (i, 0)
    const = lambda i, c, s: (0, 0)
    return pl.pallas_call(
        _combine_kernel,
        grid_spec=pltpu.PrefetchScalarGridSpec(
            num_scalar_prefetch=2,
            grid=(N // T,),
            in_specs=[pl.BlockSpec((T, LANES), row), pl.BlockSpec((T, LANES), row), pl.BlockSpec((T, D), row),
                      pl.BlockSpec(g.shape, const), pl.BlockSpec(b.shape, const),
                      pl.BlockSpec(memory_space=pl.ANY)],
            out_specs=pl.BlockSpec((T, D), row),
            scratch_shapes=[pltpu.VMEM((STRIP_BUFFERS, LOCAL_ROWS, D // 2), jnp.uint32),
                            pltpu.SemaphoreType.DMA((STRIP_BUFFERS,))]),
        out_shape=jax.ShapeDtypeStruct((N, D), jnp.float32),
        compiler_params=_cparams("arbitrary"),
        name="combine",
    )(cnt_flat, base_flat, pos, gate, h1, g, b, ys)


def _rope_tables(S):
    t = np.arange(S)
    row, col = (t // GRID_W).astype(np.float64), (t % GRID_W).astype(np.float64)

    def block(half):
        inv = ROPE_THETA ** (-np.arange(half, dtype=np.float64) / half)
        ar, ac = row[:, None] * inv[None, :], col[:, None] * inv[None, :]
        cos = np.concatenate([np.cos(ar), np.cos(ar), np.cos(ac), np.cos(ac)], axis=1)
        sin = np.concatenate([-np.sin(ar), np.sin(ar), -np.sin(ac), np.sin(ac)], axis=1)
        return cos, sin

    ca64, sa64 = block(GQA_HEAD_DIM // 4)
    ca, sa = np.tile(ca64, (1, 2)), np.tile(sa64, (1, 2))
    cb32, sb32 = block(MLA_ROPE_DIM // 4)
    ones, zeros = np.ones((S, MLA_NOPE_DIM)), np.zeros((S, MLA_NOPE_DIM))
    cb = np.concatenate([ones, cb32, ones[:, :LANES - MLA_NOPE_DIM - MLA_ROPE_DIM]], axis=1)
    sb = np.concatenate([zeros, sb32, zeros[:, :LANES - MLA_NOPE_DIM - MLA_ROPE_DIM]], axis=1)
    return tuple(jnp.asarray(a, jnp.float32) for a in (ca, sa, cb, sb))


def _layer(h_in_is_x, x2, mem, B, S, ln_g, ln_b, w_in_proj, b_gate, gqa_q_norm, gqa_k_norm, mla_q_norm,
           mla_kv_norm, w_mla_qb, w_mla_kvb, w_mem_kv, w_br_gqa, w_br_mla, w_br_mem, w_out,
           ln1_g, ln1_b, w_router, b_router, w_exp_in, b_exp_in, w_exp_out, b_exp_out, ln2_g, ln2_b):
    del h_in_is_x
    bf = jnp.bfloat16
    N, D = x2.shape
    row2 = lambda a: a.reshape(1, -1)

    W = w_in_proj
    zc = lambda n: jnp.zeros((D, n), W.dtype)
    wa = jnp.concatenate([W[:, :OFF_KROPE], zc(HALF), W[:, OFF_KROPE:OFF_QM],
                          zc(LANES - HALF - MLA_ROPE_DIM), W[:, OFF_QM:OFF_GATE]], axis=1).astype(bf)
    wg = W[:, OFF_GATE:].astype(bf)
    qd = MLA_NOPE_DIM + MLA_ROPE_DIM
    wqb = jnp.pad(w_mla_qb.reshape(MLA_Q_LORA, MLA_HEADS, qd), ((0, 0), (0, 0), (0, LANES - qd)))
    partner_lane = np.arange(LANES) ^ (MLA_ROPE_DIM // 4)
    wqb = jnp.concatenate([wqb, wqb[:, :, partner_lane]], axis=1
                          ).reshape(MLA_Q_LORA, 2 * MLA_HEADS * LANES).astype(bf)
    kvb = w_mla_kvb.reshape(MLA_KV_LORA, MLA_HEADS, MLA_NOPE_DIM + MLA_V_DIM)
    wkvk = jnp.pad(kvb[:, :, :MLA_NOPE_DIM], ((0, 0), (0, 0), (0, LANES - MLA_NOPE_DIM))
                   ).reshape(MLA_KV_LORA, MLA_HEADS * LANES).astype(bf)
    wkvv = kvb[:, :, MLA_NOPE_DIM:].reshape(MLA_KV_LORA, MLA_HEADS * MLA_V_DIM).astype(bf)
    gq = row2(jnp.tile(gqa_q_norm, 2))
    gk = row2(jnp.tile(gqa_k_norm, 2))
    ca, sa, cb, sb = _rope_tables(S)

    mkt, mv = _mem_kv(mem, w_mem_kv.astype(bf))
    h, qat, ka, vat, qbt, kb, vbt, oc = _in_proj(
        x2, row2(ln_g), row2(ln_b), wa, gq, gk, row2(mla_q_norm), row2(mla_kv_norm),
        wqb, wkvk, wkvv, ca, sa, cb, sb, mkt, mv, B, S)

    oa = _attention(qat, ka.reshape(B, S, -1), vat, shared_kv=True)
    ob = _attention(qbt, kb.reshape(B, S, -1), vbt, shared_kv=False)

    wr = jnp.pad(w_router, ((0, 0), (0, LANES - N_EXPERTS)))
    wr3 = jnp.stack(_split2(wr))
    br = row2(jnp.pad(b_router, (0, LANES - N_EXPERTS)))
    h1, idx, gate = _mix_out(h, oa.reshape(N, -1), ob.reshape(N, -1), oc, wg, row2(b_gate),
                             w_br_gqa.astype(bf), w_br_mla.astype(bf), w_br_mem.astype(bf),
                             w_out.astype(bf), row2(ln1_g), row2(ln1_b), wr3, br)

    pos, carry_t, cnt_t, total = _route(idx)
    counts = total[0, :N_EXPERTS]
    R = EXPERT_BLOCK
    padded = (counts + R - 1) // R * R
    pad_end = jnp.cumsum(padded)
    pad_start = (pad_end - padded).astype(jnp.int32)
    n_tiles = N // ROW_TILE
    max_rows = N * TOP_K + n_tiles * N_EXPERTS * (SUBLANES - 1) + N_EXPERTS * (R - 1)
    n_blocks = -(-max_rows // R)
    blk_start = jnp.arange(n_blocks, dtype=jnp.int32) * R
    blk_e = jnp.sum((pad_end[None, :] <= blk_start[:, None]).astype(jnp.int32), axis=1)
    n_used = (pad_end[-1:] // R).astype(jnp.int32)
    last_e = jnp.max(jnp.where(blk_start < pad_end[-1], blk_e, 0))
    blk_e = jnp.minimum(blk_e, last_e).astype(jnp.int32)
    per_tile = lambda a: a.reshape(n_tiles, 8, LANES)[:, 0, :N_EXPERTS]
    cnt_flat = jnp.concatenate([per_tile(cnt_t).reshape(-1), jnp.sum(per_tile(cnt_t), axis=1)])
    base_flat = (per_tile(carry_t) + pad_start[None, :]).reshape(-1)
    tail = jnp.concatenate([jnp.where(padded > 0, pad_end - R, -1), pad_end[-1:]]).astype(jnp.int32)

    xs = _dispatch(cnt_flat, base_flat, tail, pos, h1, n_blocks * R)
    ys = _experts(blk_e, n_used, xs, w_exp_in, b_exp_in[:, None, :], w_exp_out, b_exp_out[:, None, :])
    return _combine(cnt_flat, base_flat, pos, gate, h1, row2(ln2_g), row2(ln2_b), ys)


def kernel(x, mem, ln_in_g, ln_in_b, w_in_proj, b_gate, gqa_q_norm, gqa_k_norm, mla_q_norm, mla_kv_norm, w_mla_qb, w_mla_kvb, w_mem_kv, w_br_gqa, w_br_mla, w_br_mem, w_out, ln1_g, ln1_b, w_router, b_router, w_exp_in, b_exp_in, w_exp_out, b_exp_out, ln2_g, ln2_b):
    B, S, D = x.shape
    depth = w_in_proj.shape[0]
    assert depth == 1, "the input LayerNorm is fused into the first (only) layer's projection kernel"
    out = _layer(True, x.reshape(B * S, D), mem, B, S, ln_in_g, ln_in_b, w_in_proj[0], b_gate[0],
                 gqa_q_norm[0], gqa_k_norm[0], mla_q_norm[0], mla_kv_norm[0], w_mla_qb[0], w_mla_kvb[0],
                 w_mem_kv[0], w_br_gqa[0], w_br_mla[0], w_br_mem[0], w_out[0], ln1_g[0], ln1_b[0],
                 w_router[0], b_router[0], w_exp_in[0], b_exp_in[0], w_exp_out[0], b_exp_out[0],
                 ln2_g[0], ln2_b[0])
    return out.reshape(B, S, D)
```
